```python
import jax, jax.numpy as jnp
from jax import lax
import numpy as np

D_MODEL = 2048
BATCH = 4
SEQ = 8192
DEPTH = 4
DEC_BATCH = 8
DEC_SEQ = 4096
PAST_LEN = 128

N_META = 16
D_FF = 4 * D_MODEL
NORM_EPS = 1e-6
ROPE_BASE = 10000.0
N_RET_LAYERS = (DEPTH + 1) // 2
N_MLA_LAYERS = DEPTH // 2
RET_HEADS = 8
RET_DK = 256
RET_DV = 512
RET_CHUNK = 128
MLA_HEADS = 16
MLA_Q_LORA = 512
MLA_KV_LORA = 512
MLA_NOPE = 128
MLA_ROPE = 64
MLA_V = 128
MLA_QBLOCK = 128

kernel_name = "hybrid_retention_mla_encoder"


def rms_norm(x, g):
    xf = x.astype(jnp.float32)
    y = xf * lax.rsqrt(jnp.mean(xf * xf, axis=-1, keepdims=True) + NORM_EPS)
    return (y * g.astype(jnp.float32)).astype(x.dtype)


def rope_tables(n, dim):
    inv = 1.0 / (ROPE_BASE ** (jnp.arange(0, dim, 2, dtype=jnp.float32) / dim))
    ang = jnp.arange(n, dtype=jnp.float32)[:, None] * inv[None, :]
    return jnp.cos(ang), jnp.sin(ang)


def apply_rope(x, cos, sin):
    half = x.shape[-1] // 2
    x1, x2 = x[..., :half], x[..., half:]
    c = cos[None, :, None, :].astype(x.dtype)
    s = sin[None, :, None, :].astype(x.dtype)
    return jnp.concatenate([x1 * c - x2 * s, x1 * s + x2 * c], axis=-1)


def retention_decays(log_g, backward):
    i = jnp.arange(RET_CHUNK, dtype=jnp.float32)
    diff = i[:, None] - i[None, :]
    lg = log_g[:, None, None]
    pw = jnp.exp(lg * jnp.abs(diff)[None])
    if backward:
        intra = jnp.where(diff[None] < 0, pw, 0.0)
        q_dec = jnp.exp(log_g[:, None] * (RET_CHUNK - i)[None])
        k_dec = jnp.exp(log_g[:, None] * i[None])
    else:
        intra = jnp.where(diff[None] >= 0, pw, 0.0)
        q_dec = jnp.exp(log_g[:, None] * (i + 1.0)[None])
        k_dec = jnp.exp(log_g[:, None] * (RET_CHUNK - 1.0 - i)[None])
    c_dec = jnp.exp(log_g * RET_CHUNK)
    return intra, q_dec, k_dec, c_dec


def retention(h, cos, sin, wq, wk, wv, wg, wo):
    B, L, _ = h.shape
    q = apply_rope((h @ wq).reshape(B, L, RET_HEADS, RET_DK), cos, sin)
    k = apply_rope((h @ wk).reshape(B, L, RET_HEADS, RET_DK), cos, sin) * (RET_DK ** -0.5)
    v = (h @ wv).reshape(B, L, RET_HEADS, RET_DV)
    pad = RET_CHUNK - N_META

    def to_chunks(t):
        t = jnp.pad(t, ((0, 0), (pad, 0), (0, 0), (0, 0)))
        n_c = t.shape[1] // RET_CHUNK
        t = t.reshape(B, n_c, RET_CHUNK, RET_HEADS, t.shape[-1])
        return jnp.transpose(t, (1, 0, 3, 2, 4))

    qc, kc, vc = to_chunks(q), to_chunks(k), to_chunks(v)
    hh = jnp.arange(RET_HEADS, dtype=jnp.float32)
    log_g_fwd = jnp.log(1.0 - 2.0 ** (-5.0 - hh))
    log_g_bwd = jnp.log(1.0 - 2.0 ** (-5.5 - hh))
    D_f, qd_f, kd_f, cd_f = retention_decays(log_g_fwd, backward=False)
    D_b, qd_b, kd_b, cd_b = retention_decays(log_g_bwd, backward=True)

    def make_step(D, qd, kd, cd):
        def step(R, xs):
            q_, k_, v_ = xs
            s = jnp.einsum('bhid,bhjd->bhij', q_, k_) * D[None]
            o = (jnp.einsum('bhij,bhjv->bhiv', s, v_)
                 + jnp.einsum('bhid,bhdv->bhiv', q_ * qd[None, :, :, None], R))
            R = cd[None, :, None, None] * R + jnp.einsum('bhjd,bhjv->bhdv', k_ * kd[None, :, :, None], v_)
            return R, o
        return step

    R0 = jnp.zeros((B, RET_HEADS, RET_DK, RET_DV), jnp.float32)
    _, o_f = lax.scan(make_step(D_f, qd_f, kd_f, cd_f), R0, (qc, kc, vc))
    _, o_b = lax.scan(make_step(D_b, qd_b, kd_b, cd_b), R0, (qc, kc, vc), reverse=True)
    o = jnp.transpose(o_f + o_b, (1, 0, 3, 2, 4))
    o = o.reshape(B, -1, RET_HEADS, RET_DV)[:, pad:].astype(jnp.float32)
    mu = jnp.mean(o, axis=-1, keepdims=True)
    var = jnp.mean(jnp.square(o - mu), axis=-1, keepdims=True)
    o = ((o - mu) * lax.rsqrt(var + NORM_EPS)).astype(h.dtype).reshape(B, L, RET_HEADS * RET_DV)
    return (jax.nn.silu(h @ wg) * o) @ wo


def mla(h, cos, sin, wq_a, q_norm, wq_b, wkv_a, kv_norm, wkv_b, wo):
    B, L, _ = h.shape
    cq = rms_norm(h @ wq_a, q_norm)
    q = (cq @ wq_b).reshape(B, L, MLA_HEADS, MLA_NOPE + MLA_ROPE)
    q_nope = q[..., :MLA_NOPE]
    q_rope = apply_rope(q[..., MLA_NOPE:], cos, sin)
    kv_a = h @ wkv_a
    ckv = rms_norm(kv_a[..., :MLA_KV_LORA], kv_norm)
    k_rope = apply_rope(kv_a[..., None, MLA_KV_LORA:], cos, sin)[:, :, 0]
    kv = (ckv @ wkv_b).reshape(B, L, MLA_HEADS, MLA_NOPE + MLA_V)
    k_nope, v = kv[..., :MLA_NOPE], kv[..., MLA_NOPE:]
    scale = (MLA_NOPE + MLA_ROPE) ** -0.5

    def attend(qn, qr):
        s = (jnp.einsum('bqhd,bkhd->bhqk', qn, k_nope)
             + jnp.einsum('bqhd,bkd->bhqk', qr, k_rope))
        p = jax.nn.softmax(s.astype(jnp.float32) * scale, axis=-1)
        return jnp.einsum('bhqk,bkhd->bqhd', p.astype(v.dtype), v)

    o_meta = attend(q_nope[:, :N_META], q_rope[:, :N_META])
    n_real = L - N_META
    n_blk = n_real // MLA_QBLOCK

    def blocks(t):
        return jnp.moveaxis(t[:, N_META:].reshape(B, n_blk, MLA_QBLOCK, MLA_HEADS, t.shape[-1]), 1, 0)

    o_real = lax.map(lambda a: attend(a[0], a[1]), (blocks(q_nope), blocks(q_rope)))
    o_real = jnp.moveaxis(o_real, 0, 1).reshape(B, n_real, MLA_HEADS, MLA_V)
    o = jnp.concatenate([o_meta, o_real], axis=1).reshape(B, L, MLA_HEADS * MLA_V)
    return o @ wo


def sq_relu_mlp(h, w1, w2):
    return jnp.square(jax.nn.relu(h @ w1)) @ w2


def trunk(x, meta_tokens, norm1_g, norm2_g, mlp_w1, mlp_w2,
          ret_wq, ret_wk, ret_wv, ret_wg, ret_wo,
          mla_wq_a, mla_q_norm, mla_wq_b, mla_wkv_a, mla_kv_norm, mla_wkv_b, mla_wo,
          final_norm):
    B, S, D = x.shape
    meta = jnp.broadcast_to(meta_tokens.astype(x.dtype)[None], (B, N_META, D))
    h = jnp.concatenate([meta, x], axis=1)
    L = S + N_META
    cos_r, sin_r = rope_tables(L, RET_DK)
    cos_m, sin_m = rope_tables(L, MLA_ROPE)
    for i in range(DEPTH):
        a = rms_norm(h, norm1_g[i])
        j = i // 2
        if i % 2 == 0:
            h = h + retention(a, cos_r, sin_r, ret_wq[j], ret_wk[j], ret_wv[j], ret_wg[j], ret_wo[j])
        else:
            h = h + mla(a, cos_m, sin_m, mla_wq_a[j], mla_q_norm[j], mla_wq_b[j],
                        mla_wkv_a[j], mla_kv_norm[j], mla_wkv_b[j], mla_wo[j])
        a = rms_norm(h, norm2_g[i])
        h = h + sq_relu_mlp(a, mlp_w1[i], mlp_w2[i])
    return rms_norm(h, final_norm)[:, N_META:]


def setup_inputs(seed: int = 0) -> dict:
    key = jax.random.key(seed)
    ks = jax.random.split(key, 24)
    f32 = jnp.float32

    def w(k, shape, fan_in):
        return jax.random.normal(k, shape, f32) * (fan_in ** -0.5)

    def gain(k, shape):
        return 1.0 + 0.02 * jax.random.normal(k, shape, f32)

    D = D_MODEL
    NR, NM = N_RET_LAYERS, N_MLA_LAYERS
    return {
        "x_prompt": jax.random.normal(ks[0], (BATCH, SEQ, D), f32),
        "x_sample": jax.random.normal(ks[1], (DEC_BATCH, DEC_SEQ, D), f32),
        "meta_tokens": jax.random.normal(ks[2], (N_META, D), f32),
        "norm1_g": gain(ks[3], (DEPTH, D)),
        "norm2_g": gain(ks[4], (DEPTH, D)),
        "mlp_w1": w(ks[5], (DEPTH, D, D_FF), D),
        "mlp_w2": w(ks[6], (DEPTH, D_FF, D), D_FF),
        "ret_wq": w(ks[7], (NR, D, RET_HEADS * RET_DK), D),
        "ret_wk": w(ks[8], (NR, D, RET_HEADS * RET_DK), D),
        "ret_wv": w(ks[9], (NR, D, RET_HEADS * RET_DV), D),
        "ret_wg": w(ks[10], (NR, D, RET_HEADS * RET_DV), D),
        "ret_wo": w(ks[11], (NR, RET_HEADS * RET_DV, D), RET_HEADS * RET_DV),
        "mla_wq_a": w(ks[12], (NM, D, MLA_Q_LORA), D),
        "mla_q_norm": gain(ks[13], (NM, MLA_Q_LORA)),
        "mla_wq_b": w(ks[14], (NM, MLA_Q_LORA, MLA_HEADS * (MLA_NOPE + MLA_ROPE)), MLA_Q_LORA),
        "mla_wkv_a": w(ks[15], (NM, D, MLA_KV_LORA + MLA_ROPE), D),
        "mla_kv_norm": gain(ks[16], (NM, MLA_KV_LORA)),
        "mla_wkv_b": w(ks[17], (NM, MLA_KV_LORA, MLA_HEADS * (MLA_NOPE + MLA_V)), MLA_KV_LORA),
        "mla_wo": w(ks[18], (NM, MLA_HEADS * MLA_V, D), MLA_HEADS * MLA_V),
        "final_norm": gain(ks[19], (D,)),
    }


def reference(x_prompt, x_sample, meta_tokens, norm1_g, norm2_g, mlp_w1, mlp_w2,
              ret_wq, ret_wk, ret_wv, ret_wg, ret_wo,
              mla_wq_a, mla_q_norm, mla_wq_b, mla_wkv_a, mla_kv_norm, mla_wkv_b, mla_wo,
              final_norm):
    y_prompt = trunk(x_prompt, meta_tokens, norm1_g, norm2_g, mlp_w1, mlp_w2,
                     ret_wq, ret_wk, ret_wv, ret_wg, ret_wo,
                     mla_wq_a, mla_q_norm, mla_wq_b, mla_wkv_a, mla_kv_norm, mla_wkv_b, mla_wo,
                     final_norm)
    y_sample = trunk(x_sample, meta_tokens, norm1_g, norm2_g, mlp_w1, mlp_w2,
                     ret_wq, ret_wk, ret_wv, ret_wg, ret_wo,
                     mla_wq_a, mla_q_norm, mla_wq_b, mla_wkv_a, mla_kv_norm, mla_wkv_b, mla_wo,
                     final_norm)
    return (y_prompt, y_sample)
```

```python
import functools

import jax
import jax.numpy as jnp
from jax import lax
from jax.experimental import pallas as pl
from jax.experimental.pallas import tpu as pltpu

F32 = jnp.float32
BF16 = jnp.bfloat16

D_MODEL = 2048
N_META = 16
D_FF = 4 * D_MODEL
NORM_EPS = 1e-6
ROPE_BASE = 10000.0
RET_HEADS = 8
RET_DK = 256
RET_DV = 512
RET_CHUNK = 128
MLA_HEADS = 16
MLA_Q_LORA = 512
MLA_KV_LORA = 512
MLA_NOPE = 128
MLA_ROPE = 64
MLA_V = 128

LANES = 128
MXU_DIM = 256
VMEM_LIMIT_BYTES = 56 * 1024 * 1024

ROW_TILE = 1024
COL_TILE = 1024
K_TILE = 2048
NORM_ROWS = 64
RET_GROUP = 8
ATT_TQ = 1024
ATT_TK = 1024


def _tile(n, pref):
    t = min(n, pref)
    assert n % t == 0, (n, pref)
    return t


def _compiler_params(semantics):
    return pltpu.CompilerParams(dimension_semantics=semantics,
                                vmem_limit_bytes=VMEM_LIMIT_BYTES)


def _rms_rows(x, g):
    ms = jnp.mean(x * x, axis=-1, keepdims=True)
    return x * lax.rsqrt(ms + NORM_EPS) * g


def _norm_prologue(x_ref, g_ref, a_ref):
    rows = x_ref.shape[0]
    step = min(rows, NORM_ROWS)

    def body(r, carry):
        sl = pl.ds(pl.multiple_of(r * step, step), step)
        a_ref[sl, :] = _rms_rows(x_ref[sl, :], g_ref[...]).astype(BF16)
        return carry

    lax.fori_loop(0, rows // step, body, 0)


def _linear_kernel(*refs, nk, has_norm, n_extra, n_out, epilogue):
    x_ref = refs[0]
    pos = 1
    g_ref = None
    if has_norm:
        g_ref = refs[pos]
        pos += 1
    w_ref = refs[pos]
    pos += 1
    extra = refs[pos:pos + n_extra]
    pos += n_extra
    outs = refs[pos:pos + n_out]
    pos += n_out
    scratch = refs[pos:]
    j = pl.program_id(1)
    k = pl.program_id(2)

    if has_norm:
        a_ref = scratch[0]

        @pl.when(j == 0)
        def _():
            _norm_prologue(x_ref, g_ref, a_ref)

        lhs = a_ref[...]
    else:
        lhs = x_ref[...]
    part = jnp.dot(lhs, w_ref[...], preferred_element_type=F32)

    if nk == 1:
        epilogue(part, j, extra, outs)
    else:
        acc_ref = scratch[-1]

        @pl.when(k == 0)
        def _():
            acc_ref[...] = part

        @pl.when(jnp.logical_and(k > 0, k < nk - 1))
        def _():
            acc_ref[...] += part

        @pl.when(k == nk - 1)
        def _():
            epilogue(acc_ref[...] + part, j, extra, outs)


def _linear(x, w, *, name, tm, epilogue, out_widths, out_dtypes, norm_g=None, extras=(),
            extra_specs=(), tn=None, out_col_tiles=None):
    m, kdim = x.shape
    n = w.shape[1]
    assert m % tm == 0, (m, tm)
    tn = _tile(n, COL_TILE if tn is None else tn)
    tk = _tile(kdim, K_TILE)
    nk = kdim // tk
    has_norm = norm_g is not None
    assert not (has_norm and nk > 1)
    if out_col_tiles is None:
        out_col_tiles = [tn] * len(out_widths)

    in_specs = [pl.BlockSpec((tm, tk), lambda i, j, k: (i, k))]
    args = [x]
    if has_norm:
        in_specs.append(pl.BlockSpec((1, kdim), lambda i, j, k: (0, 0)))
        args.append(norm_g.reshape(1, kdim).astype(F32))
    in_specs.append(pl.BlockSpec((tk, tn), lambda i, j, k: (k, j)))
    args.append(w)
    in_specs.extend(extra_specs)
    args.extend(extras)

    out_specs = [pl.BlockSpec((tm, ct), lambda i, j, k: (i, j)) for ct in out_col_tiles]
    out_shape = [jax.ShapeDtypeStruct((m, wd), dt) for wd, dt in zip(out_widths, out_dtypes)]
    scratch = []
    if has_norm:
        scratch.append(pltpu.VMEM((tm, kdim), BF16))
    if nk > 1:
        scratch.append(pltpu.VMEM((tm, tn), F32))

    kernel = functools.partial(_linear_kernel, nk=nk, has_norm=has_norm,
                               n_extra=len(extras), n_out=len(out_widths),
                               epilogue=epilogue)
    res = pl.pallas_call(
        kernel,
        grid=(m // tm, n // tn, nk),
        in_specs=in_specs,
        out_specs=out_specs,
        out_shape=out_shape,
        scratch_shapes=scratch,
        compiler_params=_compiler_params(("parallel", "arbitrary", "arbitrary")),
        name=name,
    )(*args)
    return res


def _ep_plain(acc, j, extra, outs):
    outs[0][...] = acc.astype(outs[0].dtype)


def _ep_relu2(acc, j, extra, outs):
    r = jnp.maximum(acc, 0.0)
    outs[0][...] = (r * r).astype(outs[0].dtype)


def _ep_residual(acc, j, extra, outs):
    outs[0][...] = extra[0][...] + acc


def _silu(x):
    return x * (1.0 / (1.0 + jnp.exp(-x)))


def _ep_ret_qkvg(acc, j, extra, outs, *, n_rope, n_plain):
    cos_ref, sin_ref = extra
    o_ref = outs[0]
    tn = acc.shape[1]

    @pl.when(j < n_rope)
    def _():
        c = cos_ref[...]
        s = sin_ref[...]
        for hh in range(tn // RET_DK):
            lo = hh * RET_DK
            mid = lo + RET_DK // 2
            hi = lo + RET_DK
            x1 = acc[:, lo:mid]
            x2 = acc[:, mid:hi]
            o_ref[:, lo:mid] = (x1 * c - x2 * s).astype(o_ref.dtype)
            o_ref[:, mid:hi] = (x1 * s + x2 * c).astype(o_ref.dtype)

    @pl.when(jnp.logical_and(j >= n_rope, j < n_rope + n_plain))
    def _():
        o_ref[...] = acc.astype(o_ref.dtype)

    @pl.when(j >= n_rope + n_plain)
    def _():
        o_ref[...] = _silu(acc).astype(o_ref.dtype)


def _ep_mla_a(acc, j, extra, outs):
    qn_ref, kvn_ref, cc_ref, ss_ref = extra
    cq_ref, ckv_ref, kr_ref = outs
    a0 = MLA_Q_LORA
    a1 = a0 + MLA_KV_LORA
    a2 = a1 + LANES
    cq_ref[...] = _rms_rows(acc[:, :a0], qn_ref[...]).astype(cq_ref.dtype)
    ckv_ref[...] = _rms_rows(acc[:, a0:a1], kvn_ref[...]).astype(ckv_ref.dtype)
    kr = acc[:, a1:a2] * cc_ref[...] + acc[:, a2:] * ss_ref[...]
    kr_ref[...] = kr.astype(kr_ref.dtype)


def _ep_mla_q(acc, j, extra, outs, *, scale):
    cc_ref, ss_ref = extra
    o_ref = outs[0]
    tn = acc.shape[1]
    cc = cc_ref[...]
    ss = ss_ref[...]
    for hh in range(tn // MXU_DIM):
        lo = hh * MXU_DIM
        mid = lo + LANES
        hi = lo + MXU_DIM
        o_ref[:, lo:mid] = (acc[:, lo:mid] * scale).astype(o_ref.dtype)
        x = acc[:, mid:hi]
        xr = pltpu.roll(x, LANES // 2, 1)
        o_ref[:, mid:hi] = ((x * cc + xr * ss) * scale).astype(o_ref.dtype)


def _row_spec(tm, width, period):
    return pl.BlockSpec((tm, width), lambda i, j, k: (i % period, 0))


def _final_norm_kernel(x_ref, g_ref, o_ref):
    rows = x_ref.shape[0]
    step = min(rows, NORM_ROWS)

    def body(r, carry):
        sl = pl.ds(pl.multiple_of(r * step, step), step)
        o_ref[sl, :] = _rms_rows(x_ref[sl, :], g_ref[...])
        return carry

    lax.fori_loop(0, rows // step, body, 0)


def _final_norm(x, g):
    m, d = x.shape
    tm = _tile(m, ROW_TILE)
    return pl.pallas_call(
        _final_norm_kernel,
        grid=(m // tm,),
        in_specs=[pl.BlockSpec((tm, d), lambda i: (i, 0)),
                  pl.BlockSpec((1, d), lambda i: (0, 0))],
        out_specs=pl.BlockSpec((tm, d), lambda i: (i, 0)),
        out_shape=jax.ShapeDtypeStruct((m, d), F32),
        compiler_params=_compiler_params(("parallel",)),
        name="final_norm",
    )(x, g.reshape(1, d).astype(F32))


_NT = (((1,), (1,)), ((), ()))
_TN = (((0,), (0,)), ((), ()))


def _ret_kernel(cd_ref, q_ref, k_ref, v_ref, g_ref, qm_ref, km_ref, vm_ref, gm_ref,
                dmask_ref, qdf_ref, kdf_ref, qdb_ref, kdb_ref,
                o_ref, om_ref, rf_ref, rb_ref, opart_ref, *, n_groups, group):
    h = pl.program_id(1)
    t = pl.program_id(2)
    c = RET_CHUNK
    cd_f = cd_ref[h]
    cd_b = cd_ref[RET_HEADS + h]

    def decayed(x, dec_ref):
        return (x.astype(F32) * dec_ref[0]).astype(BF16)

    def pad_meta(ref):
        x = ref[...]
        return jnp.concatenate([jnp.zeros((c - N_META, x.shape[1]), x.dtype), x], axis=0)

    def chunk_rows(cidx):
        if isinstance(cidx, int):
            return slice(cidx * c, (cidx + 1) * c)
        return pl.ds(pl.multiple_of(cidx * c, c), c)

    def bwd_chunk(qc, kc, vc, cidx, update):
        ob = jnp.dot(decayed(qc, qdb_ref), rb_ref[...].astype(BF16), preferred_element_type=F32)
        opart_ref[chunk_rows(cidx), :] = ob
        if update:
            upd = lax.dot_general(decayed(kc, kdb_ref), vc, _TN, preferred_element_type=F32)
            rb_ref[...] = cd_b * rb_ref[...] + upd

    def fwd_chunk(qc, kc, vc, gc, cidx):
        s = lax.dot_general(qc, kc, _NT, preferred_element_type=F32) * dmask_ref[0]
        o = jnp.dot(s.astype(BF16), vc, preferred_element_type=F32)
        o = o + jnp.dot(decayed(qc, qdf_ref), rf_ref[...].astype(BF16), preferred_element_type=F32)
        o = o + opart_ref[chunk_rows(cidx), :]
        upd = lax.dot_general(decayed(kc, kdf_ref), vc, _TN, preferred_element_type=F32)
        rf_ref[...] = cd_f * rf_ref[...] + upd
        mu = jnp.mean(o, axis=-1, keepdims=True)
        xc = o - mu
        var = jnp.mean(xc * xc, axis=-1, keepdims=True)
        on = xc * lax.rsqrt(var + NORM_EPS)
        return (on * gc.astype(F32)).astype(BF16)

    @pl.when(t == 0)
    def _():
        rb_ref[...] = jnp.zeros_like(rb_ref)

    @pl.when(t < n_groups)
    def _():
        grp = n_groups - 1 - t
        for cc in range(group - 1, -1, -1):
            sl = slice(cc * c, (cc + 1) * c)
            bwd_chunk(q_ref[sl, :], k_ref[sl, :], v_ref[sl, :], 1 + grp * group + cc, True)

    @pl.when(t == n_groups - 1)
    def _():
        bwd_chunk(pad_meta(qm_ref), None, None, 0, False)

    @pl.when(t == n_groups)
    def _():
        rf_ref[...] = jnp.zeros_like(rf_ref)
        res = fwd_chunk(pad_meta(qm_ref), pad_meta(km_ref), pad_meta(vm_ref), pad_meta(gm_ref), 0)
        om_ref[...] = res[c - N_META:, :]

    @pl.when(t >= n_groups)
    def _():
        grp = t - n_groups
        for cc in range(group):
            sl = slice(cc * c, (cc + 1) * c)
            o_ref[sl, :] = fwd_chunk(q_ref[sl, :], k_ref[sl, :], v_ref[sl, :], g_ref[sl, :],
                                     1 + grp * group + cc)


def _retention_tables():
    c = RET_CHUNK
    hh = jnp.arange(RET_HEADS, dtype=F32)
    lg_f = jnp.log(1.0 - 2.0 ** (-5.0 - hh))
    lg_b = jnp.log(1.0 - 2.0 ** (-5.5 - hh))
    i = jnp.arange(c, dtype=F32)
    diff = i[:, None] - i[None, :]
    ad = jnp.abs(diff)[None]
    dmask = (jnp.where(diff[None] >= 0, jnp.exp(lg_f[:, None, None] * ad), 0.0)
             + jnp.where(diff[None] < 0, jnp.exp(lg_b[:, None, None] * ad), 0.0))

    def wide(v):
        return jnp.broadcast_to(v[:, :, None], (RET_HEADS, c, RET_DK))

    qd_f = wide(jnp.exp(lg_f[:, None] * (i + 1.0)[None]))
    kd_f = wide(jnp.exp(lg_f[:, None] * (c - 1.0 - i)[None]))
    qd_b = wide(jnp.exp(lg_b[:, None] * (c - i)[None]))
    kd_b = wide(jnp.exp(lg_b[:, None] * i[None]))
    cd = jnp.concatenate([jnp.exp(lg_f * c), jnp.exp(lg_b * c)])
    return cd, dmask, qd_f, kd_f, qd_b, kd_b


def _retention(qkvg_r, qkvg_m, batch, seq, tables):
    cd, dmask, qd_f, kd_f, qd_b, kd_b = tables
    c = RET_CHUNK
    n_real = seq // c
    group = _tile(n_real, RET_GROUP)
    n_groups = n_real // group
    rows = group * c
    k_off = RET_HEADS
    v_off = 2 * RET_HEADS * RET_DK // RET_DV
    g_off = v_off + RET_HEADS

    def sweep(t):
        return jnp.where(t < n_groups, n_groups - 1 - t, t - n_groups)

    def fwd_only(t):
        return jnp.maximum(t - n_groups, 0)

    in_specs = [
        pl.BlockSpec((rows, RET_DK), lambda b, h, t, cd: (b * n_groups + sweep(t), h)),
        pl.BlockSpec((rows, RET_DK), lambda b, h, t, cd: (b * n_groups + sweep(t), k_off + h)),
        pl.BlockSpec((rows, RET_DV), lambda b, h, t, cd: (b * n_groups + sweep(t), v_off + h)),
        pl.BlockSpec((rows, RET_DV), lambda b, h, t, cd: (b * n_groups + fwd_only(t), g_off + h)),
        pl.BlockSpec((N_META, RET_DK), lambda b, h, t, cd: (b, h)),
        pl.BlockSpec((N_META, RET_DK), lambda b, h, t, cd: (b, k_off + h)),
        pl.BlockSpec((N_META, RET_DV), lambda b, h, t, cd: (b, v_off + h)),
        pl.BlockSpec((N_META, RET_DV), lambda b, h, t, cd: (b, g_off + h)),
        pl.BlockSpec((1, c, c), lambda b, h, t, cd: (h, 0, 0)),
        pl.BlockSpec((1, c, RET_DK), lambda b, h, t, cd: (h, 0, 0)),
        pl.BlockSpec((1, c, RET_DK), lambda b, h, t, cd: (h, 0, 0)),
        pl.BlockSpec((1, c, RET_DK), lambda b, h, t, cd: (h, 0, 0)),
        pl.BlockSpec((1, c, RET_DK), lambda b, h, t, cd: (h, 0, 0)),
    ]
    out_specs = [
        pl.BlockSpec((rows, RET_DV), lambda b, h, t, cd: (b * n_groups + fwd_only(t), h)),
        pl.BlockSpec((N_META, RET_DV), lambda b, h, t, cd: (b, h)),
    ]
    out_shape = [
        jax.ShapeDtypeStruct((batch * seq, RET_HEADS * RET_DV), BF16),
        jax.ShapeDtypeStruct((batch * N_META, RET_HEADS * RET_DV), BF16),
    ]
    grid_spec = pltpu.PrefetchScalarGridSpec(
        num_scalar_prefetch=1,
        grid=(batch, RET_HEADS, 2 * n_groups),
        in_specs=in_specs,
        out_specs=out_specs,
        scratch_shapes=[
            pltpu.VMEM((RET_DK, RET_DV), F32),
            pltpu.VMEM((RET_DK, RET_DV), F32),
            pltpu.VMEM(((n_real + 1) * c, RET_DV), F32),
        ],
    )
    kernel = functools.partial(_ret_kernel, n_groups=n_groups, group=group)
    return pl.pallas_call(
        kernel,
        grid_spec=grid_spec,
        out_shape=out_shape,
        compiler_params=_compiler_params(("parallel", "parallel", "arbitrary")),
        name="retention",
    )(cd, qkvg_r, qkvg_r, qkvg_r, qkvg_r, qkvg_m, qkvg_m, qkvg_m, qkvg_m,
      dmask, qd_f, kd_f, qd_b, kd_b)


def _attn_kernel(q_ref, kn_ref, v_ref, kr_ref, knm_ref, vm_ref, krm_ref, o_ref,
                 m_ref, l_ref, acc_ref, *, tk):
    q = q_ref[...]

    def scores(kn, kr):
        kcat = jnp.concatenate([kn, kr], axis=1)
        return lax.dot_general(q, kcat, _NT, preferred_element_type=F32)

    s = scores(knm_ref[...], krm_ref[...])
    m0 = jnp.max(s, axis=1, keepdims=True)
    p = jnp.exp(s - m0)
    m_ref[...] = m0
    l_ref[...] = jnp.sum(p, axis=1, keepdims=True)
    acc_ref[...] = jnp.dot(p.astype(BF16), vm_ref[...], preferred_element_type=F32)

    def body(c, carry):
        sl = pl.ds(pl.multiple_of(c * tk, tk), tk)
        s = scores(kn_ref[sl, :], kr_ref[sl, :])
        m_prev = m_ref[...]
        m_new = jnp.maximum(m_prev, jnp.max(s, axis=1, keepdims=True))
        alpha = jnp.exp(m_prev - m_new)
        p = jnp.exp(s - m_new)
        l_ref[...] = alpha * l_ref[...] + jnp.sum(p, axis=1, keepdims=True)
        acc_ref[...] = alpha * acc_ref[...] + jnp.dot(p.astype(BF16), v_ref[sl, :],
                                                      preferred_element_type=F32)
        m_ref[...] = m_new
        return carry

    lax.fori_loop(0, kn_ref.shape[0] // tk, body, 0)
    o_ref[...] = (acc_ref[...] / l_ref[...]).astype(o_ref.dtype)


def _attention(q, kv_r, kr_r, kv_m, kr_m, batch, seq, q_rows_per_batch):
    tq = _tile(q_rows_per_batch, ATT_TQ)
    nq = q_rows_per_batch // tq
    tk = _tile(seq, ATT_TK)
    kernel = functools.partial(_attn_kernel, tk=tk)
    in_specs = [
        pl.BlockSpec((tq, MXU_DIM), lambda b, h, i: (b * nq + i, h)),
        pl.BlockSpec((seq, MLA_NOPE), lambda b, h, i: (b, h)),
        pl.BlockSpec((seq, MLA_V), lambda b, h, i: (b, MLA_HEADS + h)),
        pl.BlockSpec((seq, LANES), lambda b, h, i: (b, 0)),
        pl.BlockSpec((N_META, MLA_NOPE), lambda b, h, i: (b, h)),
        pl.BlockSpec((N_META, MLA_V), lambda b, h, i: (b, MLA_HEADS + h)),
        pl.BlockSpec((N_META, LANES), lambda b, h, i: (b, 0)),
    ]
    return pl.pallas_call(
        kernel,
        grid=(batch, MLA_HEADS, nq),
        in_specs=in_specs,
        out_specs=pl.BlockSpec((tq, MLA_V), lambda b, h, i: (b * nq + i, h)),
        out_shape=jax.ShapeDtypeStruct((q.shape[0], MLA_HEADS * MLA_V), BF16),
        scratch_shapes=[pltpu.VMEM((tq, 1), F32), pltpu.VMEM((tq, 1), F32),
                        pltpu.VMEM((tq, MLA_V), F32)],
        compiler_params=_compiler_params(("parallel", "parallel", "arbitrary")),
        name="attention",
    )(q, kv_r, kv_r, kr_r, kv_m, kv_m, kr_m)


def _rope_tables(n, dim):
    inv = 1.0 / (ROPE_BASE ** (jnp.arange(0, dim, 2, dtype=F32) / dim))
    ang = jnp.arange(n, dtype=F32)[:, None] * inv[None, :]
    return jnp.cos(ang), jnp.sin(ang)


def _swap_halves(w):
    half = w.shape[-1] // 2
    return jnp.concatenate([w[..., half:], w[..., :half]], axis=-1)


def _prep_weights(p):
    d = D_MODEL
    out = {}
    out["ret_qkvg"] = [
        jnp.concatenate([p["ret_wq"][j], p["ret_wk"][j] * (RET_DK ** -0.5),
                         p["ret_wv"][j], p["ret_wg"][j]], axis=1).astype(BF16)
        for j in range(p["ret_wq"].shape[0])]
    out["ret_wo"] = [p["ret_wo"][j].astype(BF16) for j in range(p["ret_wo"].shape[0])]
    mla_a, mla_qb, mla_kvb, mla_wo = [], [], [], []
    zeros = jnp.zeros((d, LANES - MLA_ROPE), F32)
    for j in range(p["mla_wq_a"].shape[0]):
        wkv_a = p["mla_wkv_a"][j]
        wr = wkv_a[:, MLA_KV_LORA:]
        mla_a.append(jnp.concatenate(
            [p["mla_wq_a"][j], wkv_a[:, :MLA_KV_LORA], wr, zeros, _swap_halves(wr), zeros],
            axis=1).astype(BF16))
        wq_b = p["mla_wq_b"][j].reshape(MLA_Q_LORA, MLA_HEADS, MLA_NOPE + MLA_ROPE)
        rope = wq_b[..., MLA_NOPE:]
        mla_qb.append(jnp.concatenate([wq_b[..., :MLA_NOPE], rope, _swap_halves(rope)], axis=-1)
                      .reshape(MLA_Q_LORA, MLA_HEADS * MXU_DIM).astype(BF16))
        wkv_b = p["mla_wkv_b"][j].reshape(MLA_KV_LORA, MLA_HEADS, MLA_NOPE + MLA_V)
        mla_kvb.append(jnp.concatenate(
            [wkv_b[..., :MLA_NOPE].reshape(MLA_KV_LORA, MLA_HEADS * MLA_NOPE),
             wkv_b[..., MLA_NOPE:].reshape(MLA_KV_LORA, MLA_HEADS * MLA_V)], axis=1).astype(BF16))
        mla_wo.append(p["mla_wo"][j].astype(BF16))
    out["mla_a"], out["mla_qb"], out["mla_kvb"], out["mla_wo"] = mla_a, mla_qb, mla_kvb, mla_wo
    out["mlp_w1"] = [p["mlp_w1"][i].astype(BF16) for i in range(p["mlp_w1"].shape[0])]
    out["mlp_w2"] = [p["mlp_w2"][i].astype(BF16) for i in range(p["mlp_w2"].shape[0])]
    return out


def _position_tables(batch, seq):
    n = seq + N_META
    cos_r, sin_r = _rope_tables(n, RET_DK)
    cos_m, sin_m = _rope_tables(n, MLA_ROPE)
    pad = jnp.zeros((n, LANES - MLA_ROPE), F32)
    cc = jnp.concatenate([cos_m, cos_m, pad], axis=1)
    ss = jnp.concatenate([-sin_m, sin_m, pad], axis=1)

    def split(tbl):
        return tbl[N_META:], jnp.tile(tbl[:N_META], (batch, 1))

    return {"ret_cos": split(cos_r), "ret_sin": split(sin_r),
            "mla_cc": split(cc), "mla_ss": split(ss)}


def _trunk(x, p, w, ret_tables):
    batch, seq, d = x.shape
    pos = _position_tables(batch, seq)
    h_r = x.reshape(batch * seq, d)
    h_m = jnp.broadcast_to(p["meta_tokens"].astype(F32)[None], (batch, N_META, d)).reshape(batch * N_META, d)
    depth = p["norm1_g"].shape[0]
    mla_scale = (MLA_NOPE + MLA_ROPE) ** -0.5

    def both(fn, last=False):
        return fn(0), (None if last else fn(1))

    tms = (_tile(seq, ROW_TILE), batch * N_META)
    pers = (seq // tms[0], 1)

    for i in range(depth):
        last = i == depth - 1
        j = i // 2
        hs = (h_r, h_m)
        if i % 2 == 0:
            w_in = w["ret_qkvg"][j]
            tn = _tile(RET_HEADS * RET_DK, COL_TILE)
            n_rope = 2 * RET_HEADS * RET_DK // tn
            n_plain = RET_HEADS * RET_DV // tn
            ep = functools.partial(_ep_ret_qkvg, n_rope=n_rope, n_plain=n_plain)

            def qkvg(which):
                hh = hs[which]
                tm, per = tms[which], pers[which]
                return _linear(hh, w_in, name="ret_qkvg", tm=tm, epilogue=ep, out_widths=[w_in.shape[1]], out_dtypes=[BF16],
                               norm_g=p["norm1_g"][i], tn=tn,
                               extras=(pos["ret_cos"][which], pos["ret_sin"][which]),
                               extra_specs=(_row_spec(tm, LANES, per), _row_spec(tm, LANES, per)))[0]

            qkvg_r, qkvg_m = both(qkvg)
            mix_r, mix_m = _retention(qkvg_r, qkvg_m, batch, seq, ret_tables)
            w_out = w["ret_wo"][j]
        else:
            w_a = w["mla_a"][j]

            def stage_a(which):
                hh = hs[which]
                tm, per = tms[which], pers[which]
                one = lambda width: pl.BlockSpec((1, width), lambda i_, j_, k_: (0, 0))
                return _linear(hh, w_a, name="mla_a", tm=tm, epilogue=_ep_mla_a,
                               out_widths=[MLA_Q_LORA, MLA_KV_LORA, LANES], out_dtypes=[BF16] * 3,
                               out_col_tiles=[MLA_Q_LORA, MLA_KV_LORA, LANES],
                               norm_g=p["norm1_g"][i], tn=w_a.shape[1],
                               extras=(p["mla_q_norm"][j].reshape(1, -1), p["mla_kv_norm"][j].reshape(1, -1),
                                       pos["mla_cc"][which], pos["mla_ss"][which]),
                               extra_specs=(one(MLA_Q_LORA), one(MLA_KV_LORA),
                                            _row_spec(tm, LANES, per), _row_spec(tm, LANES, per)))

            (cq_r, ckv_r, kr_r), (cq_m, ckv_m, kr_m) = both(stage_a)
            cqs = (cq_r, cq_m)
            ckvs = (ckv_r, ckv_m)
            w_qb = w["mla_qb"][j]
            w_kvb = w["mla_kvb"][j]
            ep_q = functools.partial(_ep_mla_q, scale=mla_scale)

            def stage_q(which):
                cq = cqs[which]
                tm, per = tms[which], pers[which]
                return _linear(cq, w_qb, name="mla_qb", tm=tm, epilogue=ep_q, out_widths=[w_qb.shape[1]], out_dtypes=[BF16],
                               extras=(pos["mla_cc"][which], pos["mla_ss"][which]),
                               extra_specs=(_row_spec(tm, LANES, per), _row_spec(tm, LANES, per)))[0]

            def stage_kv(which):
                return _linear(ckvs[which], w_kvb, name="mla_kvb", tm=tms[which], epilogue=_ep_plain,
                               out_widths=[w_kvb.shape[1]], out_dtypes=[BF16])[0]

            q_r, q_m = both(stage_q, last)
            kv_r, kv_m = both(stage_kv)
            mix_r = _attention(q_r, kv_r, kr_r, kv_m, kr_m, batch, seq, seq)
            mix_m = None if last else _attention(q_m, kv_r, kr_r, kv_m, kr_m, batch, seq, N_META)
            w_out = w["mla_wo"][j]

        mixes = (mix_r, mix_m)

        def proj_out(which):
            hh = hs[which]
            tm = tms[which]
            tn = _tile(d, COL_TILE)
            return _linear(mixes[which], w_out, name="mix_out", tm=tm, epilogue=_ep_residual, out_widths=[d], out_dtypes=[F32],
                           extras=(hh,), extra_specs=(pl.BlockSpec((tm, tn), lambda i_, j_, k_: (i_, j_)),))[0]

        h_r, h_m = both(proj_out, last)
        hs = (h_r, h_m)

        def mlp(which):
            hh = hs[which]
            tm = tms[which]
            tn = _tile(d, COL_TILE)
            hid = _linear(hh, w["mlp_w1"][i], name="mlp_up", tm=tm, epilogue=_ep_relu2, out_widths=[D_FF], out_dtypes=[BF16],
                          norm_g=p["norm2_g"][i])[0]
            return _linear(hid, w["mlp_w2"][i], name="mlp_down", tm=tm, epilogue=_ep_residual, out_widths=[d], out_dtypes=[F32],
                           extras=(hh,), extra_specs=(pl.BlockSpec((tm, tn), lambda i_, j_, k_: (i_, j_)),))[0]

        h_r, h_m = both(mlp, last)

    return _final_norm(h_r, p["final_norm"]).reshape(batch, seq, d)


def kernel(x_prompt, x_sample, meta_tokens, norm1_g, norm2_g, mlp_w1, mlp_w2, ret_wq, ret_wk, ret_wv, ret_wg, ret_wo, mla_wq_a, mla_q_norm, mla_wq_b, mla_wkv_a, mla_kv_norm, mla_wkv_b, mla_wo, final_norm):
    p = dict(meta_tokens=meta_tokens, norm1_g=norm1_g, norm2_g=norm2_g, mlp_w1=mlp_w1, mlp_w2=mlp_w2,
             ret_wq=ret_wq, ret_wk=ret_wk, ret_wv=ret_wv, ret_wg=ret_wg, ret_wo=ret_wo,
             mla_wq_a=mla_wq_a, mla_q_norm=mla_q_norm, mla_wq_b=mla_wq_b, mla_wkv_a=mla_wkv_a,
             mla_kv_norm=mla_kv_norm, mla_wkv_b=mla_wkv_b, mla_wo=mla_wo, final_norm=final_norm)
    w = _prep_weights(p)
    ret_tables = _retention_tables()
    return (_trunk(x_prompt, p, w, ret_tables), _trunk(x_sample, p, w, ret_tables))
```

```python
import functools
import math

import jax
import jax.numpy as jnp
from jax import lax
from jax.experimental import pallas as pl
from jax.experimental.pallas import tpu as pltpu

F32 = jnp.float32
BF16 = jnp.bfloat16

D_MODEL = 2048
N_META = 16
D_FF = 4 * D_MODEL
NORM_EPS = 1e-6
ROPE_BASE = 10000.0
RET_HEADS = 8
RET_DK = 256
RET_DV = 512
MLA_HEADS = 16
MLA_Q_LORA = 512
MLA_KV_LORA = 512
MLA_NOPE = 128
MLA_ROPE = 64
MLA_V = 128

LANES = 128
MXU_DIM = 256
VMEM_LIMIT_BYTES = 56 * 1024 * 1024

ROW_TILE = 1024
STEP_MACS = 2 ** 31
MAX_COL_TILE = 2048
NORM_ROWS = 64
RET_CHUNK = 256
RET_GROUP = 8
ATT_TQ = 1024
ATT_TK = 512
ATT_META_TK = 1024
ONES_ROWS = 16

_NT = (((1,), (1,)), ((), ()))
_TN = (((0,), (0,)), ((), ()))


def _tile(n, pref):
    t = min(n, pref)
    assert n % t == 0, (n, pref)
    return t


def _col_tile(tm, kdim, n):
    return _tile(n, max(MXU_DIM, min(MAX_COL_TILE, STEP_MACS // (tm * kdim))))


def _compiler_params(semantics):
    return pltpu.CompilerParams(dimension_semantics=semantics,
                                vmem_limit_bytes=VMEM_LIMIT_BYTES)


def _rms_rows(x, g):
    ms = jnp.mean(x * x, axis=-1, keepdims=True)
    return x * lax.rsqrt(ms + NORM_EPS) * g


def _norm_prologue(x_ref, g_ref, a_ref):
    rows = x_ref.shape[0]
    step = min(rows, NORM_ROWS)

    def body(r, carry):
        sl = pl.ds(pl.multiple_of(r * step, step), step)
        a_ref[sl, :] = _rms_rows(x_ref[sl, :], g_ref[...]).astype(BF16)
        return carry

    lax.fori_loop(0, rows // step, body, 0)


def _linear_kernel(*refs, has_norm, n_extra, n_out, epilogue):
    x_ref = refs[0]
    pos = 1
    g_ref = None
    if has_norm:
        g_ref = refs[pos]
        pos += 1
    w_ref = refs[pos]
    pos += 1
    extra = refs[pos:pos + n_extra]
    pos += n_extra
    outs = refs[pos:pos + n_out]
    pos += n_out
    j = pl.program_id(1)

    if has_norm:
        lhs_ref = refs[pos]

        @pl.when(j == 0)
        def _():
            _norm_prologue(x_ref, g_ref, lhs_ref)
    else:
        lhs_ref = x_ref

    def product():
        return jnp.dot(lhs_ref[...], w_ref[...], preferred_element_type=F32)

    epilogue(product, j, extra, outs)


def _linear(x, w, *, name, tm, epilogue, out_widths, out_dtypes, norm_g=None, extras=(),
            extra_specs=(), tn=None, out_col_tiles=None):
    m, kdim = x.shape
    n = w.shape[1]
    assert m % tm == 0, (m, tm)
    tn = _col_tile(tm, kdim, n) if tn is None else tn
    assert n % tn == 0, (n, tn)
    has_norm = norm_g is not None
    if out_col_tiles is None:
        out_col_tiles = [tn] * len(out_widths)

    in_specs = [pl.BlockSpec((tm, kdim), lambda i, j: (i, 0))]
    args = [x]
    if has_norm:
        in_specs.append(pl.BlockSpec((1, kdim), lambda i, j: (0, 0)))
        args.append(norm_g.reshape(1, kdim).astype(F32))
    in_specs.append(pl.BlockSpec((kdim, tn), lambda i, j: (0, j)))
    args.append(w)
    in_specs.extend(extra_specs)
    args.extend(extras)

    out_specs = [pl.BlockSpec((tm, ct), lambda i, j: (i, j)) for ct in out_col_tiles]
    out_shape = [jax.ShapeDtypeStruct((m, wd), dt) for wd, dt in zip(out_widths, out_dtypes)]
    scratch = [pltpu.VMEM((tm, kdim), BF16)] if has_norm else []

    kernel = functools.partial(_linear_kernel, has_norm=has_norm, n_extra=len(extras),
                               n_out=len(out_widths), epilogue=epilogue)
    return pl.pallas_call(
        kernel,
        grid=(m // tm, n // tn),
        in_specs=in_specs,
        out_specs=out_specs,
        out_shape=out_shape,
        scratch_shapes=scratch,
        compiler_params=_compiler_params(("parallel", "arbitrary")),
        name=name,
    )(*args)


def _ep_plain(product, j, extra, outs):
    outs[0][...] = product().astype(outs[0].dtype)


def _ep_relu2(product, j, extra, outs):
    r = jnp.maximum(product(), 0.0)
    outs[0][...] = (r * r).astype(outs[0].dtype)


def _ep_residual(product, j, extra, outs):
    outs[0][...] = extra[0][...] + product()


def _silu(x):
    return x * (1.0 / (1.0 + jnp.exp(-x)))


def _ep_ret_qkvg(product, j, extra, outs, *, n_rope, n_plain):
    cos_ref, sin_ref = extra
    o_ref = outs[0]
    tn = o_ref.shape[1]

    @pl.when(j < n_rope)
    def _():
        acc = product()
        c = cos_ref[...]
        s = sin_ref[...]
        for hh in range(tn // RET_DK):
            lo = hh * RET_DK
            mid = lo + RET_DK // 2
            hi = lo + RET_DK
            x1 = acc[:, lo:mid]
            x2 = acc[:, mid:hi]
            o_ref[:, lo:mid] = (x1 * c - x2 * s).astype(o_ref.dtype)
            o_ref[:, mid:hi] = (x1 * s + x2 * c).astype(o_ref.dtype)

    @pl.when(jnp.logical_and(j >= n_rope, j < n_rope + n_plain))
    def _():
        o_ref[...] = product().astype(o_ref.dtype)

    @pl.when(j >= n_rope + n_plain)
    def _():
        o_ref[...] = _silu(product()).astype(o_ref.dtype)


def _ep_mla_a(product, j, extra, outs):
    qn_ref, kvn_ref, cc_ref, ss_ref = extra
    cq_ref, ckv_ref, kr_ref = outs
    acc = product()
    a0 = MLA_Q_LORA
    a1 = a0 + MLA_KV_LORA
    a2 = a1 + LANES
    cq_ref[...] = _rms_rows(acc[:, :a0], qn_ref[...]).astype(cq_ref.dtype)
    ckv_ref[...] = _rms_rows(acc[:, a0:a1], kvn_ref[...]).astype(ckv_ref.dtype)
    kr = acc[:, a1:a2] * cc_ref[...] + acc[:, a2:] * ss_ref[...]
    kr_ref[...] = kr.astype(kr_ref.dtype)


def _ep_mla_q(product, j, extra, outs, *, scale):
    cc_ref, ss_ref = extra
    o_ref = outs[0]
    acc = product()
    cc = cc_ref[...]
    ss = ss_ref[...]
    for hh in range(o_ref.shape[1] // MXU_DIM):
        lo = hh * MXU_DIM
        mid = lo + LANES
        hi = lo + MXU_DIM
        o_ref[:, lo:mid] = (acc[:, lo:mid] * scale).astype(o_ref.dtype)
        x = acc[:, mid:hi]
        xr = pltpu.roll(x, LANES // 2, 1)
        o_ref[:, mid:hi] = ((x * cc + xr * ss) * scale).astype(o_ref.dtype)


def _row_spec(tm, width, period):
    return pl.BlockSpec((tm, width), lambda i, j: (i % period, 0))


def _linear_t_kernel(*refs, n_extra, epilogue):
    wt_ref, x_ref = refs[0], refs[1]
    extra = refs[2:2 + n_extra]
    o_ref = refs[2 + n_extra]

    def product():
        return lax.dot_general(wt_ref[...], x_ref[...], _NT, preferred_element_type=F32)

    epilogue(product, extra, o_ref)


def _linear_t(x, wt, *, name, tm, epilogue, extras=(), extra_specs=()):
    m, kdim = x.shape
    n = wt.shape[0]
    assert m % tm == 0, (m, tm)
    tn = _col_tile(tm, kdim, n)
    in_specs = [pl.BlockSpec((tn, kdim), lambda i, j: (j, 0)),
                pl.BlockSpec((tm, kdim), lambda i, j: (i, 0))]
    in_specs.extend(extra_specs)
    kernel = functools.partial(_linear_t_kernel, n_extra=len(extras), epilogue=epilogue)
    return pl.pallas_call(
        kernel,
        grid=(m // tm, n // tn),
        in_specs=in_specs,
        out_specs=pl.BlockSpec((tn, tm), lambda i, j: (j, i)),
        out_shape=jax.ShapeDtypeStruct((n, m), BF16),
        compiler_params=_compiler_params(("parallel", "arbitrary")),
        name=name,
    )(wt, x, *extras)


def _ep_t_plain(product, extra, o_ref):
    o_ref[...] = product().astype(o_ref.dtype)


def _ep_t_mla_q(product, extra, o_ref, *, scale):
    cc_ref, ss_ref = extra
    acc = product()
    cc = cc_ref[...]
    ss = ss_ref[...]
    half = LANES // 2
    for hh in range(o_ref.shape[0] // MXU_DIM):
        lo = hh * MXU_DIM
        mid = lo + LANES
        hi = lo + MXU_DIM
        o_ref[lo:mid, :] = (acc[lo:mid, :] * scale).astype(o_ref.dtype)
        x = acc[mid:hi, :]
        xr = jnp.concatenate([x[half:], x[:half]], axis=0)
        o_ref[mid:hi, :] = ((x * cc + xr * ss) * scale).astype(o_ref.dtype)


def _final_norm_kernel(x_ref, g_ref, o_ref):
    rows = x_ref.shape[0]
    step = min(rows, NORM_ROWS)

    def body(r, carry):
        sl = pl.ds(pl.multiple_of(r * step, step), step)
        o_ref[sl, :] = _rms_rows(x_ref[sl, :], g_ref[...])
        return carry

    lax.fori_loop(0, rows // step, body, 0)


def _final_norm(x, g):
    m, d = x.shape
    tm = _tile(m, ROW_TILE)
    return pl.pallas_call(
        _final_norm_kernel,
        grid=(m // tm,),
        in_specs=[pl.BlockSpec((tm, d), lambda i: (i, 0)),
                  pl.BlockSpec((1, d), lambda i: (0, 0))],
        out_specs=pl.BlockSpec((tm, d), lambda i: (i, 0)),
        out_shape=jax.ShapeDtypeStruct((m, d), F32),
        compiler_params=_compiler_params(("parallel",)),
        name="final_norm",
    )(x, g.reshape(1, d).astype(F32))


def _ret_kernel(cd_ref, q_ref, k_ref, v_ref, g_ref, qm_ref, km_ref, vm_ref, gm_ref,
                dmask_ref, qdf_ref, kdf_ref, qdb_ref, kdb_ref,
                o_ref, om_ref, rf_ref, rb_ref, opart_ref, *, n_groups, group):
    h = pl.program_id(1)
    t = pl.program_id(2)
    c = RET_CHUNK
    cd_f = cd_ref[h]
    cd_b = cd_ref[RET_HEADS + h]

    def decayed(x, dec_ref):
        return (x.astype(F32) * dec_ref[0]).astype(BF16)

    def pad_meta(ref):
        x = ref[...]
        return jnp.concatenate([jnp.zeros((c - N_META, x.shape[1]), x.dtype), x], axis=0)

    def chunk_rows(cidx):
        if isinstance(cidx, int):
            return slice(cidx * c, (cidx + 1) * c)
        return pl.ds(pl.multiple_of(cidx * c, c), c)

    def bwd_chunk(qc, kc, vc, cidx, update):
        ob = jnp.dot(decayed(qc, qdb_ref), rb_ref[...].astype(BF16), preferred_element_type=F32)
        opart_ref[chunk_rows(cidx), :] = ob
        if update:
            upd = lax.dot_general(decayed(kc, kdb_ref), vc, _TN, preferred_element_type=F32)
            rb_ref[...] = cd_b * rb_ref[...] + upd

    def fwd_chunk(qc, kc, vc, gc, cidx):
        s = lax.dot_general(qc, kc, _NT, preferred_element_type=F32) * dmask_ref[0]
        o = jnp.dot(s.astype(BF16), vc, preferred_element_type=F32)
        o = o + jnp.dot(decayed(qc, qdf_ref), rf_ref[...].astype(BF16), preferred_element_type=F32)
        o = o + opart_ref[chunk_rows(cidx), :]
        upd = lax.dot_general(decayed(kc, kdf_ref), vc, _TN, preferred_element_type=F32)
        rf_ref[...] = cd_f * rf_ref[...] + upd
        mu = jnp.mean(o, axis=-1, keepdims=True)
        xc = o - mu
        var = jnp.mean(xc * xc, axis=-1, keepdims=True)
        on = xc * lax.rsqrt(var + NORM_EPS)
        return (on * gc.astype(F32)).astype(BF16)

    @pl.when(t == 0)
    def _():
        rb_ref[...] = jnp.zeros_like(rb_ref)

    @pl.when(t < n_groups)
    def _():
        grp = n_groups - 1 - t
        for cc in range(group - 1, -1, -1):
            sl = slice(cc * c, (cc + 1) * c)
            bwd_chunk(q_ref[sl, :], k_ref[sl, :], v_ref[sl, :], 1 + grp * group + cc, True)

    @pl.when(t == n_groups - 1)
    def _():
        bwd_chunk(pad_meta(qm_ref), None, None, 0, False)

    @pl.when(t == n_groups)
    def _():
        rf_ref[...] = jnp.zeros_like(rf_ref)
        res = fwd_chunk(pad_meta(qm_ref), pad_meta(km_ref), pad_meta(vm_ref), pad_meta(gm_ref), 0)
        om_ref[...] = res[c - N_META:, :]

    @pl.when(t >= n_groups)
    def _():
        grp = t - n_groups
        for cc in range(group):
            sl = slice(cc * c, (cc + 1) * c)
            o_ref[sl, :] = fwd_chunk(q_ref[sl, :], k_ref[sl, :], v_ref[sl, :], g_ref[sl, :],
                                     1 + grp * group + cc)


def _retention_tables():
    c = RET_CHUNK
    hh = jnp.arange(RET_HEADS, dtype=F32)
    lg_f = jnp.log(1.0 - 2.0 ** (-5.0 - hh))
    lg_b = jnp.log(1.0 - 2.0 ** (-5.5 - hh))
    i = jnp.arange(c, dtype=F32)
    diff = i[:, None] - i[None, :]
    ad = jnp.abs(diff)[None]
    dmask = (jnp.where(diff[None] >= 0, jnp.exp(lg_f[:, None, None] * ad), 0.0)
             + jnp.where(diff[None] < 0, jnp.exp(lg_b[:, None, None] * ad), 0.0))

    def wide(v):
        return jnp.broadcast_to(v[:, :, None], (RET_HEADS, c, RET_DK))

    qd_f = wide(jnp.exp(lg_f[:, None] * (i + 1.0)[None]))
    kd_f = wide(jnp.exp(lg_f[:, None] * (c - 1.0 - i)[None]))
    qd_b = wide(jnp.exp(lg_b[:, None] * (c - i)[None]))
    kd_b = wide(jnp.exp(lg_b[:, None] * i[None]))
    cd = jnp.concatenate([jnp.exp(lg_f * c), jnp.exp(lg_b * c)])
    return cd, dmask, qd_f, kd_f, qd_b, kd_b


def _retention(qkvg_r, qkvg_m, batch, seq, tables):
    cd, dmask, qd_f, kd_f, qd_b, kd_b = tables
    c = RET_CHUNK
    n_real = seq // c
    group = _tile(n_real, RET_GROUP)
    n_groups = n_real // group
    rows = group * c
    k_off = RET_HEADS
    v_off = 2 * RET_HEADS * RET_DK // RET_DV
    g_off = v_off + RET_HEADS

    def sweep(t):
        return jnp.where(t < n_groups, n_groups - 1 - t, t - n_groups)

    def fwd_only(t):
        return jnp.maximum(t - n_groups, 0)

    in_specs = [
        pl.BlockSpec((rows, RET_DK), lambda b, h, t, cd: (b * n_groups + sweep(t), h)),
        pl.BlockSpec((rows, RET_DK), lambda b, h, t, cd: (b * n_groups + sweep(t), k_off + h)),
        pl.BlockSpec((rows, RET_DV), lambda b, h, t, cd: (b * n_groups + sweep(t), v_off + h)),
        pl.BlockSpec((rows, RET_DV), lambda b, h, t, cd: (b * n_groups + fwd_only(t), g_off + h)),
        pl.BlockSpec((N_META, RET_DK), lambda b, h, t, cd: (b, h)),
        pl.BlockSpec((N_META, RET_DK), lambda b, h, t, cd: (b, k_off + h)),
        pl.BlockSpec((N_META, RET_DV), lambda b, h, t, cd: (b, v_off + h)),
        pl.BlockSpec((N_META, RET_DV), lambda b, h, t, cd: (b, g_off + h)),
        pl.BlockSpec((1, c, c), lambda b, h, t, cd: (h, 0, 0)),
        pl.BlockSpec((1, c, RET_DK), lambda b, h, t, cd: (h, 0, 0)),
        pl.BlockSpec((1, c, RET_DK), lambda b, h, t, cd: (h, 0, 0)),
        pl.BlockSpec((1, c, RET_DK), lambda b, h, t, cd: (h, 0, 0)),
        pl.BlockSpec((1, c, RET_DK), lambda b, h, t, cd: (h, 0, 0)),
    ]
    out_specs = [
        pl.BlockSpec((rows, RET_DV), lambda b, h, t, cd: (b * n_groups + fwd_only(t), h)),
        pl.BlockSpec((N_META, RET_DV), lambda b, h, t, cd: (b, h)),
    ]
    out_shape = [
        jax.ShapeDtypeStruct((batch * seq, RET_HEADS * RET_DV), BF16),
        jax.ShapeDtypeStruct((batch * N_META, RET_HEADS * RET_DV), BF16),
    ]
    grid_spec = pltpu.PrefetchScalarGridSpec(
        num_scalar_prefetch=1,
        grid=(batch, RET_HEADS, 2 * n_groups),
        in_specs=in_specs,
        out_specs=out_specs,
        scratch_shapes=[
            pltpu.VMEM((RET_DK, RET_DV), F32),
            pltpu.VMEM((RET_DK, RET_DV), F32),
            pltpu.VMEM(((n_real + 1) * c, RET_DV), F32),
        ],
    )
    kernel = functools.partial(_ret_kernel, n_groups=n_groups, group=group)
    return pl.pallas_call(
        kernel,
        grid_spec=grid_spec,
        out_shape=out_shape,
        compiler_params=_compiler_params(("parallel", "parallel", "arbitrary")),
        name="retention",
    )(cd, qkvg_r, qkvg_r, qkvg_r, qkvg_r, qkvg_m, qkvg_m, qkvg_m, qkvg_m,
      dmask, qd_f, kd_f, qd_b, kd_b)


def _attn_kernel(qt_ref, kn_ref, kr_ref, vt_ref, knm_ref, krm_ref, vm_ref, o_ref, *, tk):
    qt = qt_ref[...]

    def scores(kn, kr):
        return jnp.dot(jnp.concatenate([kn, kr], axis=1), qt, preferred_element_type=F32)

    def block(c):
        sl = slice(c * tk, (c + 1) * tk)
        return scores(kn_ref[sl, :], kr_ref[sl, :])

    def values(vt, n):
        return jnp.concatenate([vt, jnp.ones((ONES_ROWS, n), BF16)], axis=0)

    nk = kn_ref.shape[0] // tk
    s = scores(knm_ref[...], krm_ref[...])
    s_next = block(0)
    m = jnp.max(s, axis=0, keepdims=True)
    p = jnp.exp2(s - m).astype(BF16)
    vm_t = vm_ref[...].astype(F32).T.astype(BF16)
    acc = jnp.dot(values(vm_t, N_META), p, preferred_element_type=F32)
    for c in range(nk):
        s = s_next
        if c + 1 < nk:
            s_next = block(c + 1)
        m_new = jnp.maximum(m, jnp.max(s, axis=0, keepdims=True))
        alpha = jnp.exp2(m - m_new)
        p = jnp.exp2(s - m_new).astype(BF16)
        sl = slice(c * tk, (c + 1) * tk)
        acc = alpha * acc + jnp.dot(values(vt_ref[:, sl], tk), p, preferred_element_type=F32)
        m = m_new
    o_ref[...] = (acc[:MLA_V] / acc[MLA_V:MLA_V + 1]).T.astype(o_ref.dtype)


def _attn_meta_kernel(q_ref, kn_ref, kr_ref, vt_ref, knm_ref, krm_ref, vm_ref, o_ref,
                      m_ref, l_ref, acc_ref, *, tk):
    q = q_ref[...]

    def scores(kn, kr):
        return lax.dot_general(q, jnp.concatenate([kn, kr], axis=1), _NT, preferred_element_type=F32)

    s = scores(knm_ref[...], krm_ref[...])
    m0 = jnp.max(s, axis=1, keepdims=True)
    p = jnp.exp2(s - m0)
    m_ref[...] = m0
    l_ref[...] = jnp.sum(p, axis=1, keepdims=True)
    acc_ref[...] = jnp.dot(p.astype(BF16), vm_ref[...], preferred_element_type=F32)

    def body(c, carry):
        sl = pl.ds(pl.multiple_of(c * tk, tk), tk)
        s = scores(kn_ref[sl, :], kr_ref[sl, :])
        m_prev = m_ref[...]
        m_new = jnp.maximum(m_prev, jnp.max(s, axis=1, keepdims=True))
        alpha = jnp.exp2(m_prev - m_new)
        p = jnp.exp2(s - m_new)
        l_ref[...] = alpha * l_ref[...] + jnp.sum(p, axis=1, keepdims=True)
        pv = lax.dot_general(p.astype(BF16), vt_ref[:, sl], _NT, preferred_element_type=F32)
        acc_ref[...] = alpha * acc_ref[...] + pv
        m_ref[...] = m_new
        return carry

    lax.fori_loop(0, kn_ref.shape[0] // tk, body, 0)
    o_ref[...] = (acc_ref[...] / l_ref[...]).astype(o_ref.dtype)


def _attention(qt, q_m, kn, kr, vt, kv_m, kr_m, batch, seq):
    tq = _tile(seq, ATT_TQ)
    nq = seq // tq
    kv_specs = [
        pl.BlockSpec((seq, MLA_NOPE), lambda b, h, i: (b, h)),
        pl.BlockSpec((seq, LANES), lambda b, h, i: (b, 0)),
        pl.BlockSpec((MLA_V, seq), lambda b, h, i: (h, b)),
        pl.BlockSpec((N_META, MLA_NOPE), lambda b, h, i: (b, h)),
        pl.BlockSpec((N_META, LANES), lambda b, h, i: (b, 0)),
        pl.BlockSpec((N_META, MLA_V), lambda b, h, i: (b, MLA_HEADS + h)),
    ]
    kv_args = (kn, kr, vt, kv_m, kr_m, kv_m)
    o_r = pl.pallas_call(
        functools.partial(_attn_kernel, tk=_tile(seq, ATT_TK)),
        grid=(batch, MLA_HEADS, nq),
        in_specs=[pl.BlockSpec((MXU_DIM, tq), lambda b, h, i: (h, b * nq + i))] + kv_specs,
        out_specs=pl.BlockSpec((tq, MLA_V), lambda b, h, i: (b * nq + i, h)),
        out_shape=jax.ShapeDtypeStruct((batch * seq, MLA_HEADS * MLA_V), BF16),
        compiler_params=_compiler_params(("parallel", "parallel", "arbitrary")),
        name="attention",
    )(qt, *kv_args)
    if q_m is None:
        return o_r, None
    o_m = pl.pallas_call(
        functools.partial(_attn_meta_kernel, tk=_tile(seq, ATT_META_TK)),
        grid=(batch, MLA_HEADS, 1),
        in_specs=[pl.BlockSpec((N_META, MXU_DIM), lambda b, h, i: (b, h))] + kv_specs,
        out_specs=pl.BlockSpec((N_META, MLA_V), lambda b, h, i: (b, h)),
        out_shape=jax.ShapeDtypeStruct((batch * N_META, MLA_HEADS * MLA_V), BF16),
        scratch_shapes=[pltpu.VMEM((N_META, 1), F32), pltpu.VMEM((N_META, 1), F32),
                        pltpu.VMEM((N_META, MLA_V), F32)],
        compiler_params=_compiler_params(("parallel", "parallel", "arbitrary")),
        name="attention_meta",
    )(q_m, *kv_args)
    return o_r, o_m


def _rope_tables(n, dim):
    inv = 1.0 / (ROPE_BASE ** (jnp.arange(0, dim, 2, dtype=F32) / dim))
    ang = jnp.arange(n, dtype=F32)[:, None] * inv[None, :]
    return jnp.cos(ang), jnp.sin(ang)


def _swap_halves(w):
    half = w.shape[-1] // 2
    return jnp.concatenate([w[..., half:], w[..., :half]], axis=-1)


def _prep_weights(p):
    d = D_MODEL
    out = {}
    out["ret_qkvg"] = [
        jnp.concatenate([p["ret_wq"][j], p["ret_wk"][j] * (RET_DK ** -0.5),
                         p["ret_wv"][j], p["ret_wg"][j]], axis=1).astype(BF16)
        for j in range(p["ret_wq"].shape[0])]
    out["ret_wo"] = [p["ret_wo"][j].astype(BF16) for j in range(p["ret_wo"].shape[0])]
    keys = ("mla_a", "mla_qb", "mla_qb_t", "mla_kvb", "mla_kn", "mla_v_t", "mla_wo")
    for key in keys:
        out[key] = []
    zeros = jnp.zeros((d, LANES - MLA_ROPE), F32)
    for j in range(p["mla_wq_a"].shape[0]):
        wkv_a = p["mla_wkv_a"][j]
        wr = wkv_a[:, MLA_KV_LORA:]
        out["mla_a"].append(jnp.concatenate(
            [p["mla_wq_a"][j], wkv_a[:, :MLA_KV_LORA], wr, zeros, _swap_halves(wr), zeros],
            axis=1).astype(BF16))
        wq_b = p["mla_wq_b"][j].reshape(MLA_Q_LORA, MLA_HEADS, MLA_NOPE + MLA_ROPE)
        rope = wq_b[..., MLA_NOPE:]
        qb = jnp.concatenate([wq_b[..., :MLA_NOPE], rope, _swap_halves(rope)], axis=-1)
        qb = qb.reshape(MLA_Q_LORA, MLA_HEADS * MXU_DIM).astype(BF16)
        out["mla_qb"].append(qb)
        out["mla_qb_t"].append(qb.T)
        wkv_b = p["mla_wkv_b"][j].reshape(MLA_KV_LORA, MLA_HEADS, MLA_NOPE + MLA_V)
        kn = wkv_b[..., :MLA_NOPE].reshape(MLA_KV_LORA, MLA_HEADS * MLA_NOPE).astype(BF16)
        vv = wkv_b[..., MLA_NOPE:].reshape(MLA_KV_LORA, MLA_HEADS * MLA_V).astype(BF16)
        out["mla_kvb"].append(jnp.concatenate([kn, vv], axis=1))
        out["mla_kn"].append(kn)
        out["mla_v_t"].append(vv.T)
        out["mla_wo"].append(p["mla_wo"][j].astype(BF16))
    out["mlp_w1"] = [p["mlp_w1"][i].astype(BF16) for i in range(p["mlp_w1"].shape[0])]
    out["mlp_w2"] = [p["mlp_w2"][i].astype(BF16) for i in range(p["mlp_w2"].shape[0])]
    return out


def _position_tables(batch, seq):
    n = seq + N_META
    cos_r, sin_r = _rope_tables(n, RET_DK)
    cos_m, sin_m = _rope_tables(n, MLA_ROPE)
    pad = jnp.zeros((n, LANES - MLA_ROPE), F32)
    cc = jnp.concatenate([cos_m, cos_m, pad], axis=1)
    ss = jnp.concatenate([-sin_m, sin_m, pad], axis=1)

    def split(tbl):
        return tbl[N_META:], jnp.tile(tbl[:N_META], (batch, 1))

    return {"ret_cos": split(cos_r), "ret_sin": split(sin_r),
            "mla_cc": split(cc), "mla_ss": split(ss),
            "mla_cc_t": cc[N_META:].T, "mla_ss_t": ss[N_META:].T}


def _trunk(x, p, w, ret_tables):
    batch, seq, d = x.shape
    pos = _position_tables(batch, seq)
    h_r = x.reshape(batch * seq, d)
    h_m = jnp.broadcast_to(p["meta_tokens"].astype(F32)[None], (batch, N_META, d)).reshape(batch * N_META, d)
    depth = p["norm1_g"].shape[0]
    q_scale = math.log2(math.e) * (MLA_NOPE + MLA_ROPE) ** -0.5

    def both(fn, last=False):
        return fn(0), (None if last else fn(1))

    tms = (_tile(seq, ROW_TILE), batch * N_META)
    pers = (seq // tms[0], 1)

    for i in range(depth):
        last = i == depth - 1
        j = i // 2
        hs = (h_r, h_m)
        if i % 2 == 0:
            w_in = w["ret_qkvg"][j]

            def qkvg(which):
                hh = hs[which]
                tm, per = tms[which], pers[which]
                tn = _col_tile(tm, d, RET_HEADS * RET_DK)
                ep = functools.partial(_ep_ret_qkvg, n_rope=2 * RET_HEADS * RET_DK // tn,
                                       n_plain=RET_HEADS * RET_DV // tn)
                return _linear(hh, w_in, name="ret_qkvg", tm=tm, epilogue=ep, out_widths=[w_in.shape[1]],
                               out_dtypes=[BF16], norm_g=p["norm1_g"][i], tn=tn,
                               extras=(pos["ret_cos"][which], pos["ret_sin"][which]),
                               extra_specs=(_row_spec(tm, LANES, per), _row_spec(tm, LANES, per)))[0]

            qkvg_r, qkvg_m = both(qkvg)
            mix_r, mix_m = _retention(qkvg_r, qkvg_m, batch, seq, ret_tables)
            w_out = w["ret_wo"][j]
        else:
            w_a = w["mla_a"][j]

            def stage_a(which):
                hh = hs[which]
                tm, per = tms[which], pers[which]
                one = lambda width: pl.BlockSpec((1, width), lambda i_, j_: (0, 0))
                return _linear(hh, w_a, name="mla_a", tm=tm, epilogue=_ep_mla_a,
                               out_widths=[MLA_Q_LORA, MLA_KV_LORA, LANES], out_dtypes=[BF16] * 3,
                               out_col_tiles=[MLA_Q_LORA, MLA_KV_LORA, LANES],
                               norm_g=p["norm1_g"][i], tn=w_a.shape[1],
                               extras=(p["mla_q_norm"][j].reshape(1, -1), p["mla_kv_norm"][j].reshape(1, -1),
                                       pos["mla_cc"][which], pos["mla_ss"][which]),
                               extra_specs=(one(MLA_Q_LORA), one(MLA_KV_LORA),
                                            _row_spec(tm, LANES, per), _row_spec(tm, LANES, per)))

            (cq_r, ckv_r, kr_r), (cq_m, ckv_m, kr_m) = both(stage_a)
            tm_r, per_r = tms[0], pers[0]
            t_spec = pl.BlockSpec((LANES, tm_r), lambda i_, j_: (0, i_ % per_r))
            qt_r = _linear_t(cq_r, w["mla_qb_t"][j], name="mla_qb_t", tm=tm_r,
                             epilogue=functools.partial(_ep_t_mla_q, scale=q_scale),
                             extras=(pos["mla_cc_t"], pos["mla_ss_t"]), extra_specs=(t_spec, t_spec))
            kn_r = _linear(ckv_r, w["mla_kn"][j], name="mla_kn", tm=tm_r, epilogue=_ep_plain,
                           out_widths=[MLA_HEADS * MLA_NOPE], out_dtypes=[BF16])[0]
            vt_r = _linear_t(ckv_r, w["mla_v_t"][j], name="mla_v_t", tm=tm_r, epilogue=_ep_t_plain)
            tm_m = tms[1]
            kv_m = _linear(ckv_m, w["mla_kvb"][j], name="mla_kvb", tm=tm_m, epilogue=_ep_plain,
                           out_widths=[w["mla_kvb"][j].shape[1]], out_dtypes=[BF16])[0]
            q_m = None
            if not last:
                m_spec = _row_spec(tm_m, LANES, 1)
                q_m = _linear(cq_m, w["mla_qb"][j], name="mla_qb", tm=tm_m,
                              epilogue=functools.partial(_ep_mla_q, scale=q_scale),
                              out_widths=[w["mla_qb"][j].shape[1]], out_dtypes=[BF16],
                              extras=(pos["mla_cc"][1], pos["mla_ss"][1]), extra_specs=(m_spec, m_spec))[0]
            mix_r, mix_m = _attention(qt_r, q_m, kn_r, kr_r, vt_r, kv_m, kr_m, batch, seq)
            w_out = w["mla_wo"][j]

        mixes = (mix_r, mix_m)

        def proj_out(which):
            hh = hs[which]
            tm = tms[which]
            tn = _col_tile(tm, w_out.shape[0], d)
            return _linear(mixes[which], w_out, name="mix_out", tm=tm, epilogue=_ep_residual,
                           out_widths=[d], out_dtypes=[F32], tn=tn,
                           extras=(hh,), extra_specs=(pl.BlockSpec((tm, tn), lambda i_, j_: (i_, j_)),))[0]

        h_r, h_m = both(proj_out, last)
        hs = (h_r, h_m)

        def mlp(which):
            hh = hs[which]
            tm = tms[which]
            tn = _col_tile(tm, D_FF, d)
            hid = _linear(hh, w["mlp_w1"][i], name="mlp_up", tm=tm, epilogue=_ep_relu2,
                          out_widths=[D_FF], out_dtypes=[BF16], norm_g=p["norm2_g"][i])[0]
            return _linear(hid, w["mlp_w2"][i], name="mlp_down", tm=tm, epilogue=_ep_residual,
                           out_widths=[d], out_dtypes=[F32], tn=tn,
                           extras=(hh,), extra_specs=(pl.BlockSpec((tm, tn), lambda i_, j_: (i_, j_)),))[0]

        h_r, h_m = both(mlp, last)

    return _final_norm(h_r, p["final_norm"]).reshape(batch, seq, d)


def kernel(x_prompt, x_sample, meta_tokens, norm1_g, norm2_g, mlp_w1, mlp_w2, ret_wq, ret_wk, ret_wv, ret_wg, ret_wo, mla_wq_a, mla_q_norm, mla_wq_b, mla_wkv_a, mla_kv_norm, mla_wkv_b, mla_wo, final_norm):
    p = dict(meta_tokens=meta_tokens, norm1_g=norm1_g, norm2_g=norm2_g, mlp_w1=mlp_w1, mlp_w2=mlp_w2,
             ret_wq=ret_wq, ret_wk=ret_wk, ret_wv=ret_wv, ret_wg=ret_wg, ret_wo=ret_wo,
             mla_wq_a=mla_wq_a, mla_q_norm=mla_q_norm, mla_wq_b=mla_wq_b, mla_wkv_a=mla_wkv_a,
             mla_kv_norm=mla_kv_norm, mla_wkv_b=mla_wkv_b, mla_wo=mla_wo, final_norm=final_norm)
    w = _prep_weights(p)
    ret_tables = _retention_tables()
    return (_trunk(x_prompt, p, w, ret_tables), _trunk(x_sample, p, w, ret_tables))
```

```python
import functools
import math

import jax
import jax.numpy as jnp
from jax import lax
from jax.experimental import pallas as pl
from jax.experimental.pallas import tpu as pltpu

F32 = jnp.float32
BF16 = jnp.bfloat16

D_MODEL = 2048
N_META = 16
D_FF = 4 * D_MODEL
NORM_EPS = 1e-6
ROPE_BASE = 10000.0
RET_HEADS = 8
RET_DK = 256
RET_DV = 512
MLA_HEADS = 16
MLA_Q_LORA = 512
MLA_KV_LORA = 512
MLA_NOPE = 128
MLA_ROPE = 64
MLA_V = 128

LANES = 128
MXU_DIM = 256
VMEM_LIMIT_BYTES = 56 * 1024 * 1024

ROW_TILE = 1024
STEP_MACS = 2 ** 32
MAX_COL_TILE = 2048
VMEM_TILE_BUDGET = 48 * 1024 * 1024
NORM_ROWS = 64
NORM_UNROLL = 4
RET_CHUNK = 256
RET_GROUP = 8
ATT_TQ = 1024
ATT_TK = 512
ATT_META_TK = 1024
ATT_META_HEADS = 4
ONES_ROWS = 16

_NT = (((1,), (1,)), ((), ()))
_TN = (((0,), (0,)), ((), ()))


def _tile(n, pref):
    t = min(n, pref)
    assert n % t == 0, (n, pref)
    return t


def _col_tile(kdim, n, *, x_bytes, has_norm, out_bytes, tm=ROW_TILE):
    def footprint(tn):
        return (2 * tm * kdim * x_bytes + (2 * tm * kdim if has_norm else 0)
                + 2 * kdim * tn * 2 + 2 * tm * tn * out_bytes)

    tn = min(n, MAX_COL_TILE)
    while tn > MXU_DIM and (tm * kdim * tn > STEP_MACS or footprint(tn) > VMEM_TILE_BUDGET):
        tn //= 2
    assert n % tn == 0, (n, tn)
    return tn


def _tiled(w, tn):
    kdim, n = w.shape
    return w.reshape(kdim, n // tn, tn).transpose(1, 0, 2).astype(BF16)


def _compiler_params(semantics):
    return pltpu.CompilerParams(dimension_semantics=semantics,
                                vmem_limit_bytes=VMEM_LIMIT_BYTES)


def _rms_rows(x, g):
    ms = jnp.mean(x * x, axis=-1, keepdims=True)
    return x * lax.rsqrt(ms + NORM_EPS) * g


def _norm_prologue(x_ref, g_ref, a_ref):
    rows = x_ref.shape[0]
    step = min(rows, NORM_ROWS)

    def body(r, carry):
        sl = pl.ds(pl.multiple_of(r * step, step), step)
        a_ref[sl, :] = _rms_rows(x_ref[sl, :], g_ref[...]).astype(BF16)
        return carry

    trips = rows // step
    lax.fori_loop(0, trips, body, 0, unroll=math.gcd(trips, NORM_UNROLL))


def _linear_kernel(*refs, has_norm, n_extra, n_out, epilogue):
    x_ref = refs[0]
    pos = 1
    g_ref = None
    if has_norm:
        g_ref = refs[pos]
        pos += 1
    w_ref = refs[pos]
    pos += 1
    extra = refs[pos:pos + n_extra]
    pos += n_extra
    outs = refs[pos:pos + n_out]
    pos += n_out
    j = pl.program_id(1)

    if has_norm:
        lhs_ref = refs[pos]

        @pl.when(j == 0)
        def _():
            _norm_prologue(x_ref, g_ref, lhs_ref)
    else:
        lhs_ref = x_ref

    def product():
        return jnp.dot(lhs_ref[...], w_ref[...], preferred_element_type=F32)

    epilogue(product, j, extra, outs)


def _linear(x, w, *, name, tm, epilogue, out_widths, out_dtypes, norm_g=None, extras=(),
            extra_specs=(), out_col_tiles=None):
    m, kdim = x.shape
    nj, wk, tn = w.shape
    assert wk == kdim and m % tm == 0, (w.shape, x.shape, tm)
    n = nj * tn
    has_norm = norm_g is not None
    if out_col_tiles is None:
        out_col_tiles = [tn] * len(out_widths)

    in_specs = [pl.BlockSpec((tm, kdim), lambda i, j: (i, 0))]
    args = [x]
    if has_norm:
        in_specs.append(pl.BlockSpec((1, kdim), lambda i, j: (0, 0)))
        args.append(norm_g.reshape(1, kdim).astype(F32))
    in_specs.append(pl.BlockSpec((None, kdim, tn), lambda i, j: (j, 0, 0)))
    args.append(w)
    in_specs.extend(extra_specs)
    args.extend(extras)

    out_specs = [pl.BlockSpec((tm, ct), lambda i, j: (i, j)) for ct in out_col_tiles]
    out_shape = [jax.ShapeDtypeStruct((m, wd), dt) for wd, dt in zip(out_widths, out_dtypes)]
    scratch = [pltpu.VMEM((tm, kdim), BF16)] if has_norm else []

    kernel = functools.partial(_linear_kernel, has_norm=has_norm, n_extra=len(extras),
                               n_out=len(out_widths), epilogue=epilogue)
    return pl.pallas_call(
        kernel,
        grid=(m // tm, n // tn),
        in_specs=in_specs,
        out_specs=out_specs,
        out_shape=out_shape,
        scratch_shapes=scratch,
        compiler_params=_compiler_params(("parallel", "arbitrary")),
        name=name,
    )(*args)


def _ep_plain(product, j, extra, outs):
    outs[0][...] = product().astype(outs[0].dtype)


def _ep_relu2(product, j, extra, outs):
    r = jnp.maximum(product(), 0.0)
    outs[0][...] = (r * r).astype(outs[0].dtype)


def _ep_residual(product, j, extra, outs):
    outs[0][...] = extra[0][...] + product()


def _silu(x):
    return x * (1.0 / (1.0 + jnp.exp(-x)))


def _ep_ret_qkvg(product, j, extra, outs, *, n_rope, n_plain):
    cos_ref, sin_ref = extra
    o_ref = outs[0]
    tn = o_ref.shape[1]

    @pl.when(j < n_rope)
    def _():
        acc = product()
        c = cos_ref[...]
        s = sin_ref[...]
        for hh in range(tn // RET_DK):
            lo = hh * RET_DK
            mid = lo + RET_DK // 2
            hi = lo + RET_DK
            x1 = acc[:, lo:mid]
            x2 = acc[:, mid:hi]
            o_ref[:, lo:mid] = (x1 * c - x2 * s).astype(o_ref.dtype)
            o_ref[:, mid:hi] = (x1 * s + x2 * c).astype(o_ref.dtype)

    @pl.when(jnp.logical_and(j >= n_rope, j < n_rope + n_plain))
    def _():
        o_ref[...] = product().astype(o_ref.dtype)

    @pl.when(j >= n_rope + n_plain)
    def _():
        o_ref[...] = _silu(product()).astype(o_ref.dtype)


def _ep_mla_a(product, j, extra, outs):
    qn_ref, kvn_ref, cc_ref, ss_ref = extra
    cq_ref, ckv_ref, kr_ref = outs
    acc = product()
    a0 = MLA_Q_LORA
    a1 = a0 + MLA_KV_LORA
    a2 = a1 + LANES
    cq_ref[...] = _rms_rows(acc[:, :a0], qn_ref[...]).astype(cq_ref.dtype)
    ckv_ref[...] = _rms_rows(acc[:, a0:a1], kvn_ref[...]).astype(ckv_ref.dtype)
    kr = acc[:, a1:a2] * cc_ref[...] + acc[:, a2:] * ss_ref[...]
    kr_ref[...] = kr.astype(kr_ref.dtype)


def _ep_mla_q(product, j, extra, outs, *, scale):
    cc_ref, ss_ref = extra
    o_ref = outs[0]
    acc = product()
    cc = cc_ref[...]
    ss = ss_ref[...]
    for hh in range(o_ref.shape[1] // MXU_DIM):
        lo = hh * MXU_DIM
        mid = lo + LANES
        hi = lo + MXU_DIM
        o_ref[:, lo:mid] = (acc[:, lo:mid] * scale).astype(o_ref.dtype)
        x = acc[:, mid:hi]
        xr = pltpu.roll(x, LANES // 2, 1)
        o_ref[:, mid:hi] = ((x * cc + xr * ss) * scale).astype(o_ref.dtype)


def _row_spec(tm, width, period):
    return pl.BlockSpec((tm, width), lambda i, j: (i % period, 0))


def _linear_t_kernel(*refs, n_extra, epilogue):
    wt_ref, x_ref = refs[0], refs[1]
    extra = refs[2:2 + n_extra]
    o_ref = refs[2 + n_extra]

    def product():
        return lax.dot_general(wt_ref[...], x_ref[...], _NT, preferred_element_type=F32)

    epilogue(product, extra, o_ref)


def _linear_t(x, wt, *, name, tm, epilogue, extras=(), extra_specs=()):
    m, kdim = x.shape
    n = wt.shape[0]
    assert m % tm == 0, (m, tm)
    tn = _col_tile(kdim, n, x_bytes=2, has_norm=False, out_bytes=2, tm=tm)
    in_specs = [pl.BlockSpec((tn, kdim), lambda i, j: (j, 0)),
                pl.BlockSpec((tm, kdim), lambda i, j: (i, 0))]
    in_specs.extend(extra_specs)
    kernel = functools.partial(_linear_t_kernel, n_extra=len(extras), epilogue=epilogue)
    return pl.pallas_call(
        kernel,
        grid=(m // tm, n // tn),
        in_specs=in_specs,
        out_specs=pl.BlockSpec((tn, tm), lambda i, j: (j, i)),
        out_shape=jax.ShapeDtypeStruct((n, m), BF16),
        compiler_params=_compiler_params(("parallel", "arbitrary")),
        name=name,
    )(wt, x, *extras)


def _ep_t_plain(product, extra, o_ref):
    o_ref[...] = product().astype(o_ref.dtype)


def _ep_t_mla_q(product, extra, o_ref, *, scale):
    cc_ref, ss_ref = extra
    acc = product()
    cc = cc_ref[...]
    ss = ss_ref[...]
    half = LANES // 2
    for hh in range(o_ref.shape[0] // MXU_DIM):
        lo = hh * MXU_DIM
        mid = lo + LANES
        hi = lo + MXU_DIM
        o_ref[lo:mid, :] = (acc[lo:mid, :] * scale).astype(o_ref.dtype)
        x = acc[mid:hi, :]
        xr = jnp.concatenate([x[half:], x[:half]], axis=0)
        o_ref[mid:hi, :] = ((x * cc + xr * ss) * scale).astype(o_ref.dtype)


def _final_norm_kernel(x_ref, g_ref, o_ref):
    rows = x_ref.shape[0]
    step = min(rows, NORM_ROWS)

    def body(r, carry):
        sl = pl.ds(pl.multiple_of(r * step, step), step)
        o_ref[sl, :] = _rms_rows(x_ref[sl, :], g_ref[...])
        return carry

    lax.fori_loop(0, rows // step, body, 0)


def _final_norm(x, g):
    m, d = x.shape
    tm = _tile(m, ROW_TILE)
    return pl.pallas_call(
        _final_norm_kernel,
        grid=(m // tm,),
        in_specs=[pl.BlockSpec((tm, d), lambda i: (i, 0)),
                  pl.BlockSpec((1, d), lambda i: (0, 0))],
        out_specs=pl.BlockSpec((tm, d), lambda i: (i, 0)),
        out_shape=jax.ShapeDtypeStruct((m, d), F32),
        compiler_params=_compiler_params(("parallel",)),
        name="final_norm",
    )(x, g.reshape(1, d).astype(F32))


def _ret_kernel(cd_ref, q_ref, k_ref, v_ref, g_ref, qm_ref, km_ref, vm_ref, gm_ref,
                dmask_ref, qdf_ref, kdf_ref, qdb_ref, kdb_ref,
                o_ref, om_ref, rf_ref, rb_ref, opart_ref, *, n_groups, group):
    h = pl.program_id(1)
    t = pl.program_id(2)
    c = RET_CHUNK
    cd_f = cd_ref[h]
    cd_b = cd_ref[RET_HEADS + h]

    def decayed(x, dec_ref):
        return (x.astype(F32) * dec_ref[0]).astype(BF16)

    def pad_meta(ref):
        x = ref[...]
        return jnp.concatenate([jnp.zeros((c - N_META, x.shape[1]), x.dtype), x], axis=0)

    def chunk_rows(cidx):
        if isinstance(cidx, int):
            return slice(cidx * c, (cidx + 1) * c)
        return pl.ds(pl.multiple_of(cidx * c, c), c)

    def bwd_chunk(qc, kc, vc, cidx, update):
        ob = jnp.dot(decayed(qc, qdb_ref), rb_ref[...].astype(BF16), preferred_element_type=F32)
        opart_ref[chunk_rows(cidx), :] = ob
        if update:
            upd = lax.dot_general(decayed(kc, kdb_ref), vc, _TN, preferred_element_type=F32)
            rb_ref[...] = cd_b * rb_ref[...] + upd

    def fwd_chunk(qc, kc, vc, gc, cidx):
        s = lax.dot_general(qc, kc, _NT, preferred_element_type=F32) * dmask_ref[0]
        o = jnp.dot(s.astype(BF16), vc, preferred_element_type=F32)
        o = o + jnp.dot(decayed(qc, qdf_ref), rf_ref[...].astype(BF16), preferred_element_type=F32)
        o = o + opart_ref[chunk_rows(cidx), :]
        upd = lax.dot_general(decayed(kc, kdf_ref), vc, _TN, preferred_element_type=F32)
        rf_ref[...] = cd_f * rf_ref[...] + upd
        mu = jnp.mean(o, axis=-1, keepdims=True)
        xc = o - mu
        var = jnp.mean(xc * xc, axis=-1, keepdims=True)
        on = xc * lax.rsqrt(var + NORM_EPS)
        return (on * gc.astype(F32)).astype(BF16)

    @pl.when(t == 0)
    def _():
        rb_ref[...] = jnp.zeros_like(rb_ref)

    @pl.when(t < n_groups)
    def _():
        grp = n_groups - 1 - t
        for cc in range(group - 1, -1, -1):
            sl = slice(cc * c, (cc + 1) * c)
            bwd_chunk(q_ref[sl, :], k_ref[sl, :], v_ref[sl, :], 1 + grp * group + cc, True)

    @pl.when(t == n_groups - 1)
    def _():
        bwd_chunk(pad_meta(qm_ref), None, None, 0, False)

    @pl.when(t == n_groups)
    def _():
        rf_ref[...] = jnp.zeros_like(rf_ref)
        res = fwd_chunk(pad_meta(qm_ref), pad_meta(km_ref), pad_meta(vm_ref), pad_meta(gm_ref), 0)
        om_ref[...] = res[c - N_META:, :]

    @pl.when(t >= n_groups)
    def _():
        grp = t - n_groups
        for cc in range(group):
            sl = slice(cc * c, (cc + 1) * c)
            o_ref[sl, :] = fwd_chunk(q_ref[sl, :], k_ref[sl, :], v_ref[sl, :], g_ref[sl, :],
                                     1 + grp * group + cc)


def _retention_tables():
    c = RET_CHUNK
    hh = jnp.arange(RET_HEADS, dtype=F32)
    lg_f = jnp.log(1.0 - 2.0 ** (-5.0 - hh))
    lg_b = jnp.log(1.0 - 2.0 ** (-5.5 - hh))
    i = jnp.arange(c, dtype=F32)
    diff = i[:, None] - i[None, :]
    ad = jnp.abs(diff)[None]
    dmask = (jnp.where(diff[None] >= 0, jnp.exp(lg_f[:, None, None] * ad), 0.0)
             + jnp.where(diff[None] < 0, jnp.exp(lg_b[:, None, None] * ad), 0.0))

    def wide(v):
        return jnp.broadcast_to(v[:, :, None], (RET_HEADS, c, RET_DK))

    qd_f = wide(jnp.exp(lg_f[:, None] * (i + 1.0)[None]))
    kd_f = wide(jnp.exp(lg_f[:, None] * (c - 1.0 - i)[None]))
    qd_b = wide(jnp.exp(lg_b[:, None] * (c - i)[None]))
    kd_b = wide(jnp.exp(lg_b[:, None] * i[None]))
    cd = jnp.concatenate([jnp.exp(lg_f * c), jnp.exp(lg_b * c)])
    return cd, dmask, qd_f, kd_f, qd_b, kd_b


def _retention(qkvg_r, qkvg_m, batch, seq, tables):
    cd, dmask, qd_f, kd_f, qd_b, kd_b = tables
    c = RET_CHUNK
    n_real = seq // c
    group = _tile(n_real, RET_GROUP)
    n_groups = n_real // group
    rows = group * c
    k_off = RET_HEADS
    v_off = 2 * RET_HEADS * RET_DK // RET_DV
    g_off = v_off + RET_HEADS

    def sweep(t):
        return jnp.where(t < n_groups, n_groups - 1 - t, t - n_groups)

    def fwd_only(t):
        return jnp.maximum(t - n_groups, 0)

    in_specs = [
        pl.BlockSpec((rows, RET_DK), lambda b, h, t, cd: (b * n_groups + sweep(t), h)),
        pl.BlockSpec((rows, RET_DK), lambda b, h, t, cd: (b * n_groups + sweep(t), k_off + h)),
        pl.BlockSpec((rows, RET_DV), lambda b, h, t, cd: (b * n_groups + sweep(t), v_off + h)),
        pl.BlockSpec((rows, RET_DV), lambda b, h, t, cd: (b * n_groups + fwd_only(t), g_off + h)),
        pl.BlockSpec((N_META, RET_DK), lambda b, h, t, cd: (b, h)),
        pl.BlockSpec((N_META, RET_DK), lambda b, h, t, cd: (b, k_off + h)),
        pl.BlockSpec((N_META, RET_DV), lambda b, h, t, cd: (b, v_off + h)),
        pl.BlockSpec((N_META, RET_DV), lambda b, h, t, cd: (b, g_off + h)),
        pl.BlockSpec((1, c, c), lambda b, h, t, cd: (h, 0, 0)),
        pl.BlockSpec((1, c, RET_DK), lambda b, h, t, cd: (h, 0, 0)),
        pl.BlockSpec((1, c, RET_DK), lambda b, h, t, cd: (h, 0, 0)),
        pl.BlockSpec((1, c, RET_DK), lambda b, h, t, cd: (h, 0, 0)),
        pl.BlockSpec((1, c, RET_DK), lambda b, h, t, cd: (h, 0, 0)),
    ]
    out_specs = [
        pl.BlockSpec((rows, RET_DV), lambda b, h, t, cd: (b * n_groups + fwd_only(t), h)),
        pl.BlockSpec((N_META, RET_DV), lambda b, h, t, cd: (b, h)),
    ]
    out_shape = [
        jax.ShapeDtypeStruct((batch * seq, RET_HEADS * RET_DV), BF16),
        jax.ShapeDtypeStruct((batch * N_META, RET_HEADS * RET_DV), BF16),
    ]
    grid_spec = pltpu.PrefetchScalarGridSpec(
        num_scalar_prefetch=1,
        grid=(batch, RET_HEADS, 2 * n_groups),
        in_specs=in_specs,
        out_specs=out_specs,
        scratch_shapes=[
            pltpu.VMEM((RET_DK, RET_DV), F32),
            pltpu.VMEM((RET_DK, RET_DV), F32),
            pltpu.VMEM(((n_real + 1) * c, RET_DV), F32),
        ],
    )
    kernel = functools.partial(_ret_kernel, n_groups=n_groups, group=group)
    return pl.pallas_call(
        kernel,
        grid_spec=grid_spec,
        out_shape=out_shape,
        compiler_params=_compiler_params(("parallel", "parallel", "arbitrary")),
        name="retention",
    )(cd, qkvg_r, qkvg_r, qkvg_r, qkvg_r, qkvg_m, qkvg_m, qkvg_m, qkvg_m,
      dmask, qd_f, kd_f, qd_b, kd_b)


def _attn_kernel(qt_ref, kn_ref, kr_ref, vt_ref, knm_ref, krm_ref, vm_ref, o_ref, *, tk):
    qt = qt_ref[...]

    def scores(kn, kr):
        return jnp.dot(jnp.concatenate([kn, kr], axis=1), qt, preferred_element_type=F32)

    def block(c):
        sl = slice(c * tk, (c + 1) * tk)
        return scores(kn_ref[sl, :], kr_ref[sl, :])

    def values(vt, n):
        return jnp.concatenate([vt, jnp.ones((ONES_ROWS, n), BF16)], axis=0)

    nk = kn_ref.shape[0] // tk
    s = scores(knm_ref[...], krm_ref[...])
    s_next = block(0)
    m = jnp.max(s, axis=0, keepdims=True)
    p = jnp.exp2(s - m).astype(BF16)
    vm_t = vm_ref[...].astype(F32).T.astype(BF16)
    acc = jnp.dot(values(vm_t, N_META), p, preferred_element_type=F32)
    for c in range(nk):
        s = s_next
        if c + 1 < nk:
            s_next = block(c + 1)
        m_new = jnp.maximum(m, jnp.max(s, axis=0, keepdims=True))
        alpha = jnp.exp2(m - m_new)
        p = jnp.exp2(s - m_new).astype(BF16)
        sl = slice(c * tk, (c + 1) * tk)
        acc = alpha * acc + jnp.dot(values(vt_ref[:, sl], tk), p, preferred_element_type=F32)
        m = m_new
    o_ref[...] = (acc[:MLA_V] / acc[MLA_V:MLA_V + 1]).T.astype(o_ref.dtype)


def _attn_meta_kernel(q_ref, kn_ref, kr_ref, vt_ref, knm_ref, krm_ref, vm_ref, o_ref, *, tk, heads):
    def cols(hh, width):
        return slice(hh * width, (hh + 1) * width)

    qs = [q_ref[:, cols(hh, MXU_DIM)] for hh in range(heads)]

    def scores(q, kn, kr):
        return lax.dot_general(q, jnp.concatenate([kn, kr], axis=1), _NT, preferred_element_type=F32)

    state = []
    for hh in range(heads):
        s = scores(qs[hh], knm_ref[:, cols(hh, MLA_NOPE)], krm_ref[...])
        m0 = jnp.max(s, axis=1, keepdims=True)
        p = jnp.exp2(s - m0)
        state += [m0, jnp.sum(p, axis=1, keepdims=True),
                  jnp.dot(p.astype(BF16), vm_ref[:, cols(hh, MLA_V)], preferred_element_type=F32)]

    def body(c, state):
        sl = pl.ds(pl.multiple_of(c * tk, tk), tk)
        kr = kr_ref[sl, :]
        new = []
        for hh in range(heads):
            m_prev, l_prev, acc = state[3 * hh:3 * hh + 3]
            s = scores(qs[hh], kn_ref[sl, cols(hh, MLA_NOPE)], kr)
            m_new = jnp.maximum(m_prev, jnp.max(s, axis=1, keepdims=True))
            alpha = jnp.exp2(m_prev - m_new)
            p = jnp.exp2(s - m_new)
            pv = lax.dot_general(p.astype(BF16), vt_ref[cols(hh, MLA_V), sl], _NT,
                                 preferred_element_type=F32)
            new += [m_new, alpha * l_prev + jnp.sum(p, axis=1, keepdims=True), alpha * acc + pv]
        return tuple(new)

    state = lax.fori_loop(0, kn_ref.shape[0] // tk, body, tuple(state))
    for hh in range(heads):
        o_ref[:, cols(hh, MLA_V)] = (state[3 * hh + 2] / state[3 * hh + 1]).astype(o_ref.dtype)


def _attention(qt, q_m, kn, kr, vt, kv_m, kr_m, batch, seq):
    tq = _tile(seq, ATT_TQ)
    nq = seq // tq
    kv_specs = [
        pl.BlockSpec((seq, MLA_NOPE), lambda b, h, i: (b, h)),
        pl.BlockSpec((seq, LANES), lambda b, h, i: (b, 0)),
        pl.BlockSpec((MLA_V, seq), lambda b, h, i: (h, b)),
        pl.BlockSpec((N_META, MLA_NOPE), lambda b, h, i: (b, h)),
        pl.BlockSpec((N_META, LANES), lambda b, h, i: (b, 0)),
        pl.BlockSpec((N_META, MLA_V), lambda b, h, i: (b, MLA_HEADS + h)),
    ]
    kv_args = (kn, kr, vt, kv_m, kr_m, kv_m)
    o_r = pl.pallas_call(
        functools.partial(_attn_kernel, tk=_tile(seq, ATT_TK)),
        grid=(batch, MLA_HEADS, nq),
        in_specs=[pl.BlockSpec((MXU_DIM, tq), lambda b, h, i: (h, b * nq + i))] + kv_specs,
        out_specs=pl.BlockSpec((tq, MLA_V), lambda b, h, i: (b * nq + i, h)),
        out_shape=jax.ShapeDtypeStruct((batch * seq, MLA_HEADS * MLA_V), BF16),
        compiler_params=_compiler_params(("parallel", "parallel", "arbitrary")),
        name="attention",
    )(qt, *kv_args)
    if q_m is None:
        return o_r, None
    hg = ATT_META_HEADS
    n_hg = MLA_HEADS // hg
    o_m = pl.pallas_call(
        functools.partial(_attn_meta_kernel, tk=_tile(seq, ATT_META_TK), heads=hg),
        grid=(batch, n_hg),
        in_specs=[
            pl.BlockSpec((N_META, hg * MXU_DIM), lambda b, g: (b, g)),
            pl.BlockSpec((seq, hg * MLA_NOPE), lambda b, g: (b, g)),
            pl.BlockSpec((seq, LANES), lambda b, g: (b, 0)),
            pl.BlockSpec((hg * MLA_V, seq), lambda b, g: (g, b)),
            pl.BlockSpec((N_META, hg * MLA_NOPE), lambda b, g: (b, g)),
            pl.BlockSpec((N_META, LANES), lambda b, g: (b, 0)),
            pl.BlockSpec((N_META, hg * MLA_V), lambda b, g: (b, n_hg + g)),
        ],
        out_specs=pl.BlockSpec((N_META, hg * MLA_V), lambda b, g: (b, g)),
        out_shape=jax.ShapeDtypeStruct((batch * N_META, MLA_HEADS * MLA_V), BF16),
        compiler_params=_compiler_params(("parallel", "arbitrary")),
        name="attention_meta",
    )(q_m, *kv_args)
    return o_r, o_m


def _rope_tables(n, dim):
    inv = 1.0 / (ROPE_BASE ** (jnp.arange(0, dim, 2, dtype=F32) / dim))
    ang = jnp.arange(n, dtype=F32)[:, None] * inv[None, :]
    return jnp.cos(ang), jnp.sin(ang)


def _swap_halves(w):
    half = w.shape[-1] // 2
    return jnp.concatenate([w[..., half:], w[..., :half]], axis=-1)


def _prep_weights(p):
    d = D_MODEL
    out = {}
    norm_in = dict(x_bytes=4, has_norm=True, out_bytes=2)
    res_out = dict(x_bytes=2, has_norm=False, out_bytes=8)
    plain = dict(x_bytes=2, has_norm=False, out_bytes=2)
    tn_qkvg = _col_tile(d, 2 * RET_HEADS * RET_DK, **norm_in)
    out["ret_qkvg"] = [
        _tiled(jnp.concatenate([p["ret_wq"][j], p["ret_wk"][j] * (RET_DK ** -0.5),
                                p["ret_wv"][j], p["ret_wg"][j]], axis=1), tn_qkvg)
        for j in range(p["ret_wq"].shape[0])]
    tn_ro = _col_tile(RET_HEADS * RET_DV, d, **res_out)
    out["ret_wo"] = [_tiled(p["ret_wo"][j], tn_ro) for j in range(p["ret_wo"].shape[0])]
    tn_mo = _col_tile(MLA_HEADS * MLA_V, d, **res_out)
    tn_kn = _col_tile(MLA_KV_LORA, MLA_HEADS * MLA_NOPE, **plain)
    keys = ("mla_a", "mla_qb", "mla_qb_t", "mla_kvb", "mla_kn", "mla_v_t", "mla_wo")
    for key in keys:
        out[key] = []
    zeros = jnp.zeros((d, LANES - MLA_ROPE), F32)
    for j in range(p["mla_wq_a"].shape[0]):
        wkv_a = p["mla_wkv_a"][j]
        wr = wkv_a[:, MLA_KV_LORA:]
        w_a = jnp.concatenate(
            [p["mla_wq_a"][j], wkv_a[:, :MLA_KV_LORA], wr, zeros, _swap_halves(wr), zeros], axis=1)
        out["mla_a"].append(_tiled(w_a, w_a.shape[1]))
        wq_b = p["mla_wq_b"][j].reshape(MLA_Q_LORA, MLA_HEADS, MLA_NOPE + MLA_ROPE)
        rope = wq_b[..., MLA_NOPE:]
        qb = jnp.concatenate([wq_b[..., :MLA_NOPE], rope, _swap_halves(rope)], axis=-1)
        qb = qb.reshape(MLA_Q_LORA, MLA_HEADS * MXU_DIM)
        out["mla_qb"].append(_tiled(qb, MAX_COL_TILE))
        out["mla_qb_t"].append(qb.T.astype(BF16))
        wkv_b = p["mla_wkv_b"][j].reshape(MLA_KV_LORA, MLA_HEADS, MLA_NOPE + MLA_V)
        kn = wkv_b[..., :MLA_NOPE].reshape(MLA_KV_LORA, MLA_HEADS * MLA_NOPE)
        vv = wkv_b[..., MLA_NOPE:].reshape(MLA_KV_LORA, MLA_HEADS * MLA_V)
        out["mla_kvb"].append(_tiled(jnp.concatenate([kn, vv], axis=1), MAX_COL_TILE))
        out["mla_kn"].append(_tiled(kn, tn_kn))
        out["mla_v_t"].append(vv.T.astype(BF16))
        out["mla_wo"].append(_tiled(p["mla_wo"][j], tn_mo))
    tn_up = _col_tile(d, D_FF, **norm_in)
    tn_down = _col_tile(D_FF, d, **res_out)
    out["mlp_w1"] = [_tiled(p["mlp_w1"][i], tn_up) for i in range(p["mlp_w1"].shape[0])]
    out["mlp_w2"] = [_tiled(p["mlp_w2"][i], tn_down) for i in range(p["mlp_w2"].shape[0])]
    return out


def _position_tables(batch, seq):
    n = seq + N_META
    cos_r, sin_r = _rope_tables(n, RET_DK)
    cos_m, sin_m = _rope_tables(n, MLA_ROPE)
    pad = jnp.zeros((n, LANES - MLA_ROPE), F32)
    cc = jnp.concatenate([cos_m, cos_m, pad], axis=1)
    ss = jnp.concatenate([-sin_m, sin_m, pad], axis=1)

    def split(tbl):
        return tbl[N_META:], jnp.tile(tbl[:N_META], (batch, 1))

    return {"ret_cos": split(cos_r), "ret_sin": split(sin_r),
            "mla_cc": split(cc), "mla_ss": split(ss),
            "mla_cc_t": cc[N_META:].T, "mla_ss_t": ss[N_META:].T}


def _trunk(x, p, w, ret_tables):
    batch, seq, d = x.shape
    pos = _position_tables(batch, seq)
    h_r = x.reshape(batch * seq, d)
    h_m = jnp.broadcast_to(p["meta_tokens"].astype(F32)[None], (batch, N_META, d)).reshape(batch * N_META, d)
    depth = p["norm1_g"].shape[0]
    q_scale = math.log2(math.e) * (MLA_NOPE + MLA_ROPE) ** -0.5

    def both(fn, last=False):
        return fn(0), (None if last else fn(1))

    tms = (_tile(seq, ROW_TILE), batch * N_META)
    pers = (seq // tms[0], 1)

    for i in range(depth):
        last = i == depth - 1
        j = i // 2
        hs = (h_r, h_m)
        if i % 2 == 0:
            w_in = w["ret_qkvg"][j]

            def qkvg(which):
                hh = hs[which]
                tm, per = tms[which], pers[which]
                nj, _, tn = w_in.shape
                ep = functools.partial(_ep_ret_qkvg, n_rope=2 * RET_HEADS * RET_DK // tn,
                                       n_plain=RET_HEADS * RET_DV // tn)
                return _linear(hh, w_in, name="ret_qkvg", tm=tm, epilogue=ep, out_widths=[nj * tn],
                               out_dtypes=[BF16], norm_g=p["norm1_g"][i],
                               extras=(pos["ret_cos"][which], pos["ret_sin"][which]),
                               extra_specs=(_row_spec(tm, LANES, per), _row_spec(tm, LANES, per)))[0]

            qkvg_r, qkvg_m = both(qkvg)
            mix_r, mix_m = _retention(qkvg_r, qkvg_m, batch, seq, ret_tables)
            w_out = w["ret_wo"][j]
        else:
            w_a = w["mla_a"][j]

            def stage_a(which):
                hh = hs[which]
                tm, per = tms[which], pers[which]
                one = lambda width: pl.BlockSpec((1, width), lambda i_, j_: (0, 0))
                return _linear(hh, w_a, name="mla_a", tm=tm, epilogue=_ep_mla_a,
                               out_widths=[MLA_Q_LORA, MLA_KV_LORA, LANES], out_dtypes=[BF16] * 3,
                               out_col_tiles=[MLA_Q_LORA, MLA_KV_LORA, LANES],
                               norm_g=p["norm1_g"][i],
                               extras=(p["mla_q_norm"][j].reshape(1, -1), p["mla_kv_norm"][j].reshape(1, -1),
                                       pos["mla_cc"][which], pos["mla_ss"][which]),
                               extra_specs=(one(MLA_Q_LORA), one(MLA_KV_LORA),
                                            _row_spec(tm, LANES, per), _row_spec(tm, LANES, per)))

            (cq_r, ckv_r, kr_r), (cq_m, ckv_m, kr_m) = both(stage_a)
            tm_r, per_r = tms[0], pers[0]
            t_spec = pl.BlockSpec((LANES, tm_r), lambda i_, j_: (0, i_ % per_r))
            qt_r = _linear_t(cq_r, w["mla_qb_t"][j], name="mla_qb_t", tm=tm_r,
                             epilogue=functools.partial(_ep_t_mla_q, scale=q_scale),
                             extras=(pos["mla_cc_t"], pos["mla_ss_t"]), extra_specs=(t_spec, t_spec))
            kn_r = _linear(ckv_r, w["mla_kn"][j], name="mla_kn", tm=tm_r, epilogue=_ep_plain,
                           out_widths=[MLA_HEADS * MLA_NOPE], out_dtypes=[BF16])[0]
            vt_r = _linear_t(ckv_r, w["mla_v_t"][j], name="mla_v_t", tm=tm_r, epilogue=_ep_t_plain)
            tm_m = tms[1]
            kv_m = _linear(ckv_m, w["mla_kvb"][j], name="mla_kvb", tm=tm_m, epilogue=_ep_plain,
                           out_widths=[MLA_HEADS * (MLA_NOPE + MLA_V)], out_dtypes=[BF16])[0]
            q_m = None
            if not last:
                m_spec = _row_spec(tm_m, LANES, 1)
                q_m = _linear(cq_m, w["mla_qb"][j], name="mla_qb", tm=tm_m,
                              epilogue=functools.partial(_ep_mla_q, scale=q_scale),
                              out_widths=[MLA_HEADS * MXU_DIM], out_dtypes=[BF16],
                              extras=(pos["mla_cc"][1], pos["mla_ss"][1]), extra_specs=(m_spec, m_spec))[0]
            mix_r, mix_m = _attention(qt_r, q_m, kn_r, kr_r, vt_r, kv_m, kr_m, batch, seq)
            w_out = w["mla_wo"][j]

        mixes = (mix_r, mix_m)

        def proj_out(which):
            hh = hs[which]
            tm = tms[which]
            tn = w_out.shape[2]
            return _linear(mixes[which], w_out, name="mix_out", tm=tm, epilogue=_ep_residual,
                           out_widths=[d], out_dtypes=[F32],
                           extras=(hh,), extra_specs=(pl.BlockSpec((tm, tn), lambda i_, j_: (i_, j_)),))[0]

        h_r, h_m = both(proj_out, last)
        hs = (h_r, h_m)

        def mlp(which):
            hh = hs[which]
            tm = tms[which]
            tn = w["mlp_w2"][i].shape[2]
            hid = _linear(hh, w["mlp_w1"][i], name="mlp_up", tm=tm, epilogue=_ep_relu2,
                          out_widths=[D_FF], out_dtypes=[BF16], norm_g=p["norm2_g"][i])[0]
            return _linear(hid, w["mlp_w2"][i], name="mlp_down", tm=tm, epilogue=_ep_residual,
                           out_widths=[d], out_dtypes=[F32],
                           extras=(hh,), extra_specs=(pl.BlockSpec((tm, tn), lambda i_, j_: (i_, j_)),))[0]

        h_r, h_m = both(mlp, last)

    return _final_norm(h_r, p["final_norm"]).reshape(batch, seq, d)


def kernel(x_prompt, x_sample, meta_tokens, norm1_g, norm2_g, mlp_w1, mlp_w2, ret_wq, ret_wk, ret_wv, ret_wg, ret_wo, mla_wq_a, mla_q_norm, mla_wq_b, mla_wkv_a, mla_kv_norm, mla_wkv_b, mla_wo, final_norm):
    p = dict(meta_tokens=meta_tokens, norm1_g=norm1_g, norm2_g=norm2_g, mlp_w1=mlp_w1, mlp_w2=mlp_w2,
             ret_wq=ret_wq, ret_wk=ret_wk, ret_wv=ret_wv, ret_wg=ret_wg, ret_wo=ret_wo,
             mla_wq_a=mla_wq_a, mla_q_norm=mla_q_norm, mla_wq_b=mla_wq_b, mla_wkv_a=mla_wkv_a,
             mla_kv_norm=mla_kv_norm, mla_wkv_b=mla_wkv_b, mla_wo=mla_wo, final_norm=final_norm)
    w = _prep_weights(p)
    ret_tables = _retention_tables()
    return (_trunk(x_prompt, p, w, ret_tables), _trunk(x_sample, p, w, ret_tables))
```

```python
import functools
import math

import jax
import jax.numpy as jnp
from jax import lax
from jax.experimental import pallas as pl
from jax.experimental.pallas import tpu as pltpu

F32 = jnp.float32
BF16 = jnp.bfloat16

D_MODEL = 2048
N_META = 16
D_FF = 4 * D_MODEL
NORM_EPS = 1e-6
ROPE_BASE = 10000.0
RET_HEADS = 8
RET_DK = 256
RET_DV = 512
MLA_HEADS = 16
MLA_Q_LORA = 512
MLA_KV_LORA = 512
MLA_NOPE = 128
MLA_ROPE = 64
MLA_V = 128

LANES = 128
MXU_DIM = 256
VMEM_LIMIT_BYTES = 56 * 1024 * 1024
VMEM_TILE_BUDGET = 48 * 1024 * 1024

ROW_TILE = 1024
STEP_MACS = 2 ** 32
MAX_COL_TILE = 2048
DOWN_K_TILE = 1024
NORM_ROWS = 64
NORM_UNROLL = 4
RET_CHUNK = 256
RET_GROUP = 8
RET_SLAB = RET_DK
ATT_TQ = 1024
ATT_TK = 512
ATT_META_TK = 1024
ATT_META_HEADS = 4
ONES_ROWS = 16

_NT = (((1,), (1,)), ((), ()))
_TN = (((0,), (0,)), ((), ()))


def _tile(n, pref):
    t = min(n, pref)
    assert n % t == 0, (n, pref)
    return t


def _col_tile(kdim, n, *, x_bytes, has_norm, out_bytes, tm=ROW_TILE):
    def footprint(tn):
        return (2 * tm * kdim * x_bytes + (2 * tm * kdim if has_norm else 0)
                + 2 * kdim * tn * 2 + 2 * tm * tn * out_bytes)

    tn = min(n, MAX_COL_TILE)
    while tn > MXU_DIM and (tm * kdim * tn > STEP_MACS or footprint(tn) > VMEM_TILE_BUDGET):
        tn //= 2
    assert n % tn == 0, (n, tn)
    return tn


_NORM_IN = dict(x_bytes=4, has_norm=True, out_bytes=2)
_RES_OUT = dict(x_bytes=2, has_norm=False, out_bytes=8)
_PLAIN = dict(x_bytes=2, has_norm=False, out_bytes=2)


def _compiler_params(semantics):
    return pltpu.CompilerParams(dimension_semantics=semantics,
                                vmem_limit_bytes=VMEM_LIMIT_BYTES)


def _rms_rows(x, g):
    ms = jnp.mean(x * x, axis=-1, keepdims=True)
    return x * lax.rsqrt(ms + NORM_EPS) * g


def _norm_rows_pass(x_ref, g_ref, o_ref):
    rows = x_ref.shape[0]
    step = min(rows, NORM_ROWS)

    def body(r, carry):
        sl = pl.ds(pl.multiple_of(r * step, step), step)
        o_ref[sl, :] = _rms_rows(x_ref[sl, :], g_ref[...]).astype(o_ref.dtype)
        return carry

    trips = rows // step
    lax.fori_loop(0, trips, body, 0, unroll=math.gcd(trips, NORM_UNROLL))


def _linear_kernel(*refs, has_norm, n_slabs, n_extra, n_out, epilogue):
    x_ref = refs[0]
    pos = 1
    g_ref = None
    if has_norm:
        g_ref = refs[pos]
        pos += 1
    w_ref = refs[pos]
    pos += 1
    extra = refs[pos:pos + n_extra]
    pos += n_extra
    outs = refs[pos:pos + n_out]
    pos += n_out
    j = pl.program_id(1)

    if has_norm:
        a_ref = refs[pos]

        @pl.when(j == 0)
        def _():
            _norm_rows_pass(x_ref, g_ref, a_ref)

    def product():
        if has_norm:
            lhs = a_ref[...]
        elif n_slabs:
            lhs = jnp.concatenate([x_ref[s] for s in range(n_slabs)], axis=1)
        else:
            lhs = x_ref[...]
        return jnp.dot(lhs, w_ref[...], preferred_element_type=F32)

    epilogue(product, j, extra, outs)


def _linear(x, w, *, name, tm, tn, epilogue, out_widths=None, out_dtypes=None, norm_g=None,
            extras=(), extra_specs=(), out_col_tiles=None, out_specs=None, out_shape=None):
    kdim, n = w.shape
    n_slabs = x.shape[0] if x.ndim == 3 else 0
    m = x.shape[-2]
    assert m % tm == 0 and n % tn == 0, (x.shape, w.shape, tm, tn)
    has_norm = norm_g is not None

    if n_slabs:
        assert n_slabs * x.shape[2] == kdim
        in_specs = [pl.BlockSpec((n_slabs, tm, x.shape[2]), lambda i, j: (0, i, 0))]
    else:
        assert x.shape[1] == kdim
        in_specs = [pl.BlockSpec((tm, kdim), lambda i, j: (i, 0))]
    args = [x]
    if has_norm:
        in_specs.append(pl.BlockSpec((1, kdim), lambda i, j: (0, 0)))
        args.append(norm_g.reshape(1, kdim).astype(F32))
    in_specs.append(pl.BlockSpec((kdim, tn), lambda i, j: (0, j)))
    args.append(w)
    in_specs.extend(extra_specs)
    args.extend(extras)

    if out_specs is None:
        if out_col_tiles is None:
            out_col_tiles = [tn] * len(out_widths)
        out_specs = [pl.BlockSpec((tm, ct), lambda i, j: (i, j)) for ct in out_col_tiles]
        out_shape = [jax.ShapeDtypeStruct((m, wd), dt) for wd, dt in zip(out_widths, out_dtypes)]
    scratch = [pltpu.VMEM((tm, kdim), BF16)] if has_norm else []

    kernel = functools.partial(_linear_kernel, has_norm=has_norm, n_slabs=n_slabs,
                               n_extra=len(extras), n_out=len(out_specs), epilogue=epilogue)
    return pl.pallas_call(
        kernel,
        grid=(m // tm, n // tn),
        in_specs=in_specs,
        out_specs=out_specs,
        out_shape=out_shape,
        scratch_shapes=scratch,
        compiler_params=_compiler_params(("parallel", "arbitrary")),
        name=name,
    )(*args)


def _ep_plain(product, j, extra, outs):
    outs[0][...] = product().astype(outs[0].dtype)


def _ep_slabs(product, j, extra, outs):
    o_ref = outs[0]
    acc = product()
    width = o_ref.shape[2]
    for s in range(o_ref.shape[0]):
        o_ref[s] = acc[:, s * width:(s + 1) * width].astype(o_ref.dtype)


def _ep_relu2(product, j, extra, outs):
    r = jnp.maximum(product(), 0.0)
    outs[0][...] = (r * r).astype(outs[0].dtype)


def _ep_residual(product, j, extra, outs):
    outs[0][...] = extra[0][...] + product()


def _silu(x):
    return x * (1.0 / (1.0 + jnp.exp(-x)))


def _ep_ret_qkvg(product, j, extra, outs, *, n_rope, n_plain):
    cos_ref, sin_ref = extra
    o_ref = outs[0]
    n_slabs = o_ref.shape[0]
    half = RET_DK // 2

    @pl.when(j < n_rope)
    def _():
        acc = product()
        c = cos_ref[...]
        s = sin_ref[...]
        for hh in range(n_slabs):
            lo = hh * RET_DK
            x1 = acc[:, lo:lo + half]
            x2 = acc[:, lo + half:lo + RET_DK]
            o_ref[hh, :, :half] = (x1 * c - x2 * s).astype(o_ref.dtype)
            o_ref[hh, :, half:] = (x1 * s + x2 * c).astype(o_ref.dtype)

    @pl.when(jnp.logical_and(j >= n_rope, j < n_rope + n_plain))
    def _():
        acc = product()
        for hh in range(n_slabs):
            o_ref[hh] = acc[:, hh * RET_SLAB:(hh + 1) * RET_SLAB].astype(o_ref.dtype)

    @pl.when(j >= n_rope + n_plain)
    def _():
        acc = _silu(product())
        for hh in range(n_slabs):
            o_ref[hh] = acc[:, hh * RET_SLAB:(hh + 1) * RET_SLAB].astype(o_ref.dtype)


def _ep_mla_a(product, j, extra, outs):
    qn_ref, kvn_ref, cc_ref, ss_ref = extra
    cq_ref, ckv_ref, kr_ref = outs
    acc = product()
    a0 = MLA_Q_LORA
    a1 = a0 + MLA_KV_LORA
    a2 = a1 + LANES
    cq_ref[...] = _rms_rows(acc[:, :a0], qn_ref[...]).astype(cq_ref.dtype)
    ckv_ref[...] = _rms_rows(acc[:, a0:a1], kvn_ref[...]).astype(ckv_ref.dtype)
    kr = acc[:, a1:a2] * cc_ref[...] + acc[:, a2:] * ss_ref[...]
    kr_ref[...] = kr.astype(kr_ref.dtype)


def _ep_mla_q(product, j, extra, outs, *, scale):
    cc_ref, ss_ref = extra
    o_ref = outs[0]
    acc = product()
    cc = cc_ref[...]
    ss = ss_ref[...]
    for hh in range(o_ref.shape[1] // MXU_DIM):
        lo = hh * MXU_DIM
        mid = lo + LANES
        hi = lo + MXU_DIM
        o_ref[:, lo:mid] = (acc[:, lo:mid] * scale).astype(o_ref.dtype)
        x = acc[:, mid:hi]
        xr = pltpu.roll(x, LANES // 2, 1)
        o_ref[:, mid:hi] = ((x * cc + xr * ss) * scale).astype(o_ref.dtype)


def _row_spec(tm, width, period):
    return pl.BlockSpec((tm, width), lambda i, j: (i % period, 0))


def _slab_out(n_slabs, rows, width, tile_slabs, tm):
    spec = pl.BlockSpec((tile_slabs, tm, width), lambda i, j: (j, i, 0))
    return [spec], [jax.ShapeDtypeStruct((n_slabs, rows, width), BF16)]


def _linear_t_kernel(*refs, n_extra, epilogue):
    wt_ref, x_ref = refs[0], refs[1]
    extra = refs[2:2 + n_extra]
    o_ref = refs[2 + n_extra]

    def product():
        return lax.dot_general(wt_ref[...], x_ref[...], _NT, preferred_element_type=F32)

    epilogue(product, extra, o_ref)


def _linear_t(x, wt, *, name, tm, epilogue, extras=(), extra_specs=()):
    m, kdim = x.shape
    n = wt.shape[0]
    assert m % tm == 0, (m, tm)
    tn = _col_tile(kdim, n, tm=tm, **_PLAIN)
    in_specs = [pl.BlockSpec((tn, kdim), lambda i, j: (j, 0)),
                pl.BlockSpec((tm, kdim), lambda i, j: (i, 0))]
    in_specs.extend(extra_specs)
    kernel = functools.partial(_linear_t_kernel, n_extra=len(extras), epilogue=epilogue)
    return pl.pallas_call(
        kernel,
        grid=(m // tm, n // tn),
        in_specs=in_specs,
        out_specs=pl.BlockSpec((tn, tm), lambda i, j: (j, i)),
        out_shape=jax.ShapeDtypeStruct((n, m), BF16),
        compiler_params=_compiler_params(("parallel", "arbitrary")),
        name=name,
    )(wt, x, *extras)


def _ep_t_plain(product, extra, o_ref):
    o_ref[...] = product().astype(o_ref.dtype)


def _ep_t_mla_q(product, extra, o_ref, *, scale):
    cc_ref, ss_ref = extra
    acc = product()
    cc = cc_ref[...]
    ss = ss_ref[...]
    half = LANES // 2
    for hh in range(o_ref.shape[0] // MXU_DIM):
        lo = hh * MXU_DIM
        mid = lo + LANES
        hi = lo + MXU_DIM
        o_ref[lo:mid, :] = (acc[lo:mid, :] * scale).astype(o_ref.dtype)
        x = acc[mid:hi, :]
        xr = jnp.concatenate([x[half:], x[:half]], axis=0)
        o_ref[mid:hi, :] = ((x * cc + xr * ss) * scale).astype(o_ref.dtype)


def _mlp_down_kernel(*refs, nk, final_norm):
    if final_norm:
        x_ref, w_ref, res_ref, g_ref, o_ref = refs
    else:
        x_ref, w_ref, res_ref, o_ref = refs
    k = pl.program_id(1)

    def product():
        return jnp.dot(x_ref[...], w_ref[...], preferred_element_type=F32)

    @pl.when(k == 0)
    def _():
        o_ref[...] = res_ref[...] + product()

    @pl.when(k > 0)
    def _():
        o_ref[...] += product()

    if final_norm:
        @pl.when(k == nk - 1)
        def _():
            _norm_rows_pass(o_ref, g_ref, o_ref)


def _mlp_down(hid, w2, res, *, tm, final_g=None):
    m, kdim = hid.shape
    d = w2.shape[1]
    tk = _tile(kdim, DOWN_K_TILE)
    nk = kdim // tk
    final_norm = final_g is not None
    in_specs = [pl.BlockSpec((tm, tk), lambda i, k: (i, k)),
                pl.BlockSpec((tk, d), lambda i, k: (k, 0)),
                pl.BlockSpec((tm, d), lambda i, k: (i, 0))]
    args = [hid, w2, res]
    if final_norm:
        in_specs.append(pl.BlockSpec((1, d), lambda i, k: (0, 0)))
        args.append(final_g.reshape(1, d).astype(F32))
    return pl.pallas_call(
        functools.partial(_mlp_down_kernel, nk=nk, final_norm=final_norm),
        grid=(m // tm, nk),
        in_specs=in_specs,
        out_specs=pl.BlockSpec((tm, d), lambda i, k: (i, 0)),
        out_shape=jax.ShapeDtypeStruct((m, d), F32),
        compiler_params=_compiler_params(("parallel", "arbitrary")),
        name="mlp_down",
    )(*args)


def _ret_kernel(cd_ref, q_ref, k_ref, v_ref, g_ref, qm_ref, km_ref, vm_ref, gm_ref,
                dmask_ref, qdf_ref, kdf_ref, qdb_ref, kdb_ref,
                o_ref, om_ref, rf_ref, rb_ref, opart_ref, *, n_groups, group):
    h = pl.program_id(1)
    t = pl.program_id(2)
    c = RET_CHUNK
    cd_f = cd_ref[h]
    cd_b = cd_ref[RET_HEADS + h]

    def decayed(x, dec_ref):
        return (x.astype(F32) * dec_ref[0]).astype(BF16)

    def wide(ref, sl):
        return jnp.concatenate([ref[0, sl, :], ref[1, sl, :]], axis=1)

    def pad_meta(x):
        return jnp.concatenate([jnp.zeros((c - N_META, x.shape[1]), x.dtype), x], axis=0)

    def chunk_rows(cidx):
        if isinstance(cidx, int):
            return slice(cidx * c, (cidx + 1) * c)
        return pl.ds(pl.multiple_of(cidx * c, c), c)

    def bwd_chunk(qc, kc, vc, cidx, update):
        ob = jnp.dot(decayed(qc, qdb_ref), rb_ref[...].astype(BF16), preferred_element_type=F32)
        opart_ref[chunk_rows(cidx), :] = ob
        if update:
            upd = lax.dot_general(decayed(kc, kdb_ref), vc, _TN, preferred_element_type=F32)
            rb_ref[...] = cd_b * rb_ref[...] + upd

    def fwd_chunk(qc, kc, vc, gc, cidx):
        s = lax.dot_general(qc, kc, _NT, preferred_element_type=F32) * dmask_ref[0]
        o = jnp.dot(s.astype(BF16), vc, preferred_element_type=F32)
        o = o + jnp.dot(decayed(qc, qdf_ref), rf_ref[...].astype(BF16), preferred_element_type=F32)
        o = o + opart_ref[chunk_rows(cidx), :]
        upd = lax.dot_general(decayed(kc, kdf_ref), vc, _TN, preferred_element_type=F32)
        rf_ref[...] = cd_f * rf_ref[...] + upd
        mu = jnp.mean(o, axis=-1, keepdims=True)
        xc = o - mu
        var = jnp.mean(xc * xc, axis=-1, keepdims=True)
        on = xc * lax.rsqrt(var + NORM_EPS)
        return (on * gc.astype(F32)).astype(BF16)

    all_meta = slice(0, N_META)

    @pl.when(t == 0)
    def _():
        rb_ref[...] = jnp.zeros_like(rb_ref)

    @pl.when(t < n_groups)
    def _():
        grp = n_groups - 1 - t
        for cc in range(group - 1, -1, -1):
            sl = slice(cc * c, (cc + 1) * c)
            bwd_chunk(q_ref[sl, :], k_ref[sl, :], wide(v_ref, sl), 1 + grp * group + cc, True)

    @pl.when(t == n_groups - 1)
    def _():
        bwd_chunk(pad_meta(qm_ref[...]), None, None, 0, False)

    @pl.when(t == n_groups)
    def _():
        rf_ref[...] = jnp.zeros_like(rf_ref)
        res = fwd_chunk(pad_meta(qm_ref[...]), pad_meta(km_ref[...]), pad_meta(wide(vm_ref, all_meta)),
                        pad_meta(wide(gm_ref, all_meta)), 0)
        om_ref[0] = res[c - N_META:, :RET_SLAB]
        om_ref[1] = res[c - N_META:, RET_SLAB:]

    @pl.when(t >= n_groups)
    def _():
        grp = t - n_groups
        for cc in range(group):
            sl = slice(cc * c, (cc + 1) * c)
            res = fwd_chunk(q_ref[sl, :], k_ref[sl, :], wide(v_ref, sl), wide(g_ref, sl),
                            1 + grp * group + cc)
            o_ref[0, sl, :] = res[:, :RET_SLAB]
            o_ref[1, sl, :] = res[:, RET_SLAB:]


def _retention_tables():
    c = RET_CHUNK
    hh = jnp.arange(RET_HEADS, dtype=F32)
    lg_f = jnp.log(1.0 - 2.0 ** (-5.0 - hh))
    lg_b = jnp.log(1.0 - 2.0 ** (-5.5 - hh))
    i = jnp.arange(c, dtype=F32)
    diff = i[:, None] - i[None, :]
    ad = jnp.abs(diff)[None]
    dmask = (jnp.where(diff[None] >= 0, jnp.exp(lg_f[:, None, None] * ad), 0.0)
             + jnp.where(diff[None] < 0, jnp.exp(lg_b[:, None, None] * ad), 0.0))

    def wide(v):
        return jnp.broadcast_to(v[:, :, None], (RET_HEADS, c, RET_DK))

    qd_f = wide(jnp.exp(lg_f[:, None] * (i + 1.0)[None]))
    kd_f = wide(jnp.exp(lg_f[:, None] * (c - 1.0 - i)[None]))
    qd_b = wide(jnp.exp(lg_b[:, None] * (c - i)[None]))
    kd_b = wide(jnp.exp(lg_b[:, None] * i[None]))
    cd = jnp.concatenate([jnp.exp(lg_f * c), jnp.exp(lg_b * c)])
    return cd, dmask, qd_f, kd_f, qd_b, kd_b


def _retention(qkvg_r, qkvg_m, batch, seq, tables):
    cd, dmask, qd_f, kd_f, qd_b, kd_b = tables
    c = RET_CHUNK
    n_real = seq // c
    group = _tile(n_real, RET_GROUP)
    n_groups = n_real // group
    rows = group * c
    v_per = RET_DV // RET_SLAB
    k_off = RET_HEADS
    v_off = 2 * RET_HEADS // v_per
    g_off = v_off + RET_HEADS

    def sweep(t):
        return jnp.where(t < n_groups, n_groups - 1 - t, t - n_groups)

    def fwd_only(t):
        return jnp.maximum(t - n_groups, 0)

    in_specs = [
        pl.BlockSpec((None, rows, RET_SLAB), lambda b, h, t, cd: (h, b * n_groups + sweep(t), 0)),
        pl.BlockSpec((None, rows, RET_SLAB), lambda b, h, t, cd: (k_off + h, b * n_groups + sweep(t), 0)),
        pl.BlockSpec((v_per, rows, RET_SLAB), lambda b, h, t, cd: (v_off + h, b * n_groups + sweep(t), 0)),
        pl.BlockSpec((v_per, rows, RET_SLAB), lambda b, h, t, cd: (g_off + h, b * n_groups + fwd_only(t), 0)),
        pl.BlockSpec((None, N_META, RET_SLAB), lambda b, h, t, cd: (h, b, 0)),
        pl.BlockSpec((None, N_META, RET_SLAB), lambda b, h, t, cd: (k_off + h, b, 0)),
        pl.BlockSpec((v_per, N_META, RET_SLAB), lambda b, h, t, cd: (v_off + h, b, 0)),
        pl.BlockSpec((v_per, N_META, RET_SLAB), lambda b, h, t, cd: (g_off + h, b, 0)),
        pl.BlockSpec((1, c, c), lambda b, h, t, cd: (h, 0, 0)),
        pl.BlockSpec((1, c, RET_DK), lambda b, h, t, cd: (h, 0, 0)),
        pl.BlockSpec((1, c, RET_DK), lambda b, h, t, cd: (h, 0, 0)),
        pl.BlockSpec((1, c, RET_DK), lambda b, h, t, cd: (h, 0, 0)),
        pl.BlockSpec((1, c, RET_DK), lambda b, h, t, cd: (h, 0, 0)),
    ]
    out_specs = [
        pl.BlockSpec((v_per, rows, RET_SLAB), lambda b, h, t, cd: (h, b * n_groups + fwd_only(t), 0)),
        pl.BlockSpec((v_per, N_META, RET_SLAB), lambda b, h, t, cd: (h, b, 0)),
    ]
    out_shape = [
        jax.ShapeDtypeStruct((RET_HEADS * v_per, batch * seq, RET_SLAB), BF16),
        jax.ShapeDtypeStruct((RET_HEADS * v_per, batch * N_META, RET_SLAB), BF16),
    ]
    grid_spec = pltpu.PrefetchScalarGridSpec(
        num_scalar_prefetch=1,
        grid=(batch, RET_HEADS, 2 * n_groups),
        in_specs=in_specs,
        out_specs=out_specs,
        scratch_shapes=[
            pltpu.VMEM((RET_DK, RET_DV), F32),
            pltpu.VMEM((RET_DK, RET_DV), F32),
            pltpu.VMEM(((n_real + 1) * c, RET_DV), F32),
        ],
    )
    kernel = functools.partial(_ret_kernel, n_groups=n_groups, group=group)
    return pl.pallas_call(
        kernel,
        grid_spec=grid_spec,
        out_shape=out_shape,
        compiler_params=_compiler_params(("parallel", "parallel", "arbitrary")),
        name="retention",
    )(cd, qkvg_r, qkvg_r, qkvg_r, qkvg_r, qkvg_m, qkvg_m, qkvg_m, qkvg_m,
      dmask, qd_f, kd_f, qd_b, kd_b)


def _attn_kernel(qt_ref, kn_ref, kr_ref, vt_ref, knm_ref, krm_ref, vm_ref, o_ref, *, tk):
    qt = qt_ref[...]

    def scores(kn, kr):
        return jnp.dot(jnp.concatenate([kn, kr], axis=1), qt, preferred_element_type=F32)

    def block(c):
        sl = slice(c * tk, (c + 1) * tk)
        return scores(kn_ref[sl, :], kr_ref[sl, :])

    def values(vt, n):
        return jnp.concatenate([vt, jnp.ones((ONES_ROWS, n), BF16)], axis=0)

    nk = kn_ref.shape[0] // tk
    s = scores(knm_ref[...], krm_ref[...])
    s_next = block(0)
    m = jnp.max(s, axis=0, keepdims=True)
    p = jnp.exp2(s - m).astype(BF16)
    vm_t = vm_ref[...].astype(F32).T.astype(BF16)
    acc = jnp.dot(values(vm_t, N_META), p, preferred_element_type=F32)
    for c in range(nk):
        s = s_next
        if c + 1 < nk:
            s_next = block(c + 1)
        m_new = jnp.maximum(m, jnp.max(s, axis=0, keepdims=True))
        alpha = jnp.exp2(m - m_new)
        p = jnp.exp2(s - m_new).astype(BF16)
        sl = slice(c * tk, (c + 1) * tk)
        acc = alpha * acc + jnp.dot(values(vt_ref[:, sl], tk), p, preferred_element_type=F32)
        m = m_new
    o_ref[...] = (acc[:MLA_V] / acc[MLA_V:MLA_V + 1]).T.astype(o_ref.dtype)


def _attn_meta_kernel(q_ref, kn_ref, kr_ref, vt_ref, knm_ref, krm_ref, vm_ref, o_ref, *, tk, heads):
    def cols(hh, width):
        return slice(hh * width, (hh + 1) * width)

    qs = [q_ref[:, cols(hh, MXU_DIM)] for hh in range(heads)]

    def scores(q, kn, kr):
        return lax.dot_general(q, jnp.concatenate([kn, kr], axis=1), _NT, preferred_element_type=F32)

    state = []
    for hh in range(heads):
        s = scores(qs[hh], knm_ref[:, cols(hh, MLA_NOPE)], krm_ref[...])
        m0 = jnp.max(s, axis=1, keepdims=True)
        p = jnp.exp2(s - m0)
        state += [m0, jnp.sum(p, axis=1, keepdims=True),
                  jnp.dot(p.astype(BF16), vm_ref[:, cols(hh, MLA_V)], preferred_element_type=F32)]

    def body(c, state):
        sl = pl.ds(pl.multiple_of(c * tk, tk), tk)
        kr = kr_ref[sl, :]
        new = []
        for hh in range(heads):
            m_prev, l_prev, acc = state[3 * hh:3 * hh + 3]
            s = scores(qs[hh], kn_ref[hh, sl, :], kr)
            m_new = jnp.maximum(m_prev, jnp.max(s, axis=1, keepdims=True))
            alpha = jnp.exp2(m_prev - m_new)
            p = jnp.exp2(s - m_new)
            pv = lax.dot_general(p.astype(BF16), vt_ref[cols(hh, MLA_V), sl], _NT,
                                 preferred_element_type=F32)
            new += [m_new, alpha * l_prev + jnp.sum(p, axis=1, keepdims=True), alpha * acc + pv]
        return tuple(new)

    state = lax.fori_loop(0, kn_ref.shape[1] // tk, body, tuple(state))
    for hh in range(heads):
        o_ref[hh] = (state[3 * hh + 2] / state[3 * hh + 1]).astype(o_ref.dtype)


def _attention(qt, q_m, kn, kr, vt, kv_m, kr_m, batch, seq):
    tq = _tile(seq, ATT_TQ)
    nq = seq // tq
    o_r = pl.pallas_call(
        functools.partial(_attn_kernel, tk=_tile(seq, ATT_TK)),
        grid=(batch, MLA_HEADS, nq),
        in_specs=[
            pl.BlockSpec((MXU_DIM, tq), lambda b, h, i: (h, b * nq + i)),
            pl.BlockSpec((None, seq, MLA_NOPE), lambda b, h, i: (h, b, 0)),
            pl.BlockSpec((seq, LANES), lambda b, h, i: (b, 0)),
            pl.BlockSpec((MLA_V, seq), lambda b, h, i: (h, b)),
            pl.BlockSpec((N_META, MLA_NOPE), lambda b, h, i: (b, h)),
            pl.BlockSpec((N_META, LANES), lambda b, h, i: (b, 0)),
            pl.BlockSpec((N_META, MLA_V), lambda b, h, i: (b, MLA_HEADS + h)),
        ],
        out_specs=pl.BlockSpec((None, tq, MLA_V), lambda b, h, i: (h, b * nq + i, 0)),
        out_shape=jax.ShapeDtypeStruct((MLA_HEADS, batch * seq, MLA_V), BF16),
        compiler_params=_compiler_params(("parallel", "parallel", "arbitrary")),
        name="attention",
    )(qt, kn, kr, vt, kv_m, kr_m, kv_m)
    if q_m is None:
        return o_r, None
    hg = ATT_META_HEADS
    n_hg = MLA_HEADS // hg
    o_m = pl.pallas_call(
        functools.partial(_attn_meta_kernel, tk=_tile(seq, ATT_META_TK), heads=hg),
        grid=(batch, n_hg),
        in_specs=[
            pl.BlockSpec((N_META, hg * MXU_DIM), lambda b, g: (b, g)),
            pl.BlockSpec((hg, seq, MLA_NOPE), lambda b, g: (g, b, 0)),
            pl.BlockSpec((seq, LANES), lambda b, g: (b, 0)),
            pl.BlockSpec((hg * MLA_V, seq), lambda b, g: (g, b)),
            pl.BlockSpec((N_META, hg * MLA_NOPE), lambda b, g: (b, g)),
            pl.BlockSpec((N_META, LANES), lambda b, g: (b, 0)),
            pl.BlockSpec((N_META, hg * MLA_V), lambda b, g: (b, n_hg + g)),
        ],
        out_specs=pl.BlockSpec((hg, N_META, MLA_V), lambda b, g: (g, b, 0)),
        out_shape=jax.ShapeDtypeStruct((MLA_HEADS, batch * N_META, MLA_V), BF16),
        compiler_params=_compiler_params(("parallel", "arbitrary")),
        name="attention_meta",
    )(q_m, kn, kr, vt, kv_m, kr_m, kv_m)
    return o_r, o_m


def _rope_tables(n, dim):
    inv = 1.0 / (ROPE_BASE ** (jnp.arange(0, dim, 2, dtype=F32) / dim))
    ang = jnp.arange(n, dtype=F32)[:, None] * inv[None, :]
    return jnp.cos(ang), jnp.sin(ang)


def _swap_halves(w):
    half = w.shape[-1] // 2
    return jnp.concatenate([w[..., half:], w[..., :half]], axis=-1)


def _prep_weights(p):
    d = D_MODEL
    out = {}
    out["ret_qkvg"] = [
        jnp.concatenate([p["ret_wq"][j], p["ret_wk"][j] * (RET_DK ** -0.5),
                         p["ret_wv"][j], p["ret_wg"][j]], axis=1).astype(BF16)
        for j in range(p["ret_wq"].shape[0])]
    out["ret_wo"] = [p["ret_wo"][j].astype(BF16) for j in range(p["ret_wo"].shape[0])]
    keys = ("mla_a", "mla_qb", "mla_qb_t", "mla_kvb", "mla_kn", "mla_v_t", "mla_wo")
    for key in keys:
        out[key] = []
    zeros = jnp.zeros((d, LANES - MLA_ROPE), F32)
    for j in range(p["mla_wq_a"].shape[0]):
        wkv_a = p["mla_wkv_a"][j]
        wr = wkv_a[:, MLA_KV_LORA:]
        out["mla_a"].append(jnp.concatenate(
            [p["mla_wq_a"][j], wkv_a[:, :MLA_KV_LORA], wr, zeros, _swap_halves(wr), zeros],
            axis=1).astype(BF16))
        wq_b = p["mla_wq_b"][j].reshape(MLA_Q_LORA, MLA_HEADS, MLA_NOPE + MLA_ROPE)
        rope = wq_b[..., MLA_NOPE:]
        qb = jnp.concatenate([wq_b[..., :MLA_NOPE], rope, _swap_halves(rope)], axis=-1)
        qb = qb.reshape(MLA_Q_LORA, MLA_HEADS * MXU_DIM).astype(BF16)
        out["mla_qb"].append(qb)
        out["mla_qb_t"].append(qb.T)
        wkv_b = p["mla_wkv_b"][j].reshape(MLA_KV_LORA, MLA_HEADS, MLA_NOPE + MLA_V)
        kn = wkv_b[..., :MLA_NOPE].reshape(MLA_KV_LORA, MLA_HEADS * MLA_NOPE).astype(BF16)
        vv = wkv_b[..., MLA_NOPE:].reshape(MLA_KV_LORA, MLA_HEADS * MLA_V).astype(BF16)
        out["mla_kvb"].append(jnp.concatenate([kn, vv], axis=1))
        out["mla_kn"].append(kn)
        out["mla_v_t"].append(vv.T)
        out["mla_wo"].append(p["mla_wo"][j].astype(BF16))
    out["mlp_w1"] = [p["mlp_w1"][i].astype(BF16) for i in range(p["mlp_w1"].shape[0])]
    out["mlp_w2"] = [p["mlp_w2"][i].astype(BF16) for i in range(p["mlp_w2"].shape[0])]
    return out


def _position_tables(batch, seq):
    n = seq + N_META
    cos_r, sin_r = _rope_tables(n, RET_DK)
    cos_m, sin_m = _rope_tables(n, MLA_ROPE)
    pad = jnp.zeros((n, LANES - MLA_ROPE), F32)
    cc = jnp.concatenate([cos_m, cos_m, pad], axis=1)
    ss = jnp.concatenate([-sin_m, sin_m, pad], axis=1)

    def split(tbl):
        return tbl[N_META:], jnp.tile(tbl[:N_META], (batch, 1))

    return {"ret_cos": split(cos_r), "ret_sin": split(sin_r),
            "mla_cc": split(cc), "mla_ss": split(ss),
            "mla_cc_t": cc[N_META:].T, "mla_ss_t": ss[N_META:].T}


def _trunk(x, p, w, ret_tables):
    batch, seq, d = x.shape
    pos = _position_tables(batch, seq)
    h_r = x.reshape(batch * seq, d)
    h_m = jnp.broadcast_to(p["meta_tokens"].astype(F32)[None], (batch, N_META, d)).reshape(batch * N_META, d)
    depth = p["norm1_g"].shape[0]
    q_scale = math.log2(math.e) * (MLA_NOPE + MLA_ROPE) ** -0.5

    def both(fn, last=False):
        return fn(0), (None if last else fn(1))

    tms = (_tile(seq, ROW_TILE), batch * N_META)
    pers = (seq // tms[0], 1)
    n_rows = (batch * seq, batch * N_META)

    for i in range(depth):
        last = i == depth - 1
        j = i // 2
        hs = (h_r, h_m)
        if i % 2 == 0:
            w_in = w["ret_qkvg"][j]
            tn = _col_tile(d, 2 * RET_HEADS * RET_DK, **_NORM_IN)
            ep = functools.partial(_ep_ret_qkvg, n_rope=2 * RET_HEADS * RET_DK // tn,
                                   n_plain=RET_HEADS * RET_DV // tn)

            def qkvg(which):
                tm, per = tms[which], pers[which]
                o_specs, o_shape = _slab_out(w_in.shape[1] // RET_SLAB, n_rows[which], RET_SLAB,
                                             tn // RET_SLAB, tm)
                return _linear(hs[which], w_in, name="ret_qkvg", tm=tm, tn=tn, epilogue=ep,
                               norm_g=p["norm1_g"][i], out_specs=o_specs, out_shape=o_shape,
                               extras=(pos["ret_cos"][which], pos["ret_sin"][which]),
                               extra_specs=(_row_spec(tm, LANES, per), _row_spec(tm, LANES, per)))[0]

            qkvg_r, qkvg_m = both(qkvg)
            mix_r, mix_m = _retention(qkvg_r, qkvg_m, batch, seq, ret_tables)
            w_out = w["ret_wo"][j]
        else:
            w_a = w["mla_a"][j]

            def stage_a(which):
                tm, per = tms[which], pers[which]
                one = lambda width: pl.BlockSpec((1, width), lambda i_, j_: (0, 0))
                return _linear(hs[which], w_a, name="mla_a", tm=tm, tn=w_a.shape[1], epilogue=_ep_mla_a,
                               out_widths=[MLA_Q_LORA, MLA_KV_LORA, LANES], out_dtypes=[BF16] * 3,
                               out_col_tiles=[MLA_Q_LORA, MLA_KV_LORA, LANES],
                               norm_g=p["norm1_g"][i],
                               extras=(p["mla_q_norm"][j].reshape(1, -1), p["mla_kv_norm"][j].reshape(1, -1),
                                       pos["mla_cc"][which], pos["mla_ss"][which]),
                               extra_specs=(one(MLA_Q_LORA), one(MLA_KV_LORA),
                                            _row_spec(tm, LANES, per), _row_spec(tm, LANES, per)))

            (cq_r, ckv_r, kr_r), (cq_m, ckv_m, kr_m) = both(stage_a)
            tm_r, per_r = tms[0], pers[0]
            t_spec = pl.BlockSpec((LANES, tm_r), lambda i_, j_: (0, i_ % per_r))
            qt_r = _linear_t(cq_r, w["mla_qb_t"][j], name="mla_qb_t", tm=tm_r,
                             epilogue=functools.partial(_ep_t_mla_q, scale=q_scale),
                             extras=(pos["mla_cc_t"], pos["mla_ss_t"]), extra_specs=(t_spec, t_spec))
            tn_kn = _col_tile(MLA_KV_LORA, MLA_HEADS * MLA_NOPE, **_PLAIN)
            kn_specs, kn_shape = _slab_out(MLA_HEADS, n_rows[0], MLA_NOPE, tn_kn // MLA_NOPE, tm_r)
            kn_r = _linear(ckv_r, w["mla_kn"][j], name="mla_kn", tm=tm_r, tn=tn_kn, epilogue=_ep_slabs,
                           out_specs=kn_specs, out_shape=kn_shape)[0]
            vt_r = _linear_t(ckv_r, w["mla_v_t"][j], name="mla_v_t", tm=tm_r, epilogue=_ep_t_plain)
            tm_m = tms[1]
            kv_m = _linear(ckv_m, w["mla_kvb"][j], name="mla_kvb", tm=tm_m, tn=MAX_COL_TILE,
                           epilogue=_ep_plain, out_widths=[MLA_HEADS * (MLA_NOPE + MLA_V)],
                           out_dtypes=[BF16])[0]
            q_m = None
            if not last:
                m_spec = _row_spec(tm_m, LANES, 1)
                q_m = _linear(cq_m, w["mla_qb"][j], name="mla_qb", tm=tm_m, tn=MAX_COL_TILE,
                              epilogue=functools.partial(_ep_mla_q, scale=q_scale),
                              out_widths=[MLA_HEADS * MXU_DIM], out_dtypes=[BF16],
                              extras=(pos["mla_cc"][1], pos["mla_ss"][1]), extra_specs=(m_spec, m_spec))[0]
            mix_r, mix_m = _attention(qt_r, q_m, kn_r, kr_r, vt_r, kv_m, kr_m, batch, seq)
            w_out = w["mla_wo"][j]

        mixes = (mix_r, mix_m)
        tn_out = _col_tile(w_out.shape[0], d, **_RES_OUT)

        def proj_out(which):
            tm = tms[which]
            return _linear(mixes[which], w_out, name="mix_out", tm=tm, tn=tn_out, epilogue=_ep_residual,
                           out_widths=[d], out_dtypes=[F32], extras=(hs[which],),
                           extra_specs=(pl.BlockSpec((tm, tn_out), lambda i_, j_: (i_, j_)),))[0]

        h_r, h_m = both(proj_out, last)
        hs = (h_r, h_m)
        tn_up = _col_tile(d, D_FF, **_NORM_IN)

        def mlp(which):
            tm = tms[which]
            hid = _linear(hs[which], w["mlp_w1"][i], name="mlp_up", tm=tm, tn=tn_up, epilogue=_ep_relu2,
                          out_widths=[D_FF], out_dtypes=[BF16], norm_g=p["norm2_g"][i])[0]
            return _mlp_down(hid, w["mlp_w2"][i], hs[which], tm=tm,
                             final_g=p["final_norm"] if last else None)

        h_r, h_m = both(mlp, last)

    return h_r.reshape(batch, seq, d)


def kernel(x_prompt, x_sample, meta_tokens, norm1_g, norm2_g, mlp_w1, mlp_w2, ret_wq, ret_wk, ret_wv, ret_wg, ret_wo, mla_wq_a, mla_q_norm, mla_wq_b, mla_wkv_a, mla_kv_norm, mla_wkv_b, mla_wo, final_norm):
    p = dict(meta_tokens=meta_tokens, norm1_g=norm1_g, norm2_g=norm2_g, mlp_w1=mlp_w1, mlp_w2=mlp_w2,
             ret_wq=ret_wq, ret_wk=ret_wk, ret_wv=ret_wv, ret_wg=ret_wg, ret_wo=ret_wo,
             mla_wq_a=mla_wq_a, mla_q_norm=mla_q_norm, mla_wq_b=mla_wq_b, mla_wkv_a=mla_wkv_a,
             mla_kv_norm=mla_kv_norm, mla_wkv_b=mla_wkv_b, mla_wo=mla_wo, final_norm=final_norm)
    w = _prep_weights(p)
    ret_tables = _retention_tables()
    return (_trunk(x_prompt, p, w, ret_tables), _trunk(x_sample, p, w, ret_tables))
```

```python
import functools
import math

import jax
import jax.numpy as jnp
from jax import lax
from jax.experimental import pallas as pl
from jax.experimental.pallas import tpu as pltpu

F32 = jnp.float32
BF16 = jnp.bfloat16

D_MODEL = 2048
N_META = 16
D_FF = 4 * D_MODEL
NORM_EPS = 1e-6
ROPE_BASE = 10000.0
RET_HEADS = 8
RET_DK = 256
RET_DV = 512
MLA_HEADS = 16
MLA_Q_LORA = 512
MLA_KV_LORA = 512
MLA_NOPE = 128
MLA_ROPE = 64
MLA_V = 128

LANES = 128
MXU_DIM = 256
VMEM_LIMIT_BYTES = 56 * 1024 * 1024
VMEM_TILE_BUDGET = 48 * 1024 * 1024

ROW_TILE = 1024
STEP_MACS = 2 ** 32
MAX_COL_TILE = 2048
DOWN_K_TILE = 1024
NORM_ROWS = 64
NORM_UNROLL = 4
NORM_SUBTILES = 4
RET_CHUNK = 256
RET_GROUP = 8
RET_SLAB = RET_DK
ATT_TQ = 1024
ATT_TK = 512
ATT_META_TK = 1024
ATT_META_HEADS = 4
ONES_ROWS = 16

_NT = (((1,), (1,)), ((), ()))
_TN = (((0,), (0,)), ((), ()))


def _tile(n, pref):
    t = min(n, pref)
    assert n % t == 0, (n, pref)
    return t


def _col_tile(kdim, n, *, x_bytes, has_norm, out_bytes, tm=ROW_TILE):
    def footprint(tn):
        return (2 * tm * kdim * x_bytes + (2 * tm * kdim if has_norm else 0)
                + 2 * kdim * tn * 2 + 2 * tm * tn * out_bytes)

    tn = min(n, MAX_COL_TILE)
    while tn > MXU_DIM and (tm * kdim * tn > STEP_MACS or footprint(tn) > VMEM_TILE_BUDGET):
        tn //= 2
    assert n % tn == 0, (n, tn)
    return tn


_NORM_IN = dict(x_bytes=4, has_norm=True, out_bytes=2)
_RES_OUT = dict(x_bytes=2, has_norm=False, out_bytes=8)
_PLAIN = dict(x_bytes=2, has_norm=False, out_bytes=2)


def _compiler_params(semantics):
    return pltpu.CompilerParams(dimension_semantics=semantics,
                                vmem_limit_bytes=VMEM_LIMIT_BYTES)


def _rms_rows(x, g):
    ms = jnp.mean(x * x, axis=-1, keepdims=True)
    return x * lax.rsqrt(ms + NORM_EPS) * g


def _norm_rows_pass(x_ref, g_ref, o_ref):
    rows = x_ref.shape[0]
    step = min(rows, NORM_ROWS)

    def body(r, carry):
        sl = pl.ds(pl.multiple_of(r * step, step), step)
        o_ref[sl, :] = _rms_rows(x_ref[sl, :], g_ref[...]).astype(o_ref.dtype)
        return carry

    trips = rows // step
    lax.fori_loop(0, trips, body, 0, unroll=math.gcd(trips, NORM_UNROLL))


def _linear_kernel(*refs, has_norm, n_slabs, n_extra, n_out, epilogue):
    x_ref = refs[0]
    pos = 1
    g_ref = None
    if has_norm:
        g_ref = refs[pos]
        pos += 1
    w_ref = refs[pos]
    pos += 1
    extra = refs[pos:pos + n_extra]
    pos += n_extra
    outs = refs[pos:pos + n_out]
    pos += n_out
    j = pl.program_id(1)
    a_ref = refs[pos] if has_norm else None
    tm = x_ref.shape[-2]

    def product():
        if has_norm:
            lhs = a_ref[...]
        elif n_slabs:
            lhs = jnp.concatenate([x_ref[s] for s in range(n_slabs)], axis=1)
        else:
            lhs = x_ref[...]
        return jnp.dot(lhs, w_ref[...], preferred_element_type=F32)

    def rows_of(ref, rows):
        if len(ref.shape) == 3:
            return ref.at[:, rows] if ref.shape[1] == tm else ref
        return ref.at[rows] if ref.shape[0] == tm else ref

    def first_step(fn):
        sub = tm // NORM_SUBTILES if tm % (NORM_SUBTILES * NORM_ROWS) == 0 else tm
        for r in range(tm // sub):
            rows = pl.ds(r * sub, sub)
            _norm_rows_pass(x_ref.at[rows], g_ref, a_ref.at[rows])

            def sub_product(rows=rows):
                return jnp.dot(a_ref[rows, :], w_ref[...], preferred_element_type=F32)

            fn(sub_product, [rows_of(e, rows) for e in extra], [rows_of(o, rows) for o in outs])

    if len(epilogue) == 1 and not has_norm:
        epilogue[0][1](product, extra, outs)
        return
    lo = 0
    for seg, (count, fn) in enumerate(epilogue):
        hi = lo + count
        if has_norm and seg == 0:
            pl.when(j == 0)(functools.partial(first_step, fn))
            lo = 1
        if hi > lo:
            pl.when(jnp.logical_and(j >= lo, j < hi))(functools.partial(fn, product, extra, outs))
        lo = hi


def _linear(x, w, *, name, tm, tn, epilogue, out_widths=None, out_dtypes=None, norm_g=None,
            extras=(), extra_specs=(), out_col_tiles=None, out_specs=None, out_shape=None):
    kdim, n = w.shape
    n_slabs = x.shape[0] if x.ndim == 3 else 0
    m = x.shape[-2]
    assert m % tm == 0 and n % tn == 0, (x.shape, w.shape, tm, tn)
    has_norm = norm_g is not None

    if n_slabs:
        assert n_slabs * x.shape[2] == kdim
        in_specs = [pl.BlockSpec((n_slabs, tm, x.shape[2]), lambda i, j: (0, i, 0))]
    else:
        assert x.shape[1] == kdim
        in_specs = [pl.BlockSpec((tm, kdim), lambda i, j: (i, 0))]
    args = [x]
    if has_norm:
        in_specs.append(pl.BlockSpec((1, kdim), lambda i, j: (0, 0)))
        args.append(norm_g.reshape(1, kdim).astype(F32))
    in_specs.append(pl.BlockSpec((kdim, tn), lambda i, j: (0, j)))
    args.append(w)
    in_specs.extend(extra_specs)
    args.extend(extras)

    if out_specs is None:
        if out_col_tiles is None:
            out_col_tiles = [tn] * len(out_widths)
        out_specs = [pl.BlockSpec((tm, ct), lambda i, j: (i, j)) for ct in out_col_tiles]
        out_shape = [jax.ShapeDtypeStruct((m, wd), dt) for wd, dt in zip(out_widths, out_dtypes)]
    scratch = [pltpu.VMEM((tm, kdim), BF16)] if has_norm else []
    if callable(epilogue):
        epilogue = [(n // tn, epilogue)]
    assert sum(count for count, _ in epilogue) == n // tn

    kernel = functools.partial(_linear_kernel, has_norm=has_norm, n_slabs=n_slabs,
                               n_extra=len(extras), n_out=len(out_specs), epilogue=epilogue)
    return pl.pallas_call(
        kernel,
        grid=(m // tm, n // tn),
        in_specs=in_specs,
        out_specs=out_specs,
        out_shape=out_shape,
        scratch_shapes=scratch,
        compiler_params=_compiler_params(("parallel", "arbitrary")),
        name=name,
    )(*args)


def _ep_plain(product, extra, outs):
    outs[0][...] = product().astype(outs[0].dtype)


def _ep_slabs(product, extra, outs):
    o_ref = outs[0]
    acc = product()
    width = o_ref.shape[2]
    for s in range(o_ref.shape[0]):
        o_ref[s] = acc[:, s * width:(s + 1) * width].astype(o_ref.dtype)


def _ep_relu2(product, extra, outs):
    r = jnp.maximum(product(), 0.0)
    outs[0][...] = (r * r).astype(outs[0].dtype)


def _ep_residual(product, extra, outs):
    outs[0][...] = extra[0][...] + product()


def _silu(x):
    return x * (1.0 / (1.0 + jnp.exp(-x)))


def _ep_ret_rope(product, extra, outs):
    cos_ref, sin_ref = extra
    o_ref = outs[0]
    half = RET_DK // 2
    acc = product()
    c = cos_ref[...]
    s = sin_ref[...]
    for hh in range(o_ref.shape[0]):
        lo = hh * RET_DK
        x1 = acc[:, lo:lo + half]
        x2 = acc[:, lo + half:lo + RET_DK]
        o_ref[hh, :, :half] = (x1 * c - x2 * s).astype(o_ref.dtype)
        o_ref[hh, :, half:] = (x1 * s + x2 * c).astype(o_ref.dtype)


def _ep_ret_gate(product, extra, outs):
    _ep_slabs(lambda: _silu(product()), extra, outs)


def _ep_mla_a(product, extra, outs):
    qn_ref, kvn_ref, cc_ref, ss_ref = extra
    cq_ref, ckv_ref, kr_ref = outs
    acc = product()
    a0 = MLA_Q_LORA
    a1 = a0 + MLA_KV_LORA
    a2 = a1 + LANES
    cq_ref[...] = _rms_rows(acc[:, :a0], qn_ref[...]).astype(cq_ref.dtype)
    ckv_ref[...] = _rms_rows(acc[:, a0:a1], kvn_ref[...]).astype(ckv_ref.dtype)
    kr = acc[:, a1:a2] * cc_ref[...] + acc[:, a2:] * ss_ref[...]
    kr_ref[...] = kr.astype(kr_ref.dtype)


def _ep_mla_q(product, extra, outs, *, scale):
    cc_ref, ss_ref = extra
    o_ref = outs[0]
    acc = product()
    cc = cc_ref[...]
    ss = ss_ref[...]
    for hh in range(o_ref.shape[1] // MXU_DIM):
        lo = hh * MXU_DIM
        mid = lo + LANES
        hi = lo + MXU_DIM
        o_ref[:, lo:mid] = (acc[:, lo:mid] * scale).astype(o_ref.dtype)
        x = acc[:, mid:hi]
        xr = pltpu.roll(x, LANES // 2, 1)
        o_ref[:, mid:hi] = ((x * cc + xr * ss) * scale).astype(o_ref.dtype)


def _row_spec(tm, width, period):
    return pl.BlockSpec((tm, width), lambda i, j: (i % period, 0))


def _slab_out(n_slabs, rows, width, tile_slabs, tm):
    spec = pl.BlockSpec((tile_slabs, tm, width), lambda i, j: (j, i, 0))
    return [spec], [jax.ShapeDtypeStruct((n_slabs, rows, width), BF16)]


def _linear_t_kernel(*refs, n_extra, epilogue):
    wt_ref, x_ref = refs[0], refs[1]
    extra = refs[2:2 + n_extra]
    o_ref = refs[2 + n_extra]

    def product():
        return lax.dot_general(wt_ref[...], x_ref[...], _NT, preferred_element_type=F32)

    epilogue(product, extra, o_ref)


def _linear_t(x, wt, *, name, tm, epilogue, extras=(), extra_specs=()):
    m, kdim = x.shape
    n = wt.shape[0]
    assert m % tm == 0, (m, tm)
    tn = _col_tile(kdim, n, tm=tm, **_PLAIN)
    in_specs = [pl.BlockSpec((tn, kdim), lambda i, j: (j, 0)),
                pl.BlockSpec((tm, kdim), lambda i, j: (i, 0))]
    in_specs.extend(extra_specs)
    kernel = functools.partial(_linear_t_kernel, n_extra=len(extras), epilogue=epilogue)
    return pl.pallas_call(
        kernel,
        grid=(m // tm, n // tn),
        in_specs=in_specs,
        out_specs=pl.BlockSpec((tn, tm), lambda i, j: (j, i)),
        out_shape=jax.ShapeDtypeStruct((n, m), BF16),
        compiler_params=_compiler_params(("parallel", "arbitrary")),
        name=name,
    )(wt, x, *extras)


def _ep_t_plain(product, extra, o_ref):
    o_ref[...] = product().astype(o_ref.dtype)


def _ep_t_mla_q(product, extra, o_ref, *, scale):
    cc_ref, ss_ref = extra
    acc = product()
    cc = cc_ref[...]
    ss = ss_ref[...]
    half = LANES // 2
    for hh in range(o_ref.shape[0] // MXU_DIM):
        lo = hh * MXU_DIM
        mid = lo + LANES
        hi = lo + MXU_DIM
        o_ref[lo:mid, :] = (acc[lo:mid, :] * scale).astype(o_ref.dtype)
        x = acc[mid:hi, :]
        xr = jnp.concatenate([x[half:], x[:half]], axis=0)
        o_ref[mid:hi, :] = ((x * cc + xr * ss) * scale).astype(o_ref.dtype)


def _mlp_down_kernel(*refs, nk, final_norm):
    if final_norm:
        x_ref, w_ref, res_ref, g_ref, o_ref = refs
    else:
        x_ref, w_ref, res_ref, o_ref = refs
    k = pl.program_id(1)

    def product():
        return jnp.dot(x_ref[...], w_ref[...], preferred_element_type=F32)

    @pl.when(k == 0)
    def _():
        o_ref[...] = res_ref[...] + product()

    @pl.when(k > 0)
    def _():
        o_ref[...] += product()

    if final_norm:
        @pl.when(k == nk - 1)
        def _():
            _norm_rows_pass(o_ref, g_ref, o_ref)


def _mlp_down(hid, w2, res, *, tm, final_g=None):
    m, kdim = hid.shape
    d = w2.shape[1]
    tk = _tile(kdim, DOWN_K_TILE)
    nk = kdim // tk
    final_norm = final_g is not None
    in_specs = [pl.BlockSpec((tm, tk), lambda i, k: (i, k)),
                pl.BlockSpec((tk, d), lambda i, k: (k, 0)),
                pl.BlockSpec((tm, d), lambda i, k: (i, 0))]
    args = [hid, w2, res]
    if final_norm:
        in_specs.append(pl.BlockSpec((1, d), lambda i, k: (0, 0)))
        args.append(final_g.reshape(1, d).astype(F32))
    return pl.pallas_call(
        functools.partial(_mlp_down_kernel, nk=nk, final_norm=final_norm),
        grid=(m // tm, nk),
        in_specs=in_specs,
        out_specs=pl.BlockSpec((tm, d), lambda i, k: (i, 0)),
        out_shape=jax.ShapeDtypeStruct((m, d), F32),
        compiler_params=_compiler_params(("parallel", "arbitrary")),
        name="mlp_down",
    )(*args)


def _ret_kernel(cd_ref, q_ref, k_ref, v_ref, g_ref, qm_ref, km_ref, vm_ref, gm_ref,
                dmask_ref, qdf_ref, kdf_ref, qdb_ref, kdb_ref,
                o_ref, om_ref, rf_ref, rb_ref, opart_ref, *, n_groups, group):
    h = pl.program_id(1)
    t = pl.program_id(2)
    c = RET_CHUNK
    cd_f = cd_ref[h]
    cd_b = cd_ref[RET_HEADS + h]

    def decayed(x, dec_ref):
        return x * dec_ref[0]

    def wide(ref, sl):
        return jnp.concatenate([ref[0, sl, :], ref[1, sl, :]], axis=1)

    def pad_meta(x):
        return jnp.concatenate([jnp.zeros((c - N_META, x.shape[1]), x.dtype), x], axis=0)

    def chunk_rows(cidx):
        if isinstance(cidx, int):
            return slice(cidx * c, (cidx + 1) * c)
        return pl.ds(pl.multiple_of(cidx * c, c), c)

    def bwd_update(kc, vc):
        return lax.dot_general(decayed(kc, kdb_ref), vc, _TN, preferred_element_type=F32)

    def bwd_out(state, qc, cidx):
        ob = jnp.dot(decayed(qc, qdb_ref), state.astype(BF16), preferred_element_type=F32)
        opart_ref[chunk_rows(cidx), :] = ob

    def fwd_indep(qc, kc, vc):
        s = lax.dot_general(qc, kc, _NT, preferred_element_type=F32) * dmask_ref[0]
        sv = jnp.dot(s.astype(BF16), vc, preferred_element_type=F32)
        upd = lax.dot_general(decayed(kc, kdf_ref), vc, _TN, preferred_element_type=F32)
        return sv, upd

    def fwd_finish(state, indep, qc, gc, cidx):
        sv, upd = indep
        o = sv + jnp.dot(decayed(qc, qdf_ref), state.astype(BF16), preferred_element_type=F32)
        o = o + opart_ref[chunk_rows(cidx), :]
        mu = jnp.mean(o, axis=-1, keepdims=True)
        xc = o - mu
        var = jnp.mean(xc * xc, axis=-1, keepdims=True)
        on = xc * lax.rsqrt(var + NORM_EPS)
        return cd_f * state + upd, on.astype(BF16) * gc

    all_meta = slice(0, N_META)

    def rows_of(cc):
        return slice(cc * c, (cc + 1) * c)

    @pl.when(t == 0)
    def _():
        rb_ref[...] = jnp.zeros_like(rb_ref)

    @pl.when(t < n_groups)
    def _():
        grp = n_groups - 1 - t
        state = rb_ref[...]
        order = list(range(group - 1, -1, -1))
        upd_next = bwd_update(k_ref[rows_of(order[0]), :], wide(v_ref, rows_of(order[0])))
        for n, cc in enumerate(order):
            upd = upd_next
            if n + 1 < group:
                nxt = rows_of(order[n + 1])
                upd_next = bwd_update(k_ref[nxt, :], wide(v_ref, nxt))
            bwd_out(state, q_ref[rows_of(cc), :], 1 + grp * group + cc)
            state = cd_b * state + upd
        rb_ref[...] = state

    @pl.when(t == n_groups - 1)
    def _():
        bwd_out(rb_ref[...], pad_meta(qm_ref[...]), 0)

    @pl.when(t == n_groups)
    def _():
        qm = pad_meta(qm_ref[...])
        indep = fwd_indep(qm, pad_meta(km_ref[...]), pad_meta(wide(vm_ref, all_meta)))
        state, res = fwd_finish(jnp.zeros(rf_ref.shape, F32), indep, qm, pad_meta(wide(gm_ref, all_meta)), 0)
        rf_ref[...] = state
        om_ref[0] = res[c - N_META:, :RET_SLAB]
        om_ref[1] = res[c - N_META:, RET_SLAB:]

    @pl.when(t >= n_groups)
    def _():
        grp = t - n_groups
        state = rf_ref[...]

        def indep_of(cc):
            sl = rows_of(cc)
            return fwd_indep(q_ref[sl, :], k_ref[sl, :], wide(v_ref, sl))

        indep_next = indep_of(0)
        for cc in range(group):
            indep = indep_next
            if cc + 1 < group:
                indep_next = indep_of(cc + 1)
            sl = rows_of(cc)
            state, res = fwd_finish(state, indep, q_ref[sl, :], wide(g_ref, sl), 1 + grp * group + cc)
            o_ref[0, sl, :] = res[:, :RET_SLAB]
            o_ref[1, sl, :] = res[:, RET_SLAB:]
        rf_ref[...] = state


def _retention_tables():
    c = RET_CHUNK
    hh = jnp.arange(RET_HEADS, dtype=F32)
    lg_f = jnp.log(1.0 - 2.0 ** (-5.0 - hh))
    lg_b = jnp.log(1.0 - 2.0 ** (-5.5 - hh))
    i = jnp.arange(c, dtype=F32)
    diff = i[:, None] - i[None, :]
    ad = jnp.abs(diff)[None]
    dmask = (jnp.where(diff[None] >= 0, jnp.exp(lg_f[:, None, None] * ad), 0.0)
             + jnp.where(diff[None] < 0, jnp.exp(lg_b[:, None, None] * ad), 0.0))

    def wide(v):
        return jnp.broadcast_to(v[:, :, None], (RET_HEADS, c, RET_DK)).astype(BF16)

    qd_f = wide(jnp.exp(lg_f[:, None] * (i + 1.0)[None]))
    kd_f = wide(jnp.exp(lg_f[:, None] * (c - 1.0 - i)[None]))
    qd_b = wide(jnp.exp(lg_b[:, None] * (c - i)[None]))
    kd_b = wide(jnp.exp(lg_b[:, None] * i[None]))
    cd = jnp.concatenate([jnp.exp(lg_f * c), jnp.exp(lg_b * c)])
    return cd, dmask, qd_f, kd_f, qd_b, kd_b


def _retention(qkvg_r, qkvg_m, batch, seq, tables):
    cd, dmask, qd_f, kd_f, qd_b, kd_b = tables
    c = RET_CHUNK
    n_real = seq // c
    group = _tile(n_real, RET_GROUP)
    n_groups = n_real // group
    rows = group * c
    v_per = RET_DV // RET_SLAB
    k_off = RET_HEADS
    v_off = 2 * RET_HEADS // v_per
    g_off = v_off + RET_HEADS

    def sweep(t):
        return jnp.where(t < n_groups, n_groups - 1 - t, t - n_groups)

    def fwd_only(t):
        return jnp.maximum(t - n_groups, 0)

    in_specs = [
        pl.BlockSpec((None, rows, RET_SLAB), lambda b, h, t, cd: (h, b * n_groups + sweep(t), 0)),
        pl.BlockSpec((None, rows, RET_SLAB), lambda b, h, t, cd: (k_off + h, b * n_groups + sweep(t), 0)),
        pl.BlockSpec((v_per, rows, RET_SLAB), lambda b, h, t, cd: (v_off + h, b * n_groups + sweep(t), 0)),
        pl.BlockSpec((v_per, rows, RET_SLAB), lambda b, h, t, cd: (g_off + h, b * n_groups + fwd_only(t), 0)),
        pl.BlockSpec((None, N_META, RET_SLAB), lambda b, h, t, cd: (h, b, 0)),
        pl.BlockSpec((None, N_META, RET_SLAB), lambda b, h, t, cd: (k_off + h, b, 0)),
        pl.BlockSpec((v_per, N_META, RET_SLAB), lambda b, h, t, cd: (v_off + h, b, 0)),
        pl.BlockSpec((v_per, N_META, RET_SLAB), lambda b, h, t, cd: (g_off + h, b, 0)),
        pl.BlockSpec((1, c, c), lambda b, h, t, cd: (h, 0, 0)),
        pl.BlockSpec((1, c, RET_DK), lambda b, h, t, cd: (h, 0, 0)),
        pl.BlockSpec((1, c, RET_DK), lambda b, h, t, cd: (h, 0, 0)),
        pl.BlockSpec((1, c, RET_DK), lambda b, h, t, cd: (h, 0, 0)),
        pl.BlockSpec((1, c, RET_DK), lambda b, h, t, cd: (h, 0, 0)),
    ]
    out_specs = [
        pl.BlockSpec((v_per, rows, RET_SLAB), lambda b, h, t, cd: (h, b * n_groups + fwd_only(t), 0)),
        pl.BlockSpec((v_per, N_META, RET_SLAB), lambda b, h, t, cd: (h, b, 0)),
    ]
    out_shape = [
        jax.ShapeDtypeStruct((RET_HEADS * v_per, batch * seq, RET_SLAB), BF16),
        jax.ShapeDtypeStruct((RET_HEADS * v_per, batch * N_META, RET_SLAB), BF16),
    ]
    grid_spec = pltpu.PrefetchScalarGridSpec(
        num_scalar_prefetch=1,
        grid=(batch, RET_HEADS, 2 * n_groups),
        in_specs=in_specs,
        out_specs=out_specs,
        scratch_shapes=[
            pltpu.VMEM((RET_DK, RET_DV), F32),
            pltpu.VMEM((RET_DK, RET_DV), F32),
            pltpu.VMEM(((n_real + 1) * c, RET_DV), F32),
        ],
    )
    kernel = functools.partial(_ret_kernel, n_groups=n_groups, group=group)
    return pl.pallas_call(
        kernel,
        grid_spec=grid_spec,
        out_shape=out_shape,
        compiler_params=_compiler_params(("parallel", "parallel", "arbitrary")),
        name="retention",
    )(cd, qkvg_r, qkvg_r, qkvg_r, qkvg_r, qkvg_m, qkvg_m, qkvg_m, qkvg_m,
      dmask, qd_f, kd_f, qd_b, kd_b)


def _attn_kernel(qt_ref, kn_ref, kr_ref, vt_ref, knm_ref, krm_ref, vm_ref, o_ref, *, tk):
    def values(vt, n):
        return jnp.concatenate([vt, jnp.ones((ONES_ROWS, n), BF16)], axis=0)

    qt = qt_ref[...]

    def scores(kn, kr):
        return jnp.dot(jnp.concatenate([kn, kr], axis=1), qt, preferred_element_type=F32)

    def block(c):
        sl = slice(c * tk, (c + 1) * tk)
        return scores(kn_ref[sl, :], kr_ref[sl, :])

    nk = kn_ref.shape[0] // tk
    s = scores(knm_ref[...], krm_ref[...])
    s_next = block(0)
    m = jnp.max(s, axis=0, keepdims=True)
    p = jnp.exp2(s - m).astype(BF16)
    vm_t = vm_ref[...].astype(F32).T.astype(BF16)
    acc = jnp.dot(values(vm_t, N_META), p, preferred_element_type=F32)
    for c in range(nk):
        s = s_next
        if c + 1 < nk:
            s_next = block(c + 1)
        m_new = jnp.maximum(m, jnp.max(s, axis=0, keepdims=True))
        alpha = jnp.exp2(m - m_new)
        p = jnp.exp2(s - m_new).astype(BF16)
        sl = slice(c * tk, (c + 1) * tk)
        acc = alpha * acc + jnp.dot(values(vt_ref[:, sl], tk), p, preferred_element_type=F32)
        m = m_new
    o_ref[...] = (acc[:MLA_V] / acc[MLA_V:MLA_V + 1]).T.astype(o_ref.dtype)


def _attn_meta_kernel(q_ref, kn_ref, kr_ref, vt_ref, knm_ref, krm_ref, vm_ref, o_ref, *, tk, heads):
    def cols(hh, width):
        return slice(hh * width, (hh + 1) * width)

    qs = [q_ref[:, cols(hh, MXU_DIM)] for hh in range(heads)]

    def scores(q, kn, kr):
        return lax.dot_general(q, jnp.concatenate([kn, kr], axis=1), _NT, preferred_element_type=F32)

    state = []
    for hh in range(heads):
        s = scores(qs[hh], knm_ref[:, cols(hh, MLA_NOPE)], krm_ref[...])
        m0 = jnp.max(s, axis=1, keepdims=True)
        p = jnp.exp2(s - m0)
        state += [m0, jnp.sum(p, axis=1, keepdims=True),
                  jnp.dot(p.astype(BF16), vm_ref[:, cols(hh, MLA_V)], preferred_element_type=F32)]

    def body(c, state):
        sl = pl.ds(pl.multiple_of(c * tk, tk), tk)
        kr = kr_ref[sl, :]
        new = []
        for hh in range(heads):
            m_prev, l_prev, acc = state[3 * hh:3 * hh + 3]
            s = scores(qs[hh], kn_ref[hh, sl, :], kr)
            m_new = jnp.maximum(m_prev, jnp.max(s, axis=1, keepdims=True))
            alpha = jnp.exp2(m_prev - m_new)
            p = jnp.exp2(s - m_new)
            pv = lax.dot_general(p.astype(BF16), vt_ref[cols(hh, MLA_V), sl], _NT,
                                 preferred_element_type=F32)
            new += [m_new, alpha * l_prev + jnp.sum(p, axis=1, keepdims=True), alpha * acc + pv]
        return tuple(new)

    state = lax.fori_loop(0, kn_ref.shape[1] // tk, body, tuple(state))
    for hh in range(heads):
        o_ref[hh] = (state[3 * hh + 2] / state[3 * hh + 1]).astype(o_ref.dtype)


def _attention(qt, q_m, kn, kr, vt, kv_m, kr_m, batch, seq):
    tq = _tile(seq, ATT_TQ)
    nq = seq // tq
    o_r = pl.pallas_call(
        functools.partial(_attn_kernel, tk=_tile(seq, ATT_TK)),
        grid=(batch, MLA_HEADS, nq),
        in_specs=[
            pl.BlockSpec((MXU_DIM, tq), lambda b, h, i: (h, b * nq + i)),
            pl.BlockSpec((None, seq, MLA_NOPE), lambda b, h, i: (h, b, 0)),
            pl.BlockSpec((seq, LANES), lambda b, h, i: (b, 0)),
            pl.BlockSpec((MLA_V, seq), lambda b, h, i: (h, b)),
            pl.BlockSpec((N_META, MLA_NOPE), lambda b, h, i: (b, h)),
            pl.BlockSpec((N_META, LANES), lambda b, h, i: (b, 0)),
            pl.BlockSpec((N_META, MLA_V), lambda b, h, i: (b, MLA_HEADS + h)),
        ],
        out_specs=pl.BlockSpec((None, tq, MLA_V), lambda b, h, i: (h, b * nq + i, 0)),
        out_shape=jax.ShapeDtypeStruct((MLA_HEADS, batch * seq, MLA_V), BF16),
        compiler_params=_compiler_params(("parallel", "parallel", "arbitrary")),
        name="attention",
    )(qt, kn, kr, vt, kv_m, kr_m, kv_m)
    if q_m is None:
        return o_r, None
    hg = ATT_META_HEADS
    n_hg = MLA_HEADS // hg
    o_m = pl.pallas_call(
        functools.partial(_attn_meta_kernel, tk=_tile(seq, ATT_META_TK), heads=hg),
        grid=(batch, n_hg),
        in_specs=[
            pl.BlockSpec((N_META, hg * MXU_DIM), lambda b, g: (b, g)),
            pl.BlockSpec((hg, seq, MLA_NOPE), lambda b, g: (g, b, 0)),
            pl.BlockSpec((seq, LANES), lambda b, g: (b, 0)),
            pl.BlockSpec((hg * MLA_V, seq), lambda b, g: (g, b)),
            pl.BlockSpec((N_META, hg * MLA_NOPE), lambda b, g: (b, g)),
            pl.BlockSpec((N_META, LANES), lambda b, g: (b, 0)),
            pl.BlockSpec((N_META, hg * MLA_V), lambda b, g: (b, n_hg + g)),
        ],
        out_specs=pl.BlockSpec((hg, N_META, MLA_V), lambda b, g: (g, b, 0)),
        out_shape=jax.ShapeDtypeStruct((MLA_HEADS, batch * N_META, MLA_V), BF16),
        compiler_params=_compiler_params(("parallel", "arbitrary")),
        name="attention_meta",
    )(q_m, kn, kr, vt, kv_m, kr_m, kv_m)
    return o_r, o_m


def _rope_tables(n, dim):
    inv = 1.0 / (ROPE_BASE ** (jnp.arange(0, dim, 2, dtype=F32) / dim))
    ang = jnp.arange(n, dtype=F32)[:, None] * inv[None, :]
    return jnp.cos(ang), jnp.sin(ang)


def _swap_halves(w):
    half = w.shape[-1] // 2
    return jnp.concatenate([w[..., half:], w[..., :half]], axis=-1)


def _prep_weights(p):
    d = D_MODEL
    out = {}
    out["ret_qkvg"] = [
        jnp.concatenate([p["ret_wq"][j], p["ret_wk"][j] * (RET_DK ** -0.5),
                         p["ret_wv"][j], p["ret_wg"][j]], axis=1).astype(BF16)
        for j in range(p["ret_wq"].shape[0])]
    out["ret_wo"] = [p["ret_wo"][j].astype(BF16) for j in range(p["ret_wo"].shape[0])]
    keys = ("mla_a", "mla_qb", "mla_qb_t", "mla_kvb", "mla_kn", "mla_v_t", "mla_wo")
    for key in keys:
        out[key] = []
    zeros = jnp.zeros((d, LANES - MLA_ROPE), F32)
    for j in range(p["mla_wq_a"].shape[0]):
        wkv_a = p["mla_wkv_a"][j]
        wr = wkv_a[:, MLA_KV_LORA:]
        out["mla_a"].append(jnp.concatenate(
            [p["mla_wq_a"][j], wkv_a[:, :MLA_KV_LORA], wr, zeros, _swap_halves(wr), zeros],
            axis=1).astype(BF16))
        wq_b = p["mla_wq_b"][j].reshape(MLA_Q_LORA, MLA_HEADS, MLA_NOPE + MLA_ROPE)
        rope = wq_b[..., MLA_NOPE:]
        qb = jnp.concatenate([wq_b[..., :MLA_NOPE], rope, _swap_halves(rope)], axis=-1)
        qb = qb.reshape(MLA_Q_LORA, MLA_HEADS * MXU_DIM).astype(BF16)
        out["mla_qb"].append(qb)
        out["mla_qb_t"].append(qb.T)
        wkv_b = p["mla_wkv_b"][j].reshape(MLA_KV_LORA, MLA_HEADS, MLA_NOPE + MLA_V)
        kn = wkv_b[..., :MLA_NOPE].reshape(MLA_KV_LORA, MLA_HEADS * MLA_NOPE).astype(BF16)
        vv = wkv_b[..., MLA_NOPE:].reshape(MLA_KV_LORA, MLA_HEADS * MLA_V).astype(BF16)
        out["mla_kvb"].append(jnp.concatenate([kn, vv], axis=1))
        out["mla_kn"].append(kn)
        out["mla_v_t"].append(vv.T)
        out["mla_wo"].append(p["mla_wo"][j].astype(BF16))
    out["mlp_w1"] = [p["mlp_w1"][i].astype(BF16) for i in range(p["mlp_w1"].shape[0])]
    out["mlp_w2"] = [p["mlp_w2"][i].astype(BF16) for i in range(p["mlp_w2"].shape[0])]
    return out


def _position_tables(batch, seq):
    n = seq + N_META
    cos_r, sin_r = _rope_tables(n, RET_DK)
    cos_m, sin_m = _rope_tables(n, MLA_ROPE)
    pad = jnp.zeros((n, LANES - MLA_ROPE), F32)
    cc = jnp.concatenate([cos_m, cos_m, pad], axis=1)
    ss = jnp.concatenate([-sin_m, sin_m, pad], axis=1)

    def split(tbl):
        return tbl[N_META:], jnp.tile(tbl[:N_META], (batch, 1))

    return {"ret_cos": split(cos_r), "ret_sin": split(sin_r),
            "mla_cc": split(cc), "mla_ss": split(ss),
            "mla_cc_t": cc[N_META:].T, "mla_ss_t": ss[N_META:].T}


def _trunk(x, p, w, ret_tables):
    batch, seq, d = x.shape
    pos = _position_tables(batch, seq)
    h_r = x.reshape(batch * seq, d)
    h_m = jnp.broadcast_to(p["meta_tokens"].astype(F32)[None], (batch, N_META, d)).reshape(batch * N_META, d)
    depth = p["norm1_g"].shape[0]
    q_scale = math.log2(math.e) * (MLA_NOPE + MLA_ROPE) ** -0.5

    def both(fn, last=False):
        return fn(0), (None if last else fn(1))

    tms = (_tile(seq, ROW_TILE), batch * N_META)
    pers = (seq // tms[0], 1)
    n_rows = (batch * seq, batch * N_META)

    for i in range(depth):
        last = i == depth - 1
        j = i // 2
        hs = (h_r, h_m)
        if i % 2 == 0:
            w_in = w["ret_qkvg"][j]
            tn = _col_tile(d, 2 * RET_HEADS * RET_DK, **_NORM_IN)
            ep = [(2 * RET_HEADS * RET_DK // tn, _ep_ret_rope), (RET_HEADS * RET_DV // tn, _ep_slabs),
                  (RET_HEADS * RET_DV // tn, _ep_ret_gate)]

            def qkvg(which):
                tm, per = tms[which], pers[which]
                o_specs, o_shape = _slab_out(w_in.shape[1] // RET_SLAB, n_rows[which], RET_SLAB,
                                             tn // RET_SLAB, tm)
                return _linear(hs[which], w_in, name="ret_qkvg", tm=tm, tn=tn, epilogue=ep,
                               norm_g=p["norm1_g"][i], out_specs=o_specs, out_shape=o_shape,
                               extras=(pos["ret_cos"][which], pos["ret_sin"][which]),
                               extra_specs=(_row_spec(tm, LANES, per), _row_spec(tm, LANES, per)))[0]

            qkvg_r, qkvg_m = both(qkvg)
            mix_r, mix_m = _retention(qkvg_r, qkvg_m, batch, seq, ret_tables)
            w_out = w["ret_wo"][j]
        else:
            w_a = w["mla_a"][j]

            def stage_a(which):
                tm, per = tms[which], pers[which]
                one = lambda width: pl.BlockSpec((1, width), lambda i_, j_: (0, 0))
                return _linear(hs[which], w_a, name="mla_a", tm=tm, tn=w_a.shape[1], epilogue=_ep_mla_a,
                               out_widths=[MLA_Q_LORA, MLA_KV_LORA, LANES], out_dtypes=[BF16] * 3,
                               out_col_tiles=[MLA_Q_LORA, MLA_KV_LORA, LANES],
                               norm_g=p["norm1_g"][i],
                               extras=(p["mla_q_norm"][j].reshape(1, -1), p["mla_kv_norm"][j].reshape(1, -1),
                                       pos["mla_cc"][which], pos["mla_ss"][which]),
                               extra_specs=(one(MLA_Q_LORA), one(MLA_KV_LORA),
                                            _row_spec(tm, LANES, per), _row_spec(tm, LANES, per)))

            (cq_r, ckv_r, kr_r), (cq_m, ckv_m, kr_m) = both(stage_a)
            tm_r, per_r = tms[0], pers[0]
            t_spec = pl.BlockSpec((LANES, tm_r), lambda i_, j_: (0, i_ % per_r))
            qt_r = _linear_t(cq_r, w["mla_qb_t"][j], name="mla_qb_t", tm=tm_r,
                             epilogue=functools.partial(_ep_t_mla_q, scale=q_scale),
                             extras=(pos["mla_cc_t"], pos["mla_ss_t"]), extra_specs=(t_spec, t_spec))
            tn_kn = _col_tile(MLA_KV_LORA, MLA_HEADS * MLA_NOPE, **_PLAIN)
            kn_specs, kn_shape = _slab_out(MLA_HEADS, n_rows[0], MLA_NOPE, tn_kn // MLA_NOPE, tm_r)
            kn_r = _linear(ckv_r, w["mla_kn"][j], name="mla_kn", tm=tm_r, tn=tn_kn, epilogue=_ep_slabs,
                           out_specs=kn_specs, out_shape=kn_shape)[0]
            vt_r = _linear_t(ckv_r, w["mla_v_t"][j], name="mla_v_t", tm=tm_r, epilogue=_ep_t_plain)
            tm_m = tms[1]
            kv_m = _linear(ckv_m, w["mla_kvb"][j], name="mla_kvb", tm=tm_m, tn=MAX_COL_TILE,
                           epilogue=_ep_plain, out_widths=[MLA_HEADS * (MLA_NOPE + MLA_V)],
                           out_dtypes=[BF16])[0]
            q_m = None
            if not last:
                m_spec = _row_spec(tm_m, LANES, 1)
                q_m = _linear(cq_m, w["mla_qb"][j], name="mla_qb", tm=tm_m, tn=MAX_COL_TILE,
                              epilogue=functools.partial(_ep_mla_q, scale=q_scale),
                              out_widths=[MLA_HEADS * MXU_DIM], out_dtypes=[BF16],
                              extras=(pos["mla_cc"][1], pos["mla_ss"][1]), extra_specs=(m_spec, m_spec))[0]
            mix_r, mix_m = _attention(qt_r, q_m, kn_r, kr_r, vt_r, kv_m, kr_m, batch, seq)
            w_out = w["mla_wo"][j]

        mixes = (mix_r, mix_m)
        tn_out = _col_tile(w_out.shape[0], d, **_RES_OUT)

        def proj_out(which):
            tm = tms[which]
            return _linear(mixes[which], w_out, name="mix_out", tm=tm, tn=tn_out, epilogue=_ep_residual,
                           out_widths=[d], out_dtypes=[F32], extras=(hs[which],),
                           extra_specs=(pl.BlockSpec((tm, tn_out), lambda i_, j_: (i_, j_)),))[0]

        h_r, h_m = both(proj_out, last)
        hs = (h_r, h_m)
        tn_up = _col_tile(d, D_FF, **_NORM_IN)

        def mlp(which):
            tm = tms[which]
            hid = _linear(hs[which], w["mlp_w1"][i], name="mlp_up", tm=tm, tn=tn_up, epilogue=_ep_relu2,
                          out_widths=[D_FF], out_dtypes=[BF16], norm_g=p["norm2_g"][i])[0]
            return _mlp_down(hid, w["mlp_w2"][i], hs[which], tm=tm,
                             final_g=p["final_norm"] if last else None)

        h_r, h_m = both(mlp, last)

    return h_r.reshape(batch, seq, d)


def kernel(x_prompt, x_sample, meta_tokens, norm1_g, norm2_g, mlp_w1, mlp_w2, ret_wq, ret_wk, ret_wv, ret_wg, ret_wo, mla_wq_a, mla_q_norm, mla_wq_b, mla_wkv_a, mla_kv_norm, mla_wkv_b, mla_wo, final_norm):
    p = dict(meta_tokens=meta_tokens, norm1_g=norm1_g, norm2_g=norm2_g, mlp_w1=mlp_w1, mlp_w2=mlp_w2,
             ret_wq=ret_wq, ret_wk=ret_wk, ret_wv=ret_wv, ret_wg=ret_wg, ret_wo=ret_wo,
             mla_wq_a=mla_wq_a, mla_q_norm=mla_q_norm, mla_wq_b=mla_wq_b, mla_wkv_a=mla_wkv_a,
             mla_kv_norm=mla_kv_norm, mla_wkv_b=mla_wkv_b, mla_wo=mla_wo, final_norm=final_norm)
    w = _prep_weights(p)
    ret_tables = _retention_tables()
    return (_trunk(x_prompt, p, w, ret_tables), _trunk(x_sample, p, w, ret_tables))
```

```python
import functools
import math

import jax
import jax.numpy as jnp
from jax import lax
from jax.experimental import pallas as pl
from jax.experimental.pallas import tpu as pltpu

F32 = jnp.float32
BF16 = jnp.bfloat16

D_MODEL = 2048
N_META = 16
D_FF = 4 * D_MODEL
NORM_EPS = 1e-6
ROPE_BASE = 10000.0
RET_HEADS = 8
RET_DK = 256
RET_DV = 512
MLA_HEADS = 16
MLA_Q_LORA = 512
MLA_KV_LORA = 512
MLA_NOPE = 128
MLA_ROPE = 64
MLA_V = 128

LANES = 128
MXU_DIM = 256
VMEM_LIMIT_BYTES = 56 * 1024 * 1024
VMEM_TILE_BUDGET = 48 * 1024 * 1024

ROW_TILE = 1024
STEP_MACS = 2 ** 32
MAX_COL_TILE = 2048
DOWN_K_TILE = 1024
OUT_RESIDENT_K = 2048
NORM_ROWS = 64
NORM_UNROLL = 4
NORM_SUBTILES = 4
RET_CHUNK = 256
RET_GROUP = 8
RET_SLAB = RET_DK
ATT_TQ = 1024
ATT_TK = 512
ATT_META_TK = 1024
ATT_META_HEADS = 4
ATT_META_UNROLL = 2
ONES_ROWS = 16

_NT = (((1,), (1,)), ((), ()))
_TN = (((0,), (0,)), ((), ()))


def _tile(n, pref):
    t = min(n, pref)
    assert n % t == 0, (n, pref)
    return t


def _col_tile(kdim, n, *, x_bytes, has_norm, out_bytes, tm=ROW_TILE):
    def footprint(tn):
        return (2 * tm * kdim * x_bytes + (2 * tm * kdim if has_norm else 0)
                + 2 * kdim * tn * 2 + 2 * tm * tn * out_bytes)

    tn = min(n, MAX_COL_TILE)
    while tn > MXU_DIM and (tm * kdim * tn > STEP_MACS or footprint(tn) > VMEM_TILE_BUDGET):
        tn //= 2
    assert n % tn == 0, (n, tn)
    return tn


_NORM_IN = dict(x_bytes=4, has_norm=True, out_bytes=2)
_RES_OUT = dict(x_bytes=2, has_norm=False, out_bytes=8)
_PLAIN = dict(x_bytes=2, has_norm=False, out_bytes=2)


def _compiler_params(semantics):
    return pltpu.CompilerParams(dimension_semantics=semantics,
                                vmem_limit_bytes=VMEM_LIMIT_BYTES)


def _rms_rows(x, g):
    ms = jnp.mean(x * x, axis=-1, keepdims=True)
    return x * lax.rsqrt(ms + NORM_EPS) * g


def _norm_rows_pass(x_ref, g_ref, o_ref):
    rows = x_ref.shape[0]
    step = min(rows, NORM_ROWS)

    def body(r, carry):
        sl = pl.ds(pl.multiple_of(r * step, step), step)
        o_ref[sl, :] = _rms_rows(x_ref[sl, :], g_ref[...]).astype(o_ref.dtype)
        return carry

    trips = rows // step
    lax.fori_loop(0, trips, body, 0, unroll=math.gcd(trips, NORM_UNROLL))


def _linear_kernel(*refs, has_norm, n_slabs, n_extra, n_out, epilogue):
    x_ref = refs[0]
    pos = 1
    g_ref = None
    if has_norm:
        g_ref = refs[pos]
        pos += 1
    w_ref = refs[pos]
    pos += 1
    extra = refs[pos:pos + n_extra]
    pos += n_extra
    outs = refs[pos:pos + n_out]
    pos += n_out
    j = pl.program_id(1)
    a_ref = refs[pos] if has_norm else None
    tm = x_ref.shape[-2]

    def product():
        if has_norm:
            lhs = a_ref[...]
        elif n_slabs:
            lhs = jnp.concatenate([x_ref[s] for s in range(n_slabs)], axis=1)
        else:
            lhs = x_ref[...]
        return jnp.dot(lhs, w_ref[...], preferred_element_type=F32)

    def rows_of(ref, rows):
        if len(ref.shape) == 3:
            return ref.at[:, rows] if ref.shape[1] == tm else ref
        return ref.at[rows] if ref.shape[0] == tm else ref

    def first_step(fn):
        sub = tm // NORM_SUBTILES if tm % (NORM_SUBTILES * NORM_ROWS) == 0 else tm
        for r in range(tm // sub):
            rows = pl.ds(r * sub, sub)
            _norm_rows_pass(x_ref.at[rows], g_ref, a_ref.at[rows])

            def sub_product(rows=rows):
                return jnp.dot(a_ref[rows, :], w_ref[...], preferred_element_type=F32)

            fn(sub_product, [rows_of(e, rows) for e in extra], [rows_of(o, rows) for o in outs])

    if len(epilogue) == 1 and not has_norm:
        epilogue[0][1](product, extra, outs)
        return
    lo = 0
    for seg, (count, fn) in enumerate(epilogue):
        hi = lo + count
        if has_norm and seg == 0:
            pl.when(j == 0)(functools.partial(first_step, fn))
            lo = 1
        if hi > lo:
            pl.when(jnp.logical_and(j >= lo, j < hi))(functools.partial(fn, product, extra, outs))
        lo = hi


def _linear(x, w, *, name, tm, tn, epilogue, out_widths=None, out_dtypes=None, norm_g=None,
            extras=(), extra_specs=(), out_col_tiles=None, out_specs=None, out_shape=None):
    kdim, n = w.shape
    n_slabs = x.shape[0] if x.ndim == 3 else 0
    m = x.shape[-2]
    assert m % tm == 0 and n % tn == 0, (x.shape, w.shape, tm, tn)
    has_norm = norm_g is not None

    if n_slabs:
        assert n_slabs * x.shape[2] == kdim
        in_specs = [pl.BlockSpec((n_slabs, tm, x.shape[2]), lambda i, j: (0, i, 0))]
    else:
        assert x.shape[1] == kdim
        in_specs = [pl.BlockSpec((tm, kdim), lambda i, j: (i, 0))]
    args = [x]
    if has_norm:
        in_specs.append(pl.BlockSpec((1, kdim), lambda i, j: (0, 0)))
        args.append(norm_g.reshape(1, kdim).astype(F32))
    in_specs.append(pl.BlockSpec((kdim, tn), lambda i, j: (0, j)))
    args.append(w)
    in_specs.extend(extra_specs)
    args.extend(extras)

    if out_specs is None:
        if out_col_tiles is None:
            out_col_tiles = [tn] * len(out_widths)
        out_specs = [pl.BlockSpec((tm, ct), lambda i, j: (i, j)) for ct in out_col_tiles]
        out_shape = [jax.ShapeDtypeStruct((m, wd), dt) for wd, dt in zip(out_widths, out_dtypes)]
    scratch = [pltpu.VMEM((tm, kdim), BF16)] if has_norm else []
    if callable(epilogue):
        epilogue = [(n // tn, epilogue)]
    assert sum(count for count, _ in epilogue) == n // tn

    kernel = functools.partial(_linear_kernel, has_norm=has_norm, n_slabs=n_slabs,
                               n_extra=len(extras), n_out=len(out_specs), epilogue=epilogue)
    return pl.pallas_call(
        kernel,
        grid=(m // tm, n // tn),
        in_specs=in_specs,
        out_specs=out_specs,
        out_shape=out_shape,
        scratch_shapes=scratch,
        compiler_params=_compiler_params(("parallel", "arbitrary")),
        name=name,
    )(*args)


def _ep_plain(product, extra, outs):
    outs[0][...] = product().astype(outs[0].dtype)


def _ep_slabs(product, extra, outs):
    o_ref = outs[0]
    acc = product()
    width = o_ref.shape[2]
    for s in range(o_ref.shape[0]):
        o_ref[s] = acc[:, s * width:(s + 1) * width].astype(o_ref.dtype)


def _ep_relu2(product, extra, outs):
    r = jnp.maximum(product(), 0.0)
    outs[0][...] = (r * r).astype(outs[0].dtype)


def _ep_residual(product, extra, outs):
    outs[0][...] = extra[0][...] + product()


def _silu(x):
    return x * (1.0 / (1.0 + jnp.exp(-x)))


def _ep_ret_rope(product, extra, outs):
    cos_ref, sin_ref = extra
    o_ref = outs[0]
    half = RET_DK // 2
    acc = product()
    c = cos_ref[...]
    s = sin_ref[...]
    for hh in range(o_ref.shape[0]):
        lo = hh * RET_DK
        x1 = acc[:, lo:lo + half]
        x2 = acc[:, lo + half:lo + RET_DK]
        o_ref[hh, :, :half] = (x1 * c - x2 * s).astype(o_ref.dtype)
        o_ref[hh, :, half:] = (x1 * s + x2 * c).astype(o_ref.dtype)


def _ep_ret_gate(product, extra, outs):
    _ep_slabs(lambda: _silu(product()), extra, outs)


def _ep_mla_a(product, extra, outs):
    qn_ref, kvn_ref, cc_ref, ss_ref = extra
    cq_ref, ckv_ref, kr_ref = outs
    acc = product()
    a0 = MLA_Q_LORA
    a1 = a0 + MLA_KV_LORA
    a2 = a1 + LANES
    cq_ref[...] = _rms_rows(acc[:, :a0], qn_ref[...]).astype(cq_ref.dtype)
    ckv_ref[...] = _rms_rows(acc[:, a0:a1], kvn_ref[...]).astype(ckv_ref.dtype)
    kr = acc[:, a1:a2] * cc_ref[...] + acc[:, a2:] * ss_ref[...]
    kr_ref[...] = kr.astype(kr_ref.dtype)


def _ep_mla_q(product, extra, outs, *, scale):
    cc_ref, ss_ref = extra
    o_ref = outs[0]
    acc = product()
    cc = cc_ref[...]
    ss = ss_ref[...]
    for hh in range(o_ref.shape[1] // MXU_DIM):
        lo = hh * MXU_DIM
        mid = lo + LANES
        hi = lo + MXU_DIM
        o_ref[:, lo:mid] = (acc[:, lo:mid] * scale).astype(o_ref.dtype)
        x = acc[:, mid:hi]
        xr = pltpu.roll(x, LANES // 2, 1)
        o_ref[:, mid:hi] = ((x * cc + xr * ss) * scale).astype(o_ref.dtype)


def _row_spec(tm, width, period):
    return pl.BlockSpec((tm, width), lambda i, j: (i % period, 0))


def _slab_out(n_slabs, rows, width, tile_slabs, tm):
    spec = pl.BlockSpec((tile_slabs, tm, width), lambda i, j: (j, i, 0))
    return [spec], [jax.ShapeDtypeStruct((n_slabs, rows, width), BF16)]


def _linear_t_kernel(*refs, n_extra, epilogue):
    wt_ref, x_ref = refs[0], refs[1]
    extra = refs[2:2 + n_extra]
    o_ref = refs[2 + n_extra]

    def product():
        return lax.dot_general(wt_ref[...], x_ref[...], _NT, preferred_element_type=F32)

    epilogue(product, extra, o_ref)


def _linear_t(x, wt, *, name, tm, epilogue, extras=(), extra_specs=()):
    m, kdim = x.shape
    n = wt.shape[0]
    assert m % tm == 0, (m, tm)
    tn = _col_tile(kdim, n, tm=tm, **_PLAIN)
    in_specs = [pl.BlockSpec((tn, kdim), lambda i, j: (j, 0)),
                pl.BlockSpec((tm, kdim), lambda i, j: (i, 0))]
    in_specs.extend(extra_specs)
    kernel = functools.partial(_linear_t_kernel, n_extra=len(extras), epilogue=epilogue)
    return pl.pallas_call(
        kernel,
        grid=(m // tm, n // tn),
        in_specs=in_specs,
        out_specs=pl.BlockSpec((tn, tm), lambda i, j: (j, i)),
        out_shape=jax.ShapeDtypeStruct((n, m), BF16),
        compiler_params=_compiler_params(("parallel", "arbitrary")),
        name=name,
    )(wt, x, *extras)


def _ep_t_plain(product, extra, o_ref):
    o_ref[...] = product().astype(o_ref.dtype)


def _ep_t_mla_q(product, extra, o_ref, *, scale):
    cc_ref, ss_ref = extra
    acc = product()
    cc = cc_ref[...]
    ss = ss_ref[...]
    half = LANES // 2
    for hh in range(o_ref.shape[0] // MXU_DIM):
        lo = hh * MXU_DIM
        mid = lo + LANES
        hi = lo + MXU_DIM
        o_ref[lo:mid, :] = (acc[lo:mid, :] * scale).astype(o_ref.dtype)
        x = acc[mid:hi, :]
        xr = jnp.concatenate([x[half:], x[:half]], axis=0)
        o_ref[mid:hi, :] = ((x * cc + xr * ss) * scale).astype(o_ref.dtype)


def _mlp_down_kernel(*refs, nk, final_norm):
    if final_norm:
        x_ref, w_ref, res_ref, g_ref, o_ref = refs
    else:
        x_ref, w_ref, res_ref, o_ref = refs
    k = pl.program_id(1)

    def product():
        return jnp.dot(x_ref[...], w_ref[...], preferred_element_type=F32)

    @pl.when(k == 0)
    def _():
        o_ref[...] = res_ref[...] + product()

    @pl.when(k > 0)
    def _():
        o_ref[...] += product()

    if final_norm:
        @pl.when(k == nk - 1)
        def _():
            _norm_rows_pass(o_ref, g_ref, o_ref)


def _mlp_down(hid, w2, res, *, tm, final_g=None):
    m, kdim = hid.shape
    d = w2.shape[1]
    tk = _tile(kdim, DOWN_K_TILE)
    nk = kdim // tk
    final_norm = final_g is not None
    in_specs = [pl.BlockSpec((tm, tk), lambda i, k: (i, k)),
                pl.BlockSpec((tk, d), lambda i, k: (k, 0)),
                pl.BlockSpec((tm, d), lambda i, k: (i, 0))]
    args = [hid, w2, res]
    if final_norm:
        in_specs.append(pl.BlockSpec((1, d), lambda i, k: (0, 0)))
        args.append(final_g.reshape(1, d).astype(F32))
    return pl.pallas_call(
        functools.partial(_mlp_down_kernel, nk=nk, final_norm=final_norm),
        grid=(m // tm, nk),
        in_specs=in_specs,
        out_specs=pl.BlockSpec((tm, d), lambda i, k: (i, 0)),
        out_shape=jax.ShapeDtypeStruct((m, d), F32),
        compiler_params=_compiler_params(("parallel", "arbitrary")),
        name="mlp_down",
    )(*args)


def _ret_kernel(cd_ref, q_ref, k_ref, v_ref, g_ref, qm_ref, km_ref, vm_ref, gm_ref,
                dmask_ref, qdf_ref, kdf_ref, qdb_ref, kdb_ref,
                o_ref, om_ref, rf_ref, rb_ref, opart_ref, *, n_groups, group):
    h = pl.program_id(1)
    t = pl.program_id(2)
    c = RET_CHUNK
    cd_f = cd_ref[h]
    cd_b = cd_ref[RET_HEADS + h]

    def decayed(x, dec_ref):
        return x * dec_ref[0]

    def wide(ref, sl):
        return jnp.concatenate([ref[0, sl, :], ref[1, sl, :]], axis=1)

    def pad_meta(x):
        return jnp.concatenate([jnp.zeros((c - N_META, x.shape[1]), x.dtype), x], axis=0)

    def chunk_rows(cidx):
        if isinstance(cidx, int):
            return slice(cidx * c, (cidx + 1) * c)
        return pl.ds(pl.multiple_of(cidx * c, c), c)

    def bwd_update(kc, vc):
        return lax.dot_general(decayed(kc, kdb_ref), vc, _TN, preferred_element_type=F32)

    def bwd_out(state, qc, cidx):
        ob = jnp.dot(decayed(qc, qdb_ref), state.astype(BF16), preferred_element_type=F32)
        opart_ref[chunk_rows(cidx), :] = ob

    def fwd_indep(qc, kc, vc):
        s = lax.dot_general(qc, kc, _NT, preferred_element_type=F32) * dmask_ref[0]
        sv = jnp.dot(s.astype(BF16), vc, preferred_element_type=F32)
        upd = lax.dot_general(decayed(kc, kdf_ref), vc, _TN, preferred_element_type=F32)
        return sv, upd

    def fwd_finish(state, indep, qc, gc, cidx):
        sv, upd = indep
        o = sv + jnp.dot(decayed(qc, qdf_ref), state.astype(BF16), preferred_element_type=F32)
        o = o + opart_ref[chunk_rows(cidx), :]
        mu = jnp.mean(o, axis=-1, keepdims=True)
        xc = o - mu
        var = jnp.mean(xc * xc, axis=-1, keepdims=True)
        on = xc * lax.rsqrt(var + NORM_EPS)
        return cd_f * state + upd, on.astype(BF16) * gc

    all_meta = slice(0, N_META)

    def rows_of(cc):
        return slice(cc * c, (cc + 1) * c)

    @pl.when(t == 0)
    def _():
        rb_ref[...] = jnp.zeros_like(rb_ref)

    @pl.when(t < n_groups)
    def _():
        grp = n_groups - 1 - t
        state = rb_ref[...]
        order = list(range(group - 1, -1, -1))
        upd_next = bwd_update(k_ref[rows_of(order[0]), :], wide(v_ref, rows_of(order[0])))
        for n, cc in enumerate(order):
            upd = upd_next
            if n + 1 < group:
                nxt = rows_of(order[n + 1])
                upd_next = bwd_update(k_ref[nxt, :], wide(v_ref, nxt))
            bwd_out(state, q_ref[rows_of(cc), :], 1 + grp * group + cc)
            state = cd_b * state + upd
        rb_ref[...] = state

    @pl.when(t == n_groups - 1)
    def _():
        bwd_out(rb_ref[...], pad_meta(qm_ref[...]), 0)

    @pl.when(t == n_groups)
    def _():
        qm = pad_meta(qm_ref[...])
        indep = fwd_indep(qm, pad_meta(km_ref[...]), pad_meta(wide(vm_ref, all_meta)))
        state, res = fwd_finish(jnp.zeros(rf_ref.shape, F32), indep, qm, pad_meta(wide(gm_ref, all_meta)), 0)
        rf_ref[...] = state
        om_ref[0] = res[c - N_META:, :RET_SLAB]
        om_ref[1] = res[c - N_META:, RET_SLAB:]

    @pl.when(t >= n_groups)
    def _():
        grp = t - n_groups
        state = rf_ref[...]

        def indep_of(cc):
            sl = rows_of(cc)
            return fwd_indep(q_ref[sl, :], k_ref[sl, :], wide(v_ref, sl))

        indep_next = indep_of(0)
        for cc in range(group):
            indep = indep_next
            if cc + 1 < group:
                indep_next = indep_of(cc + 1)
            sl = rows_of(cc)
            state, res = fwd_finish(state, indep, q_ref[sl, :], wide(g_ref, sl), 1 + grp * group + cc)
            o_ref[0, sl, :] = res[:, :RET_SLAB]
            o_ref[1, sl, :] = res[:, RET_SLAB:]
        rf_ref[...] = state


def _retention_tables():
    c = RET_CHUNK
    hh = jnp.arange(RET_HEADS, dtype=F32)
    lg_f = jnp.log(1.0 - 2.0 ** (-5.0 - hh))
    lg_b = jnp.log(1.0 - 2.0 ** (-5.5 - hh))
    i = jnp.arange(c, dtype=F32)
    diff = i[:, None] - i[None, :]
    ad = jnp.abs(diff)[None]
    dmask = (jnp.where(diff[None] >= 0, jnp.exp(lg_f[:, None, None] * ad), 0.0)
             + jnp.where(diff[None] < 0, jnp.exp(lg_b[:, None, None] * ad), 0.0))

    def wide(v):
        return jnp.broadcast_to(v[:, :, None], (RET_HEADS, c, RET_DK)).astype(BF16)

    qd_f = wide(jnp.exp(lg_f[:, None] * (i + 1.0)[None]))
    kd_f = wide(jnp.exp(lg_f[:, None] * (c - 1.0 - i)[None]))
    qd_b = wide(jnp.exp(lg_b[:, None] * (c - i)[None]))
    kd_b = wide(jnp.exp(lg_b[:, None] * i[None]))
    cd = jnp.concatenate([jnp.exp(lg_f * c), jnp.exp(lg_b * c)])
    return cd, dmask, qd_f, kd_f, qd_b, kd_b


def _retention(qkvg_r, qkvg_m, batch, seq, tables):
    cd, dmask, qd_f, kd_f, qd_b, kd_b = tables
    c = RET_CHUNK
    n_real = seq // c
    group = _tile(n_real, RET_GROUP)
    n_groups = n_real // group
    rows = group * c
    v_per = RET_DV // RET_SLAB
    k_off = RET_HEADS
    v_off = 2 * RET_HEADS // v_per
    g_off = v_off + RET_HEADS

    def sweep(t):
        return jnp.where(t < n_groups, n_groups - 1 - t, t - n_groups)

    def fwd_only(t):
        return jnp.maximum(t - n_groups, 0)

    in_specs = [
        pl.BlockSpec((None, rows, RET_SLAB), lambda b, h, t, cd: (h, b * n_groups + sweep(t), 0)),
        pl.BlockSpec((None, rows, RET_SLAB), lambda b, h, t, cd: (k_off + h, b * n_groups + sweep(t), 0)),
        pl.BlockSpec((v_per, rows, RET_SLAB), lambda b, h, t, cd: (v_off + h, b * n_groups + sweep(t), 0)),
        pl.BlockSpec((v_per, rows, RET_SLAB), lambda b, h, t, cd: (g_off + h, b * n_groups + fwd_only(t), 0)),
        pl.BlockSpec((None, N_META, RET_SLAB), lambda b, h, t, cd: (h, b, 0)),
        pl.BlockSpec((None, N_META, RET_SLAB), lambda b, h, t, cd: (k_off + h, b, 0)),
        pl.BlockSpec((v_per, N_META, RET_SLAB), lambda b, h, t, cd: (v_off + h, b, 0)),
        pl.BlockSpec((v_per, N_META, RET_SLAB), lambda b, h, t, cd: (g_off + h, b, 0)),
        pl.BlockSpec((1, c, c), lambda b, h, t, cd: (h, 0, 0)),
        pl.BlockSpec((1, c, RET_DK), lambda b, h, t, cd: (h, 0, 0)),
        pl.BlockSpec((1, c, RET_DK), lambda b, h, t, cd: (h, 0, 0)),
        pl.BlockSpec((1, c, RET_DK), lambda b, h, t, cd: (h, 0, 0)),
        pl.BlockSpec((1, c, RET_DK), lambda b, h, t, cd: (h, 0, 0)),
    ]
    out_specs = [
        pl.BlockSpec((v_per, rows, RET_SLAB), lambda b, h, t, cd: (h, b * n_groups + fwd_only(t), 0)),
        pl.BlockSpec((v_per, N_META, RET_SLAB), lambda b, h, t, cd: (h, b, 0)),
    ]
    out_shape = [
        jax.ShapeDtypeStruct((RET_HEADS * v_per, batch * seq, RET_SLAB), BF16),
        jax.ShapeDtypeStruct((RET_HEADS * v_per, batch * N_META, RET_SLAB), BF16),
    ]
    grid_spec = pltpu.PrefetchScalarGridSpec(
        num_scalar_prefetch=1,
        grid=(batch, RET_HEADS, 2 * n_groups),
        in_specs=in_specs,
        out_specs=out_specs,
        scratch_shapes=[
            pltpu.VMEM((RET_DK, RET_DV), F32),
            pltpu.VMEM((RET_DK, RET_DV), F32),
            pltpu.VMEM(((n_real + 1) * c, RET_DV), F32),
        ],
    )
    kernel = functools.partial(_ret_kernel, n_groups=n_groups, group=group)
    return pl.pallas_call(
        kernel,
        grid_spec=grid_spec,
        out_shape=out_shape,
        compiler_params=_compiler_params(("parallel", "parallel", "arbitrary")),
        name="retention",
    )(cd, qkvg_r, qkvg_r, qkvg_r, qkvg_r, qkvg_m, qkvg_m, qkvg_m, qkvg_m,
      dmask, qd_f, kd_f, qd_b, kd_b)


def _attn_kernel(qt_ref, kn_ref, kr_ref, vt_ref, knm_ref, krm_ref, vm_ref, o_ref, *, tk):
    def values(vt, n):
        return jnp.concatenate([vt, jnp.ones((ONES_ROWS, n), BF16)], axis=0)

    qt = qt_ref[...]

    def scores(kn, kr):
        return jnp.dot(jnp.concatenate([kn, kr], axis=1), qt, preferred_element_type=F32)

    def block(c):
        sl = slice(c * tk, (c + 1) * tk)
        return scores(kn_ref[sl, :], kr_ref[sl, :])

    nk = kn_ref.shape[0] // tk
    s = scores(knm_ref[...], krm_ref[...])
    s_next = block(0)
    m = jnp.max(s, axis=0, keepdims=True)
    p = jnp.exp2(s - m).astype(BF16)
    vm_t = vm_ref[...].astype(F32).T.astype(BF16)
    acc = jnp.dot(values(vm_t, N_META), p, preferred_element_type=F32)
    for c in range(nk):
        s = s_next
        if c + 1 < nk:
            s_next = block(c + 1)
        m_new = jnp.maximum(m, jnp.max(s, axis=0, keepdims=True))
        alpha = jnp.exp2(m - m_new)
        p = jnp.exp2(s - m_new).astype(BF16)
        sl = slice(c * tk, (c + 1) * tk)
        acc = alpha * acc + jnp.dot(values(vt_ref[:, sl], tk), p, preferred_element_type=F32)
        m = m_new
    o_ref[...] = (acc[:MLA_V] / acc[MLA_V:MLA_V + 1]).T.astype(o_ref.dtype)


def _attn_meta_kernel(q_ref, kn_ref, kr_ref, vt_ref, knm_ref, krm_ref, vm_ref, o_ref, *, tk, heads):
    def cols(hh, width):
        return slice(hh * width, (hh + 1) * width)

    qs = [q_ref[:, cols(hh, MXU_DIM)] for hh in range(heads)]

    def scores(q, kn, kr):
        return lax.dot_general(q, jnp.concatenate([kn, kr], axis=1), _NT, preferred_element_type=F32)

    state = []
    for hh in range(heads):
        s = scores(qs[hh], knm_ref[:, cols(hh, MLA_NOPE)], krm_ref[...])
        m0 = jnp.max(s, axis=1, keepdims=True)
        p = jnp.exp2(s - m0)
        state += [m0, jnp.sum(p, axis=1, keepdims=True),
                  jnp.dot(p.astype(BF16), vm_ref[:, cols(hh, MLA_V)], preferred_element_type=F32)]

    def body(c, state):
        sl = pl.ds(pl.multiple_of(c * tk, tk), tk)
        kr = kr_ref[sl, :]
        new = []
        for hh in range(heads):
            m_prev, l_prev, acc = state[3 * hh:3 * hh + 3]
            s = scores(qs[hh], kn_ref[hh, sl, :], kr)
            m_new = jnp.maximum(m_prev, jnp.max(s, axis=1, keepdims=True))
            alpha = jnp.exp2(m_prev - m_new)
            p = jnp.exp2(s - m_new)
            pv = lax.dot_general(p.astype(BF16), vt_ref[cols(hh, MLA_V), sl], _NT,
                                 preferred_element_type=F32)
            new += [m_new, alpha * l_prev + jnp.sum(p, axis=1, keepdims=True), alpha * acc + pv]
        return tuple(new)

    trips = kn_ref.shape[1] // tk
    state = lax.fori_loop(0, trips, body, tuple(state), unroll=math.gcd(trips, ATT_META_UNROLL))
    for hh in range(heads):
        o_ref[hh] = (state[3 * hh + 2] / state[3 * hh + 1]).astype(o_ref.dtype)


def _attention(qt, q_m, kn, kr, vt, kv_m, kr_m, batch, seq):
    tq = _tile(seq, ATT_TQ)
    nq = seq // tq
    o_r = pl.pallas_call(
        functools.partial(_attn_kernel, tk=_tile(seq, ATT_TK)),
        grid=(batch, MLA_HEADS, nq),
        in_specs=[
            pl.BlockSpec((MXU_DIM, tq), lambda b, h, i: (h, b * nq + i)),
            pl.BlockSpec((None, seq, MLA_NOPE), lambda b, h, i: (h, b, 0)),
            pl.BlockSpec((seq, LANES), lambda b, h, i: (b, 0)),
            pl.BlockSpec((MLA_V, seq), lambda b, h, i: (h, b)),
            pl.BlockSpec((N_META, MLA_NOPE), lambda b, h, i: (b, h)),
            pl.BlockSpec((N_META, LANES), lambda b, h, i: (b, 0)),
            pl.BlockSpec((N_META, MLA_V), lambda b, h, i: (b, MLA_HEADS + h)),
        ],
        out_specs=pl.BlockSpec((None, tq, MLA_V), lambda b, h, i: (h, b * nq + i, 0)),
        out_shape=jax.ShapeDtypeStruct((MLA_HEADS, batch * seq, MLA_V), BF16),
        compiler_params=_compiler_params(("parallel", "parallel", "arbitrary")),
        name="attention",
    )(qt, kn, kr, vt, kv_m, kr_m, kv_m)
    if q_m is None:
        return o_r, None
    hg = ATT_META_HEADS
    n_hg = MLA_HEADS // hg
    o_m = pl.pallas_call(
        functools.partial(_attn_meta_kernel, tk=_tile(seq, ATT_META_TK), heads=hg),
        grid=(batch, n_hg),
        in_specs=[
            pl.BlockSpec((N_META, hg * MXU_DIM), lambda b, g: (b, g)),
            pl.BlockSpec((hg, seq, MLA_NOPE), lambda b, g: (g, b, 0)),
            pl.BlockSpec((seq, LANES), lambda b, g: (b, 0)),
            pl.BlockSpec((hg * MLA_V, seq), lambda b, g: (g, b)),
            pl.BlockSpec((N_META, hg * MLA_NOPE), lambda b, g: (b, g)),
            pl.BlockSpec((N_META, LANES), lambda b, g: (b, 0)),
            pl.BlockSpec((N_META, hg * MLA_V), lambda b, g: (b, n_hg + g)),
        ],
        out_specs=pl.BlockSpec((hg, N_META, MLA_V), lambda b, g: (g, b, 0)),
        out_shape=jax.ShapeDtypeStruct((MLA_HEADS, batch * N_META, MLA_V), BF16),
        compiler_params=_compiler_params(("parallel", "arbitrary")),
        name="attention_meta",
    )(q_m, kn, kr, vt, kv_m, kr_m, kv_m)
    return o_r, o_m


def _rope_tables(n, dim):
    inv = 1.0 / (ROPE_BASE ** (jnp.arange(0, dim, 2, dtype=F32) / dim))
    ang = jnp.arange(n, dtype=F32)[:, None] * inv[None, :]
    return jnp.cos(ang), jnp.sin(ang)


def _swap_halves(w):
    half = w.shape[-1] // 2
    return jnp.concatenate([w[..., half:], w[..., :half]], axis=-1)


def _prep_weights(p):
    d = D_MODEL
    out = {}
    out["ret_qkvg"] = [
        jnp.concatenate([p["ret_wq"][j], p["ret_wk"][j] * (RET_DK ** -0.5),
                         p["ret_wv"][j], p["ret_wg"][j]], axis=1).astype(BF16)
        for j in range(p["ret_wq"].shape[0])]
    out["ret_wo"] = [p["ret_wo"][j].astype(BF16) for j in range(p["ret_wo"].shape[0])]
    keys = ("mla_a", "mla_qb", "mla_qb_t", "mla_kvb", "mla_kn", "mla_v_t", "mla_wo")
    for key in keys:
        out[key] = []
    zeros = jnp.zeros((d, LANES - MLA_ROPE), F32)
    for j in range(p["mla_wq_a"].shape[0]):
        wkv_a = p["mla_wkv_a"][j]
        wr = wkv_a[:, MLA_KV_LORA:]
        out["mla_a"].append(jnp.concatenate(
            [p["mla_wq_a"][j], wkv_a[:, :MLA_KV_LORA], wr, zeros, _swap_halves(wr), zeros],
            axis=1).astype(BF16))
        wq_b = p["mla_wq_b"][j].reshape(MLA_Q_LORA, MLA_HEADS, MLA_NOPE + MLA_ROPE)
        rope = wq_b[..., MLA_NOPE:]
        qb = jnp.concatenate([wq_b[..., :MLA_NOPE], rope, _swap_halves(rope)], axis=-1)
        qb = qb.reshape(MLA_Q_LORA, MLA_HEADS * MXU_DIM).astype(BF16)
        out["mla_qb"].append(qb)
        out["mla_qb_t"].append(qb.T)
        wkv_b = p["mla_wkv_b"][j].reshape(MLA_KV_LORA, MLA_HEADS, MLA_NOPE + MLA_V)
        kn = wkv_b[..., :MLA_NOPE].reshape(MLA_KV_LORA, MLA_HEADS * MLA_NOPE).astype(BF16)
        vv = wkv_b[..., MLA_NOPE:].reshape(MLA_KV_LORA, MLA_HEADS * MLA_V).astype(BF16)
        out["mla_kvb"].append(jnp.concatenate([kn, vv], axis=1))
        out["mla_kn"].append(kn)
        out["mla_v_t"].append(vv.T)
        out["mla_wo"].append(p["mla_wo"][j].astype(BF16))
    out["mlp_w1"] = [p["mlp_w1"][i].astype(BF16) for i in range(p["mlp_w1"].shape[0])]
    out["mlp_w2"] = [p["mlp_w2"][i].astype(BF16) for i in range(p["mlp_w2"].shape[0])]
    return out


def _position_tables(batch, seq):
    n = seq + N_META
    cos_r, sin_r = _rope_tables(n, RET_DK)
    cos_m, sin_m = _rope_tables(n, MLA_ROPE)
    pad = jnp.zeros((n, LANES - MLA_ROPE), F32)
    cc = jnp.concatenate([cos_m, cos_m, pad], axis=1)
    ss = jnp.concatenate([-sin_m, sin_m, pad], axis=1)

    def split(tbl):
        return tbl[N_META:], jnp.tile(tbl[:N_META], (batch, 1))

    return {"ret_cos": split(cos_r), "ret_sin": split(sin_r),
            "mla_cc": split(cc), "mla_ss": split(ss),
            "mla_cc_t": cc[N_META:].T, "mla_ss_t": ss[N_META:].T}


def _trunk(x, p, w, ret_tables):
    batch, seq, d = x.shape
    pos = _position_tables(batch, seq)
    h_r = x.reshape(batch * seq, d)
    h_m = jnp.broadcast_to(p["meta_tokens"].astype(F32)[None], (batch, N_META, d)).reshape(batch * N_META, d)
    depth = p["norm1_g"].shape[0]
    q_scale = math.log2(math.e) * (MLA_NOPE + MLA_ROPE) ** -0.5

    def both(fn, last=False):
        return fn(0), (None if last else fn(1))

    tms = (_tile(seq, ROW_TILE), batch * N_META)
    pers = (seq // tms[0], 1)
    n_rows = (batch * seq, batch * N_META)

    for i in range(depth):
        last = i == depth - 1
        j = i // 2
        hs = (h_r, h_m)
        if i % 2 == 0:
            w_in = w["ret_qkvg"][j]
            tn = _col_tile(d, 2 * RET_HEADS * RET_DK, **_NORM_IN)
            ep = [(2 * RET_HEADS * RET_DK // tn, _ep_ret_rope), (RET_HEADS * RET_DV // tn, _ep_slabs),
                  (RET_HEADS * RET_DV // tn, _ep_ret_gate)]

            def qkvg(which):
                tm, per = tms[which], pers[which]
                o_specs, o_shape = _slab_out(w_in.shape[1] // RET_SLAB, n_rows[which], RET_SLAB,
                                             tn // RET_SLAB, tm)
                return _linear(hs[which], w_in, name="ret_qkvg", tm=tm, tn=tn, epilogue=ep,
                               norm_g=p["norm1_g"][i], out_specs=o_specs, out_shape=o_shape,
                               extras=(pos["ret_cos"][which], pos["ret_sin"][which]),
                               extra_specs=(_row_spec(tm, LANES, per), _row_spec(tm, LANES, per)))[0]

            qkvg_r, qkvg_m = both(qkvg)
            mix_r, mix_m = _retention(qkvg_r, qkvg_m, batch, seq, ret_tables)
            w_out = w["ret_wo"][j]
        else:
            w_a = w["mla_a"][j]

            def stage_a(which):
                tm, per = tms[which], pers[which]
                one = lambda width: pl.BlockSpec((1, width), lambda i_, j_: (0, 0))
                return _linear(hs[which], w_a, name="mla_a", tm=tm, tn=w_a.shape[1], epilogue=_ep_mla_a,
                               out_widths=[MLA_Q_LORA, MLA_KV_LORA, LANES], out_dtypes=[BF16] * 3,
                               out_col_tiles=[MLA_Q_LORA, MLA_KV_LORA, LANES],
                               norm_g=p["norm1_g"][i],
                               extras=(p["mla_q_norm"][j].reshape(1, -1), p["mla_kv_norm"][j].reshape(1, -1),
                                       pos["mla_cc"][which], pos["mla_ss"][which]),
                               extra_specs=(one(MLA_Q_LORA), one(MLA_KV_LORA),
                                            _row_spec(tm, LANES, per), _row_spec(tm, LANES, per)))

            (cq_r, ckv_r, kr_r), (cq_m, ckv_m, kr_m) = both(stage_a)
            tm_r, per_r = tms[0], pers[0]
            t_spec = pl.BlockSpec((LANES, tm_r), lambda i_, j_: (0, i_ % per_r))
            qt_r = _linear_t(cq_r, w["mla_qb_t"][j], name="mla_qb_t", tm=tm_r,
                             epilogue=functools.partial(_ep_t_mla_q, scale=q_scale),
                             extras=(pos["mla_cc_t"], pos["mla_ss_t"]), extra_specs=(t_spec, t_spec))
            tn_kn = _col_tile(MLA_KV_LORA, MLA_HEADS * MLA_NOPE, **_PLAIN)
            kn_specs, kn_shape = _slab_out(MLA_HEADS, n_rows[0], MLA_NOPE, tn_kn // MLA_NOPE, tm_r)
            kn_r = _linear(ckv_r, w["mla_kn"][j], name="mla_kn", tm=tm_r, tn=tn_kn, epilogue=_ep_slabs,
                           out_specs=kn_specs, out_shape=kn_shape)[0]
            vt_r = _linear_t(ckv_r, w["mla_v_t"][j], name="mla_v_t", tm=tm_r, epilogue=_ep_t_plain)
            tm_m = tms[1]
            kv_m = _linear(ckv_m, w["mla_kvb"][j], name="mla_kvb", tm=tm_m, tn=MAX_COL_TILE,
                           epilogue=_ep_plain, out_widths=[MLA_HEADS * (MLA_NOPE + MLA_V)],
                           out_dtypes=[BF16])[0]
            q_m = None
            if not last:
                m_spec = _row_spec(tm_m, LANES, 1)
                q_m = _linear(cq_m, w["mla_qb"][j], name="mla_qb", tm=tm_m, tn=MAX_COL_TILE,
                              epilogue=functools.partial(_ep_mla_q, scale=q_scale),
                              out_widths=[MLA_HEADS * MXU_DIM], out_dtypes=[BF16],
                              extras=(pos["mla_cc"][1], pos["mla_ss"][1]), extra_specs=(m_spec, m_spec))[0]
            mix_r, mix_m = _attention(qt_r, q_m, kn_r, kr_r, vt_r, kv_m, kr_m, batch, seq)
            w_out = w["mla_wo"][j]

        mixes = (mix_r, mix_m)
        tm_out = _tile(seq, ROW_TILE // 2) if w_out.shape[0] <= OUT_RESIDENT_K else tms[0]
        tn_out = _col_tile(w_out.shape[0], d, tm=tm_out, **_RES_OUT)

        def proj_out(which):
            tm = tm_out if which == 0 else tms[1]
            return _linear(mixes[which], w_out, name="mix_out", tm=tm, tn=tn_out, epilogue=_ep_residual,
                           out_widths=[d], out_dtypes=[F32], extras=(hs[which],),
                           extra_specs=(pl.BlockSpec((tm, tn_out), lambda i_, j_: (i_, j_)),))[0]

        h_r, h_m = both(proj_out, last)
        hs = (h_r, h_m)
        tn_up = _col_tile(d, D_FF, **_NORM_IN)

        def mlp(which):
            tm = tms[which]
            hid = _linear(hs[which], w["mlp_w1"][i], name="mlp_up", tm=tm, tn=tn_up, epilogue=_ep_relu2,
                          out_widths=[D_FF], out_dtypes=[BF16], norm_g=p["norm2_g"][i])[0]
            return _mlp_down(hid, w["mlp_w2"][i], hs[which], tm=tm,
                             final_g=p["final_norm"] if last else None)

        h_r, h_m = both(mlp, last)

    return h_r.reshape(batch, seq, d)


def kernel(x_prompt, x_sample, meta_tokens, norm1_g, norm2_g, mlp_w1, mlp_w2, ret_wq, ret_wk, ret_wv, ret_wg, ret_wo, mla_wq_a, mla_q_norm, mla_wq_b, mla_wkv_a, mla_kv_norm, mla_wkv_b, mla_wo, final_norm):
    p = dict(meta_tokens=meta_tokens, norm1_g=norm1_g, norm2_g=norm2_g, mlp_w1=mlp_w1, mlp_w2=mlp_w2,
             ret_wq=ret_wq, ret_wk=ret_wk, ret_wv=ret_wv, ret_wg=ret_wg, ret_wo=ret_wo,
             mla_wq_a=mla_wq_a, mla_q_norm=mla_q_norm, mla_wq_b=mla_wq_b, mla_wkv_a=mla_wkv_a,
             mla_kv_norm=mla_kv_norm, mla_wkv_b=mla_wkv_b, mla_wo=mla_wo, final_norm=final_norm)
    w = _prep_weights(p)
    ret_tables = _retention_tables()
    return (_trunk(x_prompt, p, w, ret_tables), _trunk(x_sample, p, w, ret_tables))
```

```python
import functools
import math

import jax
import jax.numpy as jnp
from jax import lax
from jax.experimental import pallas as pl
from jax.experimental.pallas import tpu as pltpu

F32 = jnp.float32
BF16 = jnp.bfloat16

D_MODEL = 2048
N_META = 16
D_FF = 4 * D_MODEL
NORM_EPS = 1e-6
ROPE_BASE = 10000.0
RET_HEADS = 8
RET_DK = 256
RET_DV = 512
MLA_HEADS = 16
MLA_Q_LORA = 512
MLA_KV_LORA = 512
MLA_NOPE = 128
MLA_ROPE = 64
MLA_V = 128

LANES = 128
MXU_DIM = 256
VMEM_LIMIT_BYTES = 56 * 1024 * 1024
VMEM_TILE_BUDGET = 48 * 1024 * 1024

ROW_TILE = 1024
STEP_MACS = 2 ** 32
MAX_COL_TILE = 2048
DOWN_K_TILE = 1024
OUT_RESIDENT_K = 2048
NORM_ROWS = 64
NORM_UNROLL = 4
NORM_SUBTILES = 4
RET_CHUNK = 256
RET_GROUP = 8
RET_SLAB = RET_DK
ATT_TQ = 1024
ATT_TK = 512
ATT_META_TK = 1024
ATT_META_HEADS = 4
ATT_META_UNROLL = 2
ONES_ROWS = 16

_NT = (((1,), (1,)), ((), ()))
_TN = (((0,), (0,)), ((), ()))


def _tile(n, pref):
    t = min(n, pref)
    assert n % t == 0, (n, pref)
    return t


def _col_tile(kdim, n, *, x_bytes, has_norm, out_bytes, tm=ROW_TILE):
    def footprint(tn):
        return (2 * tm * kdim * x_bytes + (2 * tm * kdim if has_norm else 0)
                + 2 * kdim * tn * 2 + 2 * tm * tn * out_bytes)

    tn = min(n, MAX_COL_TILE)
    while tn > MXU_DIM and (tm * kdim * tn > STEP_MACS or footprint(tn) > VMEM_TILE_BUDGET):
        tn //= 2
    assert n % tn == 0, (n, tn)
    return tn


_NORM_IN = dict(x_bytes=4, has_norm=True, out_bytes=2)
_RES_OUT = dict(x_bytes=2, has_norm=False, out_bytes=8)
_PLAIN = dict(x_bytes=2, has_norm=False, out_bytes=2)


def _compiler_params(semantics):
    return pltpu.CompilerParams(dimension_semantics=semantics,
                                vmem_limit_bytes=VMEM_LIMIT_BYTES)


def _rms_rows(x, g):
    ms = jnp.mean(x * x, axis=-1, keepdims=True)
    return x * lax.rsqrt(ms + NORM_EPS) * g


def _norm_rows_pass(x_ref, g_ref, o_ref):
    rows = x_ref.shape[0]
    step = min(rows, NORM_ROWS)

    def body(r, carry):
        sl = pl.ds(pl.multiple_of(r * step, step), step)
        o_ref[sl, :] = _rms_rows(x_ref[sl, :], g_ref[...]).astype(o_ref.dtype)
        return carry

    trips = rows // step
    lax.fori_loop(0, trips, body, 0, unroll=math.gcd(trips, NORM_UNROLL))


def _linear_kernel(*refs, has_norm, n_slabs, n_extra, n_out, epilogue):
    x_ref = refs[0]
    pos = 1
    g_ref = None
    if has_norm:
        g_ref = refs[pos]
        pos += 1
    w_ref = refs[pos]
    pos += 1
    extra = refs[pos:pos + n_extra]
    pos += n_extra
    outs = refs[pos:pos + n_out]
    pos += n_out
    j = pl.program_id(1)
    a_ref = refs[pos] if has_norm else None
    tm = x_ref.shape[-2]

    def product():
        if has_norm:
            lhs = a_ref[...]
        elif n_slabs:
            lhs = jnp.concatenate([x_ref[s] for s in range(n_slabs)], axis=1)
        else:
            lhs = x_ref[...]
        return jnp.dot(lhs, w_ref[...], preferred_element_type=F32)

    def rows_of(ref, rows):
        if len(ref.shape) == 3:
            return ref.at[:, rows] if ref.shape[1] == tm else ref
        return ref.at[rows] if ref.shape[0] == tm else ref

    def first_step(fn):
        sub = tm // NORM_SUBTILES if tm % (NORM_SUBTILES * NORM_ROWS) == 0 else tm
        for r in range(tm // sub):
            rows = pl.ds(r * sub, sub)
            _norm_rows_pass(x_ref.at[rows], g_ref, a_ref.at[rows])

            def sub_product(rows=rows):
                return jnp.dot(a_ref[rows, :], w_ref[...], preferred_element_type=F32)

            fn(sub_product, [rows_of(e, rows) for e in extra], [rows_of(o, rows) for o in outs])

    if len(epilogue) == 1 and not has_norm:
        epilogue[0][1](product, extra, outs)
        return
    lo = 0
    for seg, (count, fn) in enumerate(epilogue):
        hi = lo + count
        if has_norm and seg == 0:
            pl.when(j == 0)(functools.partial(first_step, fn))
            lo = 1
        if hi > lo:
            pl.when(jnp.logical_and(j >= lo, j < hi))(functools.partial(fn, product, extra, outs))
        lo = hi


def _linear(x, w, *, name, tm, tn, epilogue, out_widths=None, out_dtypes=None, norm_g=None,
            extras=(), extra_specs=(), out_col_tiles=None, out_specs=None, out_shape=None):
    kdim, n = w.shape
    n_slabs = x.shape[0] if x.ndim == 3 else 0
    m = x.shape[-2]
    assert m % tm == 0 and n % tn == 0, (x.shape, w.shape, tm, tn)
    has_norm = norm_g is not None

    if n_slabs:
        assert n_slabs * x.shape[2] == kdim
        in_specs = [pl.BlockSpec((n_slabs, tm, x.shape[2]), lambda i, j: (0, i, 0))]
    else:
        assert x.shape[1] == kdim
        in_specs = [pl.BlockSpec((tm, kdim), lambda i, j: (i, 0))]
    args = [x]
    if has_norm:
        in_specs.append(pl.BlockSpec((1, kdim), lambda i, j: (0, 0)))
        args.append(norm_g.reshape(1, kdim).astype(F32))
    in_specs.append(pl.BlockSpec((kdim, tn), lambda i, j: (0, j)))
    args.append(w)
    in_specs.extend(extra_specs)
    args.extend(extras)

    if out_specs is None:
        if out_col_tiles is None:
            out_col_tiles = [tn] * len(out_widths)
        out_specs = [pl.BlockSpec((tm, ct), lambda i, j: (i, j)) for ct in out_col_tiles]
        out_shape = [jax.ShapeDtypeStruct((m, wd), dt) for wd, dt in zip(out_widths, out_dtypes)]
    scratch = [pltpu.VMEM((tm, kdim), BF16)] if has_norm else []
    if callable(epilogue):
        epilogue = [(n // tn, epilogue)]
    assert sum(count for count, _ in epilogue) == n // tn

    kernel = functools.partial(_linear_kernel, has_norm=has_norm, n_slabs=n_slabs,
                               n_extra=len(extras), n_out=len(out_specs), epilogue=epilogue)
    return pl.pallas_call(
        kernel,
        grid=(m // tm, n // tn),
        in_specs=in_specs,
        out_specs=out_specs,
        out_shape=out_shape,
        scratch_shapes=scratch,
        compiler_params=_compiler_params(("parallel", "arbitrary")),
        name=name,
    )(*args)


def _ep_plain(product, extra, outs):
    outs[0][...] = product().astype(outs[0].dtype)


def _ep_slabs(product, extra, outs):
    o_ref = outs[0]
    acc = product()
    width = o_ref.shape[2]
    for s in range(o_ref.shape[0]):
        o_ref[s] = acc[:, s * width:(s + 1) * width].astype(o_ref.dtype)


def _ep_relu2(product, extra, outs):
    r = jnp.maximum(product(), 0.0)
    outs[0][...] = (r * r).astype(outs[0].dtype)


def _ep_residual(product, extra, outs):
    outs[0][...] = extra[0][...] + product()


def _silu(x):
    return x * (1.0 / (1.0 + jnp.exp(-x)))


def _ep_ret_rope(product, extra, outs):
    cos_ref, sin_ref = extra
    o_ref = outs[0]
    half = RET_DK // 2
    acc = product()
    c = cos_ref[...]
    s = sin_ref[...]
    for hh in range(o_ref.shape[0]):
        lo = hh * RET_DK
        x1 = acc[:, lo:lo + half]
        x2 = acc[:, lo + half:lo + RET_DK]
        o_ref[hh, :, :half] = (x1 * c - x2 * s).astype(o_ref.dtype)
        o_ref[hh, :, half:] = (x1 * s + x2 * c).astype(o_ref.dtype)


def _ep_ret_gate(product, extra, outs):
    _ep_slabs(lambda: _silu(product()), extra, outs)


def _ep_mla_a(product, extra, outs):
    qn_ref, kvn_ref, cc_ref, ss_ref = extra
    cq_ref, ckv_ref, kr_ref = outs
    acc = product()
    a0 = MLA_Q_LORA
    a1 = a0 + MLA_KV_LORA
    a2 = a1 + LANES
    cq_ref[...] = _rms_rows(acc[:, :a0], qn_ref[...]).astype(cq_ref.dtype)
    ckv_ref[...] = _rms_rows(acc[:, a0:a1], kvn_ref[...]).astype(ckv_ref.dtype)
    kr = acc[:, a1:a2] * cc_ref[...] + acc[:, a2:] * ss_ref[...]
    kr_ref[...] = kr.astype(kr_ref.dtype)


def _ep_mla_q(product, extra, outs, *, scale):
    cc_ref, ss_ref = extra
    o_ref = outs[0]
    acc = product()
    cc = cc_ref[...]
    ss = ss_ref[...]
    for hh in range(o_ref.shape[1] // MXU_DIM):
        lo = hh * MXU_DIM
        mid = lo + LANES
        hi = lo + MXU_DIM
        o_ref[:, lo:mid] = (acc[:, lo:mid] * scale).astype(o_ref.dtype)
        x = acc[:, mid:hi]
        xr = pltpu.roll(x, LANES // 2, 1)
        o_ref[:, mid:hi] = ((x * cc + xr * ss) * scale).astype(o_ref.dtype)


def _row_spec(tm, width, period):
    return pl.BlockSpec((tm, width), lambda i, j: (i % period, 0))


def _slab_out(n_slabs, rows, width, tile_slabs, tm):
    spec = pl.BlockSpec((tile_slabs, tm, width), lambda i, j: (j, i, 0))
    return [spec], [jax.ShapeDtypeStruct((n_slabs, rows, width), BF16)]


def _linear_t_kernel(*refs, n_extra, epilogue):
    wt_ref, x_ref = refs[0], refs[1]
    extra = refs[2:2 + n_extra]
    o_ref = refs[2 + n_extra]

    def product():
        return lax.dot_general(wt_ref[...], x_ref[...], _NT, preferred_element_type=F32)

    epilogue(product, extra, o_ref)


def _linear_t(x, wt, *, name, tm, epilogue, extras=(), extra_specs=()):
    m, kdim = x.shape
    n = wt.shape[0]
    assert m % tm == 0, (m, tm)
    tn = _col_tile(kdim, n, tm=tm, **_PLAIN)
    in_specs = [pl.BlockSpec((tn, kdim), lambda i, j: (j, 0)),
                pl.BlockSpec((tm, kdim), lambda i, j: (i, 0))]
    in_specs.extend(extra_specs)
    kernel = functools.partial(_linear_t_kernel, n_extra=len(extras), epilogue=epilogue)
    return pl.pallas_call(
        kernel,
        grid=(m // tm, n // tn),
        in_specs=in_specs,
        out_specs=pl.BlockSpec((tn, tm), lambda i, j: (j, i)),
        out_shape=jax.ShapeDtypeStruct((n, m), BF16),
        compiler_params=_compiler_params(("parallel", "arbitrary")),
        name=name,
    )(wt, x, *extras)


def _ep_t_plain(product, extra, o_ref):
    o_ref[...] = product().astype(o_ref.dtype)


def _ep_t_mla_q(product, extra, o_ref, *, scale):
    cc_ref, ss_ref = extra
    acc = product()
    cc = cc_ref[...]
    ss = ss_ref[...]
    half = LANES // 2
    for hh in range(o_ref.shape[0] // MXU_DIM):
        lo = hh * MXU_DIM
        mid = lo + LANES
        hi = lo + MXU_DIM
        o_ref[lo:mid, :] = (acc[lo:mid, :] * scale).astype(o_ref.dtype)
        x = acc[mid:hi, :]
        xr = jnp.concatenate([x[half:], x[:half]], axis=0)
        o_ref[mid:hi, :] = ((x * cc + xr * ss) * scale).astype(o_ref.dtype)


def _mlp_down_kernel(*refs, nk, final_norm):
    if final_norm:
        x_ref, w_ref, res_ref, g_ref, o_ref = refs
    else:
        x_ref, w_ref, res_ref, o_ref = refs
    k = pl.program_id(1)

    def product():
        return jnp.dot(x_ref[...], w_ref[...], preferred_element_type=F32)

    @pl.when(k == 0)
    def _():
        o_ref[...] = res_ref[...] + product()

    @pl.when(k > 0)
    def _():
        o_ref[...] += product()

    if final_norm:
        @pl.when(k == nk - 1)
        def _():
            _norm_rows_pass(o_ref, g_ref, o_ref)


def _mlp_down(hid, w2, res, *, tm, final_g=None):
    m, kdim = hid.shape
    d = w2.shape[1]
    tk = _tile(kdim, DOWN_K_TILE)
    nk = kdim // tk
    final_norm = final_g is not None
    in_specs = [pl.BlockSpec((tm, tk), lambda i, k: (i, k)),
                pl.BlockSpec((tk, d), lambda i, k: (k, 0)),
                pl.BlockSpec((tm, d), lambda i, k: (i, 0))]
    args = [hid, w2, res]
    if final_norm:
        in_specs.append(pl.BlockSpec((1, d), lambda i, k: (0, 0)))
        args.append(final_g.reshape(1, d).astype(F32))
    return pl.pallas_call(
        functools.partial(_mlp_down_kernel, nk=nk, final_norm=final_norm),
        grid=(m // tm, nk),
        in_specs=in_specs,
        out_specs=pl.BlockSpec((tm, d), lambda i, k: (i, 0)),
        out_shape=jax.ShapeDtypeStruct((m, d), F32),
        compiler_params=_compiler_params(("parallel", "arbitrary")),
        name="mlp_down",
    )(*args)


def _ret_kernel(cd_ref, q_ref, k_ref, v_ref, g_ref, qm_ref, km_ref, vm_ref, gm_ref,
                dmask_ref, qdf_ref, kdf_ref, qdb_ref, kdb_ref,
                o_ref, om_ref, rf_ref, rb_ref, opart_ref, *, n_groups, group):
    h = pl.program_id(1)
    t = pl.program_id(2)
    c = RET_CHUNK
    cd_f = cd_ref[h]
    cd_b = cd_ref[RET_HEADS + h]

    def decayed(x, dec_ref):
        return x * dec_ref[0]

    def wide(ref, sl):
        return jnp.concatenate([ref[0, sl, :], ref[1, sl, :]], axis=1)

    def pad_meta(x):
        return jnp.concatenate([jnp.zeros((c - N_META, x.shape[1]), x.dtype), x], axis=0)

    def chunk_rows(cidx):
        if isinstance(cidx, int):
            return slice(cidx * c, (cidx + 1) * c)
        return pl.ds(pl.multiple_of(cidx * c, c), c)

    def bwd_update(kc, vc):
        return lax.dot_general(decayed(kc, kdb_ref), vc, _TN, preferred_element_type=F32)

    def bwd_out(state, qc, cidx):
        ob = jnp.dot(decayed(qc, qdb_ref), state.astype(BF16), preferred_element_type=F32)
        opart_ref[chunk_rows(cidx), :] = ob

    def fwd_indep(qc, kc, vc):
        s = lax.dot_general(qc, kc, _NT, preferred_element_type=F32) * dmask_ref[0]
        sv = jnp.dot(s.astype(BF16), vc, preferred_element_type=F32)
        upd = lax.dot_general(decayed(kc, kdf_ref), vc, _TN, preferred_element_type=F32)
        return sv, upd

    def fwd_finish(state, indep, qc, gc, cidx):
        sv, upd = indep
        o = sv + jnp.dot(decayed(qc, qdf_ref), state.astype(BF16), preferred_element_type=F32)
        o = o + opart_ref[chunk_rows(cidx), :]
        mu = jnp.mean(o, axis=-1, keepdims=True)
        xc = o - mu
        var = jnp.mean(xc * xc, axis=-1, keepdims=True)
        on = xc * lax.rsqrt(var + NORM_EPS)
        return cd_f * state + upd, on.astype(BF16) * gc

    all_meta = slice(0, N_META)

    def rows_of(cc):
        return slice(cc * c, (cc + 1) * c)

    @pl.when(t == 0)
    def _():
        rb_ref[...] = jnp.zeros_like(rb_ref)

    @pl.when(t < n_groups)
    def _():
        grp = n_groups - 1 - t
        state = rb_ref[...]
        order = list(range(group - 1, -1, -1))
        upd_next = bwd_update(k_ref[rows_of(order[0]), :], wide(v_ref, rows_of(order[0])))
        for n, cc in enumerate(order):
            upd = upd_next
            if n + 1 < group:
                nxt = rows_of(order[n + 1])
                upd_next = bwd_update(k_ref[nxt, :], wide(v_ref, nxt))
            bwd_out(state, q_ref[rows_of(cc), :], 1 + grp * group + cc)
            state = cd_b * state + upd
        rb_ref[...] = state

    @pl.when(t == n_groups - 1)
    def _():
        bwd_out(rb_ref[...], pad_meta(qm_ref[...]), 0)

    @pl.when(t == n_groups)
    def _():
        qm = pad_meta(qm_ref[...])
        indep = fwd_indep(qm, pad_meta(km_ref[...]), pad_meta(wide(vm_ref, all_meta)))
        state, res = fwd_finish(jnp.zeros(rf_ref.shape, F32), indep, qm, pad_meta(wide(gm_ref, all_meta)), 0)
        rf_ref[...] = state
        om_ref[0] = res[c - N_META:, :RET_SLAB]
        om_ref[1] = res[c - N_META:, RET_SLAB:]

    @pl.when(t >= n_groups)
    def _():
        grp = t - n_groups
        state = rf_ref[...]

        def indep_of(cc):
            sl = rows_of(cc)
            return fwd_indep(q_ref[sl, :], k_ref[sl, :], wide(v_ref, sl))

        indep_next = indep_of(0)
        for cc in range(group):
            indep = indep_next
            if cc + 1 < group:
                indep_next = indep_of(cc + 1)
            sl = rows_of(cc)
            state, res = fwd_finish(state, indep, q_ref[sl, :], wide(g_ref, sl), 1 + grp * group + cc)
            o_ref[0, sl, :] = res[:, :RET_SLAB]
            o_ref[1, sl, :] = res[:, RET_SLAB:]
        rf_ref[...] = state


def _retention_tables():
    c = RET_CHUNK
    hh = jnp.arange(RET_HEADS, dtype=F32)
    lg_f = jnp.log(1.0 - 2.0 ** (-5.0 - hh))
    lg_b = jnp.log(1.0 - 2.0 ** (-5.5 - hh))
    i = jnp.arange(c, dtype=F32)
    diff = i[:, None] - i[None, :]
    ad = jnp.abs(diff)[None]
    dmask = (jnp.where(diff[None] >= 0, jnp.exp(lg_f[:, None, None] * ad), 0.0)
             + jnp.where(diff[None] < 0, jnp.exp(lg_b[:, None, None] * ad), 0.0))

    def wide(v):
        return jnp.broadcast_to(v[:, :, None], (RET_HEADS, c, RET_DK)).astype(BF16)

    qd_f = wide(jnp.exp(lg_f[:, None] * (i + 1.0)[None]))
    kd_f = wide(jnp.exp(lg_f[:, None] * (c - 1.0 - i)[None]))
    qd_b = wide(jnp.exp(lg_b[:, None] * (c - i)[None]))
    kd_b = wide(jnp.exp(lg_b[:, None] * i[None]))
    cd = jnp.concatenate([jnp.exp(lg_f * c), jnp.exp(lg_b * c)])
    return cd, dmask, qd_f, kd_f, qd_b, kd_b


def _retention(qkvg_r, qkvg_m, batch, seq, tables):
    cd, dmask, qd_f, kd_f, qd_b, kd_b = tables
    c = RET_CHUNK
    n_real = seq // c
    group = _tile(n_real, RET_GROUP)
    n_groups = n_real // group
    rows = group * c
    v_per = RET_DV // RET_SLAB
    k_off = RET_HEADS
    v_off = 2 * RET_HEADS // v_per
    g_off = v_off + RET_HEADS

    def sweep(t):
        return jnp.where(t < n_groups, n_groups - 1 - t, t - n_groups)

    def fwd_only(t):
        return jnp.maximum(t - n_groups, 0)

    in_specs = [
        pl.BlockSpec((None, rows, RET_SLAB), lambda b, h, t, cd: (h, b * n_groups + sweep(t), 0)),
        pl.BlockSpec((None, rows, RET_SLAB), lambda b, h, t, cd: (k_off + h, b * n_groups + sweep(t), 0)),
        pl.BlockSpec((v_per, rows, RET_SLAB), lambda b, h, t, cd: (v_off + h, b * n_groups + sweep(t), 0)),
        pl.BlockSpec((v_per, rows, RET_SLAB), lambda b, h, t, cd: (g_off + h, b * n_groups + fwd_only(t), 0)),
        pl.BlockSpec((None, N_META, RET_SLAB), lambda b, h, t, cd: (h, b, 0)),
        pl.BlockSpec((None, N_META, RET_SLAB), lambda b, h, t, cd: (k_off + h, b, 0)),
        pl.BlockSpec((v_per, N_META, RET_SLAB), lambda b, h, t, cd: (v_off + h, b, 0)),
        pl.BlockSpec((v_per, N_META, RET_SLAB), lambda b, h, t, cd: (g_off + h, b, 0)),
        pl.BlockSpec((1, c, c), lambda b, h, t, cd: (h, 0, 0)),
        pl.BlockSpec((1, c, RET_DK), lambda b, h, t, cd: (h, 0, 0)),
        pl.BlockSpec((1, c, RET_DK), lambda b, h, t, cd: (h, 0, 0)),
        pl.BlockSpec((1, c, RET_DK), lambda b, h, t, cd: (h, 0, 0)),
        pl.BlockSpec((1, c, RET_DK), lambda b, h, t, cd: (h, 0, 0)),
    ]
    out_specs = [
        pl.BlockSpec((v_per, rows, RET_SLAB), lambda b, h, t, cd: (h, b * n_groups + fwd_only(t), 0)),
        pl.BlockSpec((v_per, N_META, RET_SLAB), lambda b, h, t, cd: (h, b, 0)),
    ]
    out_shape = [
        jax.ShapeDtypeStruct((RET_HEADS * v_per, batch * seq, RET_SLAB), BF16),
        jax.ShapeDtypeStruct((RET_HEADS * v_per, batch * N_META, RET_SLAB), BF16),
    ]
    grid_spec = pltpu.PrefetchScalarGridSpec(
        num_scalar_prefetch=1,
        grid=(batch, RET_HEADS, 2 * n_groups),
        in_specs=in_specs,
        out_specs=out_specs,
        scratch_shapes=[
            pltpu.VMEM((RET_DK, RET_DV), F32),
            pltpu.VMEM((RET_DK, RET_DV), F32),
            pltpu.VMEM(((n_real + 1) * c, RET_DV), F32),
        ],
    )
    kernel = functools.partial(_ret_kernel, n_groups=n_groups, group=group)
    return pl.pallas_call(
        kernel,
        grid_spec=grid_spec,
        out_shape=out_shape,
        compiler_params=_compiler_params(("parallel", "parallel", "arbitrary")),
        name="retention",
    )(cd, qkvg_r, qkvg_r, qkvg_r, qkvg_r, qkvg_m, qkvg_m, qkvg_m, qkvg_m,
      dmask, qd_f, kd_f, qd_b, kd_b)


def _attn_kernel(qt_ref, kn_ref, kr_ref, vt_ref, knm_ref, krm_ref, vm_ref, o_ref, *, tk):
    def values(vt, n):
        return jnp.concatenate([vt, jnp.ones((ONES_ROWS, n), BF16)], axis=0)

    qt = qt_ref[...]
    nk = kn_ref.shape[0] // tk

    def block(c):
        sl = slice(c * tk, (c + 1) * tk)
        kcat = jnp.concatenate([kn_ref[sl, :], kr_ref[sl, :]], axis=1)
        if c == 0:
            meta = jnp.concatenate([knm_ref[...], krm_ref[...]], axis=1)
            kcat = jnp.concatenate([kcat, meta], axis=0)
        return jnp.dot(kcat, qt, preferred_element_type=F32)

    def block_values(c):
        vt = vt_ref[:, slice(c * tk, (c + 1) * tk)]
        if c == 0:
            vm_t = vm_ref[...].astype(F32).T.astype(BF16)
            vt = jnp.concatenate([vt, vm_t], axis=1)
        return values(vt, vt.shape[1])

    s = block(0)
    s_next = block(1) if nk > 1 else None
    m = jnp.max(s, axis=0, keepdims=True)
    p = jnp.exp2(s - m).astype(BF16)
    acc = jnp.dot(block_values(0), p, preferred_element_type=F32)
    for c in range(1, nk):
        s = s_next
        if c + 1 < nk:
            s_next = block(c + 1)
        m_new = jnp.maximum(m, jnp.max(s, axis=0, keepdims=True))
        alpha = jnp.exp2(m - m_new)
        p = jnp.exp2(s - m_new).astype(BF16)
        acc = alpha * acc + jnp.dot(block_values(c), p, preferred_element_type=F32)
        m = m_new
    o_ref[...] = (acc[:MLA_V] / acc[MLA_V:MLA_V + 1]).T.astype(o_ref.dtype)


def _attn_meta_kernel(q_ref, kn_ref, kr_ref, vt_ref, knm_ref, krm_ref, vm_ref, o_ref, *, tk, heads):
    def cols(hh, width):
        return slice(hh * width, (hh + 1) * width)

    qs = [q_ref[:, cols(hh, MXU_DIM)] for hh in range(heads)]

    def scores(q, kn, kr):
        return lax.dot_general(q, jnp.concatenate([kn, kr], axis=1), _NT, preferred_element_type=F32)

    state = []
    for hh in range(heads):
        s = scores(qs[hh], knm_ref[:, cols(hh, MLA_NOPE)], krm_ref[...])
        m0 = jnp.max(s, axis=1, keepdims=True)
        p = jnp.exp2(s - m0)
        state += [m0, jnp.sum(p, axis=1, keepdims=True),
                  jnp.dot(p.astype(BF16), vm_ref[:, cols(hh, MLA_V)], preferred_element_type=F32)]

    def body(c, state):
        sl = pl.ds(pl.multiple_of(c * tk, tk), tk)
        kr = kr_ref[sl, :]
        new = []
        for hh in range(heads):
            m_prev, l_prev, acc = state[3 * hh:3 * hh + 3]
            s = scores(qs[hh], kn_ref[hh, sl, :], kr)
            m_new = jnp.maximum(m_prev, jnp.max(s, axis=1, keepdims=True))
            alpha = jnp.exp2(m_prev - m_new)
            p = jnp.exp2(s - m_new)
            pv = lax.dot_general(p.astype(BF16), vt_ref[cols(hh, MLA_V), sl], _NT,
                                 preferred_element_type=F32)
            new += [m_new, alpha * l_prev + jnp.sum(p, axis=1, keepdims=True), alpha * acc + pv]
        return tuple(new)

    trips = kn_ref.shape[1] // tk
    state = lax.fori_loop(0, trips, body, tuple(state), unroll=math.gcd(trips, ATT_META_UNROLL))
    for hh in range(heads):
        o_ref[hh] = (state[3 * hh + 2] / state[3 * hh + 1]).astype(o_ref.dtype)


def _attention(qt, q_m, kn, kr, vt, kv_m, kr_m, batch, seq):
    tq = _tile(seq, ATT_TQ)
    nq = seq // tq
    o_r = pl.pallas_call(
        functools.partial(_attn_kernel, tk=_tile(seq, ATT_TK)),
        grid=(batch, MLA_HEADS, nq),
        in_specs=[
            pl.BlockSpec((MXU_DIM, tq), lambda b, h, i: (h, b * nq + i)),
            pl.BlockSpec((None, seq, MLA_NOPE), lambda b, h, i: (h, b, 0)),
            pl.BlockSpec((seq, LANES), lambda b, h, i: (b, 0)),
            pl.BlockSpec((MLA_V, seq), lambda b, h, i: (h, b)),
            pl.BlockSpec((N_META, MLA_NOPE), lambda b, h, i: (b, h)),
            pl.BlockSpec((N_META, LANES), lambda b, h, i: (b, 0)),
            pl.BlockSpec((N_META, MLA_V), lambda b, h, i: (b, MLA_HEADS + h)),
        ],
        out_specs=pl.BlockSpec((None, tq, MLA_V), lambda b, h, i: (h, b * nq + i, 0)),
        out_shape=jax.ShapeDtypeStruct((MLA_HEADS, batch * seq, MLA_V), BF16),
        compiler_params=_compiler_params(("parallel", "parallel", "arbitrary")),
        name="attention",
    )(qt, kn, kr, vt, kv_m, kr_m, kv_m)
    if q_m is None:
        return o_r, None
    hg = ATT_META_HEADS
    n_hg = MLA_HEADS // hg
    o_m = pl.pallas_call(
        functools.partial(_attn_meta_kernel, tk=_tile(seq, ATT_META_TK), heads=hg),
        grid=(batch, n_hg),
        in_specs=[
            pl.BlockSpec((N_META, hg * MXU_DIM), lambda b, g: (b, g)),
            pl.BlockSpec((hg, seq, MLA_NOPE), lambda b, g: (g, b, 0)),
            pl.BlockSpec((seq, LANES), lambda b, g: (b, 0)),
            pl.BlockSpec((hg * MLA_V, seq), lambda b, g: (g, b)),
            pl.BlockSpec((N_META, hg * MLA_NOPE), lambda b, g: (b, g)),
            pl.BlockSpec((N_META, LANES), lambda b, g: (b, 0)),
            pl.BlockSpec((N_META, hg * MLA_V), lambda b, g: (b, n_hg + g)),
        ],
        out_specs=pl.BlockSpec((hg, N_META, MLA_V), lambda b, g: (g, b, 0)),
        out_shape=jax.ShapeDtypeStruct((MLA_HEADS, batch * N_META, MLA_V), BF16),
        compiler_params=_compiler_params(("parallel", "arbitrary")),
        name="attention_meta",
    )(q_m, kn, kr, vt, kv_m, kr_m, kv_m)
    return o_r, o_m


def _rope_tables(n, dim):
    inv = 1.0 / (ROPE_BASE ** (jnp.arange(0, dim, 2, dtype=F32) / dim))
    ang = jnp.arange(n, dtype=F32)[:, None] * inv[None, :]
    return jnp.cos(ang), jnp.sin(ang)


def _swap_halves(w):
    half = w.shape[-1] // 2
    return jnp.concatenate([w[..., half:], w[..., :half]], axis=-1)


def _prep_weights(p):
    d = D_MODEL
    out = {}
    out["ret_qkvg"] = [
        jnp.concatenate([p["ret_wq"][j], p["ret_wk"][j] * (RET_DK ** -0.5),
                         p["ret_wv"][j], p["ret_wg"][j]], axis=1).astype(BF16)
        for j in range(p["ret_wq"].shape[0])]
    out["ret_wo"] = [p["ret_wo"][j].astype(BF16) for j in range(p["ret_wo"].shape[0])]
    keys = ("mla_a", "mla_qb", "mla_qb_t", "mla_kvb", "mla_kn", "mla_v_t", "mla_wo")
    for key in keys:
        out[key] = []
    zeros = jnp.zeros((d, LANES - MLA_ROPE), F32)
    for j in range(p["mla_wq_a"].shape[0]):
        wkv_a = p["mla_wkv_a"][j]
        wr = wkv_a[:, MLA_KV_LORA:]
        out["mla_a"].append(jnp.concatenate(
            [p["mla_wq_a"][j], wkv_a[:, :MLA_KV_LORA], wr, zeros, _swap_halves(wr), zeros],
            axis=1).astype(BF16))
        wq_b = p["mla_wq_b"][j].reshape(MLA_Q_LORA, MLA_HEADS, MLA_NOPE + MLA_ROPE)
        rope = wq_b[..., MLA_NOPE:]
        qb = jnp.concatenate([wq_b[..., :MLA_NOPE], rope, _swap_halves(rope)], axis=-1)
        qb = qb.reshape(MLA_Q_LORA, MLA_HEADS * MXU_DIM).astype(BF16)
        out["mla_qb"].append(qb)
        out["mla_qb_t"].append(qb.T)
        wkv_b = p["mla_wkv_b"][j].reshape(MLA_KV_LORA, MLA_HEADS, MLA_NOPE + MLA_V)
        kn = wkv_b[..., :MLA_NOPE].reshape(MLA_KV_LORA, MLA_HEADS * MLA_NOPE).astype(BF16)
        vv = wkv_b[..., MLA_NOPE:].reshape(MLA_KV_LORA, MLA_HEADS * MLA_V).astype(BF16)
        out["mla_kvb"].append(jnp.concatenate([kn, vv], axis=1))
        out["mla_kn"].append(kn)
        out["mla_v_t"].append(vv.T)
        out["mla_wo"].append(p["mla_wo"][j].astype(BF16))
    out["mlp_w1"] = [p["mlp_w1"][i].astype(BF16) for i in range(p["mlp_w1"].shape[0])]
    out["mlp_w2"] = [p["mlp_w2"][i].astype(BF16) for i in range(p["mlp_w2"].shape[0])]
    return out


def _position_tables(batch, seq):
    n = seq + N_META
    cos_r, sin_r = _rope_tables(n, RET_DK)
    cos_m, sin_m = _rope_tables(n, MLA_ROPE)
    pad = jnp.zeros((n, LANES - MLA_ROPE), F32)
    cc = jnp.concatenate([cos_m, cos_m, pad], axis=1)
    ss = jnp.concatenate([-sin_m, sin_m, pad], axis=1)

    def split(tbl):
        return tbl[N_META:], jnp.tile(tbl[:N_META], (batch, 1))

    return {"ret_cos": split(cos_r), "ret_sin": split(sin_r),
            "mla_cc": split(cc), "mla_ss": split(ss),
            "mla_cc_t": cc[N_META:].T, "mla_ss_t": ss[N_META:].T}


def _trunk(x, p, w, ret_tables):
    batch, seq, d = x.shape
    pos = _position_tables(batch, seq)
    h_r = x.reshape(batch * seq, d)
    h_m = jnp.broadcast_to(p["meta_tokens"].astype(F32)[None], (batch, N_META, d)).reshape(batch * N_META, d)
    depth = p["norm1_g"].shape[0]
    q_scale = math.log2(math.e) * (MLA_NOPE + MLA_ROPE) ** -0.5

    def both(fn, last=False):
        return fn(0), (None if last else fn(1))

    tms = (_tile(seq, ROW_TILE), batch * N_META)
    pers = (seq // tms[0], 1)
    n_rows = (batch * seq, batch * N_META)

    for i in range(depth):
        last = i == depth - 1
        j = i // 2
        hs = (h_r, h_m)
        if i % 2 == 0:
            w_in = w["ret_qkvg"][j]
            tn = _col_tile(d, 2 * RET_HEADS * RET_DK, **_NORM_IN)
            ep = [(2 * RET_HEADS * RET_DK // tn, _ep_ret_rope), (RET_HEADS * RET_DV // tn, _ep_slabs),
                  (RET_HEADS * RET_DV // tn, _ep_ret_gate)]

            def qkvg(which):
                tm, per = tms[which], pers[which]
                o_specs, o_shape = _slab_out(w_in.shape[1] // RET_SLAB, n_rows[which], RET_SLAB,
                                             tn // RET_SLAB, tm)
                return _linear(hs[which], w_in, name="ret_qkvg", tm=tm, tn=tn, epilogue=ep,
                               norm_g=p["norm1_g"][i], out_specs=o_specs, out_shape=o_shape,
                               extras=(pos["ret_cos"][which], pos["ret_sin"][which]),
                               extra_specs=(_row_spec(tm, LANES, per), _row_spec(tm, LANES, per)))[0]

            qkvg_r, qkvg_m = both(qkvg)
            mix_r, mix_m = _retention(qkvg_r, qkvg_m, batch, seq, ret_tables)
            w_out = w["ret_wo"][j]
        else:
            w_a = w["mla_a"][j]

            def stage_a(which):
                tm, per = tms[which], pers[which]
                one = lambda width: pl.BlockSpec((1, width), lambda i_, j_: (0, 0))
                return _linear(hs[which], w_a, name="mla_a", tm=tm, tn=w_a.shape[1], epilogue=_ep_mla_a,
                               out_widths=[MLA_Q_LORA, MLA_KV_LORA, LANES], out_dtypes=[BF16] * 3,
                               out_col_tiles=[MLA_Q_LORA, MLA_KV_LORA, LANES],
                               norm_g=p["norm1_g"][i],
                               extras=(p["mla_q_norm"][j].reshape(1, -1), p["mla_kv_norm"][j].reshape(1, -1),
                                       pos["mla_cc"][which], pos["mla_ss"][which]),
                               extra_specs=(one(MLA_Q_LORA), one(MLA_KV_LORA),
                                            _row_spec(tm, LANES, per), _row_spec(tm, LANES, per)))

            (cq_r, ckv_r, kr_r), (cq_m, ckv_m, kr_m) = both(stage_a)
            tm_r, per_r = tms[0], pers[0]
            t_spec = pl.BlockSpec((LANES, tm_r), lambda i_, j_: (0, i_ % per_r))
            qt_r = _linear_t(cq_r, w["mla_qb_t"][j], name="mla_qb_t", tm=tm_r,
                             epilogue=functools.partial(_ep_t_mla_q, scale=q_scale),
                             extras=(pos["mla_cc_t"], pos["mla_ss_t"]), extra_specs=(t_spec, t_spec))
            tn_kn = _col_tile(MLA_KV_LORA, MLA_HEADS * MLA_NOPE, **_PLAIN)
            kn_specs, kn_shape = _slab_out(MLA_HEADS, n_rows[0], MLA_NOPE, tn_kn // MLA_NOPE, tm_r)
            kn_r = _linear(ckv_r, w["mla_kn"][j], name="mla_kn", tm=tm_r, tn=tn_kn, epilogue=_ep_slabs,
                           out_specs=kn_specs, out_shape=kn_shape)[0]
            vt_r = _linear_t(ckv_r, w["mla_v_t"][j], name="mla_v_t", tm=tm_r, epilogue=_ep_t_plain)
            tm_m = tms[1]
            kv_m = _linear(ckv_m, w["mla_kvb"][j], name="mla_kvb", tm=tm_m, tn=MAX_COL_TILE,
                           epilogue=_ep_plain, out_widths=[MLA_HEADS * (MLA_NOPE + MLA_V)],
                           out_dtypes=[BF16])[0]
            q_m = None
            if not last:
                m_spec = _row_spec(tm_m, LANES, 1)
                q_m = _linear(cq_m, w["mla_qb"][j], name="mla_qb", tm=tm_m, tn=MAX_COL_TILE,
                              epilogue=functools.partial(_ep_mla_q, scale=q_scale),
                              out_widths=[MLA_HEADS * MXU_DIM], out_dtypes=[BF16],
                              extras=(pos["mla_cc"][1], pos["mla_ss"][1]), extra_specs=(m_spec, m_spec))[0]
            mix_r, mix_m = _attention(qt_r, q_m, kn_r, kr_r, vt_r, kv_m, kr_m, batch, seq)
            w_out = w["mla_wo"][j]

        mixes = (mix_r, mix_m)
        tm_out = _tile(seq, ROW_TILE // 2) if w_out.shape[0] <= OUT_RESIDENT_K else tms[0]
        tn_out = _col_tile(w_out.shape[0], d, tm=tm_out, **_RES_OUT)

        def proj_out(which):
            tm = tm_out if which == 0 else tms[1]
            return _linear(mixes[which], w_out, name="mix_out", tm=tm, tn=tn_out, epilogue=_ep_residual,
                           out_widths=[d], out_dtypes=[F32], extras=(hs[which],),
                           extra_specs=(pl.BlockSpec((tm, tn_out), lambda i_, j_: (i_, j_)),))[0]

        h_r, h_m = both(proj_out, last)
        hs = (h_r, h_m)
        tn_up = _col_tile(d, D_FF, **_NORM_IN)

        def mlp(which):
            tm = tms[which]
            hid = _linear(hs[which], w["mlp_w1"][i], name="mlp_up", tm=tm, tn=tn_up, epilogue=_ep_relu2,
                          out_widths=[D_FF], out_dtypes=[BF16], norm_g=p["norm2_g"][i])[0]
            return _mlp_down(hid, w["mlp_w2"][i], hs[which], tm=tm,
                             final_g=p["final_norm"] if last else None)

        h_r, h_m = both(mlp, last)

    return h_r.reshape(batch, seq, d)


def kernel(x_prompt, x_sample, meta_tokens, norm1_g, norm2_g, mlp_w1, mlp_w2, ret_wq, ret_wk, ret_wv, ret_wg, ret_wo, mla_wq_a, mla_q_norm, mla_wq_b, mla_wkv_a, mla_kv_norm, mla_wkv_b, mla_wo, final_norm):
    p = dict(meta_tokens=meta_tokens, norm1_g=norm1_g, norm2_g=norm2_g, mlp_w1=mlp_w1, mlp_w2=mlp_w2,
             ret_wq=ret_wq, ret_wk=ret_wk, ret_wv=ret_wv, ret_wg=ret_wg, ret_wo=ret_wo,
             mla_wq_a=mla_wq_a, mla_q_norm=mla_q_norm, mla_wq_b=mla_wq_b, mla_wkv_a=mla_wkv_a,
             mla_kv_norm=mla_kv_norm, mla_wkv_b=mla_wkv_b, mla_wo=mla_wo, final_norm=final_norm)
    w = _prep_weights(p)
    ret_tables = _retention_tables()
    return (_trunk(x_prompt, p, w, ret_tables), _trunk(x_sample, p, w, ret_tables))
```

```python
import functools
import math

import jax
import jax.numpy as jnp
from jax import lax
from jax.experimental import pallas as pl
from jax.experimental.pallas import tpu as pltpu

F32 = jnp.float32
BF16 = jnp.bfloat16

D_MODEL = 2048
N_META = 16
D_FF = 4 * D_MODEL
NORM_EPS = 1e-6
ROPE_BASE = 10000.0
RET_HEADS = 8
RET_DK = 256
RET_DV = 512
MLA_HEADS = 16
MLA_Q_LORA = 512
MLA_KV_LORA = 512
MLA_NOPE = 128
MLA_ROPE = 64
MLA_V = 128

LANES = 128
MXU_DIM = 256
VMEM_LIMIT_BYTES = 56 * 1024 * 1024
VMEM_TILE_BUDGET = 48 * 1024 * 1024

ROW_TILE = 1024
STEP_MACS = 2 ** 32
MAX_COL_TILE = 2048
DOWN_K_TILE = 1024
OUT_RESIDENT_K = 2048
NORM_ROWS = 64
NORM_UNROLL = 4
NORM_SUBTILES = 4
RET_CHUNK = 256
RET_GROUP = 8
RET_SLAB = RET_DK
ATT_TQ = 1024
ATT_TK = 512
ATT_META_TK = 2048
ATT_META_HEADS = 4
ATT_META_UNROLL = 2
ONES_ROWS = 16

_NT = (((1,), (1,)), ((), ()))
_TN = (((0,), (0,)), ((), ()))


def _tile(n, pref):
    t = min(n, pref)
    assert n % t == 0, (n, pref)
    return t


def _col_tile(kdim, n, *, x_bytes, has_norm, out_bytes, tm=ROW_TILE):
    def footprint(tn):
        return (2 * tm * kdim * x_bytes + (2 * tm * kdim if has_norm else 0)
                + 2 * kdim * tn * 2 + 2 * tm * tn * out_bytes)

    tn = min(n, MAX_COL_TILE)
    while tn > MXU_DIM and (tm * kdim * tn > STEP_MACS or footprint(tn) > VMEM_TILE_BUDGET):
        tn //= 2
    assert n % tn == 0, (n, tn)
    return tn


_NORM_IN = dict(x_bytes=4, has_norm=True, out_bytes=2)
_RES_OUT = dict(x_bytes=2, has_norm=False, out_bytes=8)
_PLAIN = dict(x_bytes=2, has_norm=False, out_bytes=2)


def _compiler_params(semantics):
    return pltpu.CompilerParams(dimension_semantics=semantics,
                                vmem_limit_bytes=VMEM_LIMIT_BYTES)


def _rms_rows(x, g):
    ms = jnp.mean(x * x, axis=-1, keepdims=True)
    return x * lax.rsqrt(ms + NORM_EPS) * g


def _norm_rows_pass(x_ref, g_ref, o_ref):
    rows = x_ref.shape[0]
    step = min(rows, NORM_ROWS)

    def body(r, carry):
        sl = pl.ds(pl.multiple_of(r * step, step), step)
        o_ref[sl, :] = _rms_rows(x_ref[sl, :], g_ref[...]).astype(o_ref.dtype)
        return carry

    trips = rows // step
    lax.fori_loop(0, trips, body, 0, unroll=math.gcd(trips, NORM_UNROLL))


def _linear_kernel(*refs, has_norm, n_slabs, n_extra, n_out, epilogue):
    x_ref = refs[0]
    pos = 1
    g_ref = None
    if has_norm:
        g_ref = refs[pos]
        pos += 1
    w_ref = refs[pos]
    pos += 1
    extra = refs[pos:pos + n_extra]
    pos += n_extra
    outs = refs[pos:pos + n_out]
    pos += n_out
    j = pl.program_id(1)
    a_ref = refs[pos] if has_norm else None
    tm = x_ref.shape[-2]

    def product():
        if has_norm:
            lhs = a_ref[...]
        elif n_slabs:
            lhs = jnp.concatenate([x_ref[s] for s in range(n_slabs)], axis=1)
        else:
            lhs = x_ref[...]
        return jnp.dot(lhs, w_ref[...], preferred_element_type=F32)

    def rows_of(ref, rows):
        if len(ref.shape) == 3:
            return ref.at[:, rows] if ref.shape[1] == tm else ref
        return ref.at[rows] if ref.shape[0] == tm else ref

    def first_step(fn):
        sub = tm // NORM_SUBTILES if tm % (NORM_SUBTILES * NORM_ROWS) == 0 else tm
        for r in range(tm // sub):
            rows = pl.ds(r * sub, sub)
            _norm_rows_pass(x_ref.at[rows], g_ref, a_ref.at[rows])

            def sub_product(rows=rows):
                return jnp.dot(a_ref[rows, :], w_ref[...], preferred_element_type=F32)

            fn(sub_product, [rows_of(e, rows) for e in extra], [rows_of(o, rows) for o in outs])

    if len(epilogue) == 1 and not has_norm:
        epilogue[0][1](product, extra, outs)
        return
    lo = 0
    for seg, (count, fn) in enumerate(epilogue):
        hi = lo + count
        if has_norm and seg == 0:
            pl.when(j == 0)(functools.partial(first_step, fn))
            lo = 1
        if hi > lo:
            pl.when(jnp.logical_and(j >= lo, j < hi))(functools.partial(fn, product, extra, outs))
        lo = hi


def _linear(x, w, *, name, tm, tn, epilogue, out_widths=None, out_dtypes=None, norm_g=None,
            extras=(), extra_specs=(), out_col_tiles=None, out_specs=None, out_shape=None):
    kdim, n = w.shape
    n_slabs = x.shape[0] if x.ndim == 3 else 0
    m = x.shape[-2]
    assert m % tm == 0 and n % tn == 0, (x.shape, w.shape, tm, tn)
    has_norm = norm_g is not None

    if n_slabs:
        assert n_slabs * x.shape[2] == kdim
        in_specs = [pl.BlockSpec((n_slabs, tm, x.shape[2]), lambda i, j: (0, i, 0))]
    else:
        assert x.shape[1] == kdim
        in_specs = [pl.BlockSpec((tm, kdim), lambda i, j: (i, 0))]
    args = [x]
    if has_norm:
        in_specs.append(pl.BlockSpec((1, kdim), lambda i, j: (0, 0)))
        args.append(norm_g.reshape(1, kdim).astype(F32))
    in_specs.append(pl.BlockSpec((kdim, tn), lambda i, j: (0, j)))
    args.append(w)
    in_specs.extend(extra_specs)
    args.extend(extras)

    if out_specs is None:
        if out_col_tiles is None:
            out_col_tiles = [tn] * len(out_widths)
        out_specs = [pl.BlockSpec((tm, ct), lambda i, j: (i, j)) for ct in out_col_tiles]
        out_shape = [jax.ShapeDtypeStruct((m, wd), dt) for wd, dt in zip(out_widths, out_dtypes)]
    scratch = [pltpu.VMEM((tm, kdim), BF16)] if has_norm else []
    if callable(epilogue):
        epilogue = [(n // tn, epilogue)]
    assert sum(count for count, _ in epilogue) == n // tn

    kernel = functools.partial(_linear_kernel, has_norm=has_norm, n_slabs=n_slabs,
                               n_extra=len(extras), n_out=len(out_specs), epilogue=epilogue)
    return pl.pallas_call(
        kernel,
        grid=(m // tm, n // tn),
        in_specs=in_specs,
        out_specs=out_specs,
        out_shape=out_shape,
        scratch_shapes=scratch,
        compiler_params=_compiler_params(("parallel", "arbitrary")),
        name=name,
    )(*args)


def _ep_plain(product, extra, outs):
    outs[0][...] = product().astype(outs[0].dtype)


def _ep_slabs(product, extra, outs):
    o_ref = outs[0]
    acc = product()
    width = o_ref.shape[2]
    for s in range(o_ref.shape[0]):
        o_ref[s] = acc[:, s * width:(s + 1) * width].astype(o_ref.dtype)


def _ep_relu2(product, extra, outs):
    r = jnp.maximum(product(), 0.0)
    outs[0][...] = (r * r).astype(outs[0].dtype)


def _ep_residual(product, extra, outs):
    outs[0][...] = extra[0][...] + product()


def _silu(x):
    return x * (1.0 / (1.0 + jnp.exp(-x)))


def _ep_ret_rope(product, extra, outs):
    cos_ref, sin_ref = extra
    o_ref = outs[0]
    half = RET_DK // 2
    acc = product()
    c = cos_ref[...]
    s = sin_ref[...]
    for hh in range(o_ref.shape[0]):
        lo = hh * RET_DK
        x1 = acc[:, lo:lo + half]
        x2 = acc[:, lo + half:lo + RET_DK]
        o_ref[hh, :, :half] = (x1 * c - x2 * s).astype(o_ref.dtype)
        o_ref[hh, :, half:] = (x1 * s + x2 * c).astype(o_ref.dtype)


def _ep_ret_gate(product, extra, outs):
    _ep_slabs(lambda: _silu(product()), extra, outs)


def _ep_mla_a(product, extra, outs):
    qn_ref, kvn_ref, cc_ref, ss_ref = extra
    cq_ref, ckv_ref, kr_ref = outs
    acc = product()
    a0 = MLA_Q_LORA
    a1 = a0 + MLA_KV_LORA
    a2 = a1 + LANES
    cq_ref[...] = _rms_rows(acc[:, :a0], qn_ref[...]).astype(cq_ref.dtype)
    ckv_ref[...] = _rms_rows(acc[:, a0:a1], kvn_ref[...]).astype(ckv_ref.dtype)
    kr = acc[:, a1:a2] * cc_ref[...] + acc[:, a2:] * ss_ref[...]
    kr_ref[...] = kr.astype(kr_ref.dtype)


def _ep_mla_q(product, extra, outs, *, scale):
    cc_ref, ss_ref = extra
    o_ref = outs[0]
    acc = product()
    cc = cc_ref[...]
    ss = ss_ref[...]
    for hh in range(o_ref.shape[1] // MXU_DIM):
        lo = hh * MXU_DIM
        mid = lo + LANES
        hi = lo + MXU_DIM
        o_ref[:, lo:mid] = (acc[:, lo:mid] * scale).astype(o_ref.dtype)
        x = acc[:, mid:hi]
        xr = pltpu.roll(x, LANES // 2, 1)
        o_ref[:, mid:hi] = ((x * cc + xr * ss) * scale).astype(o_ref.dtype)


def _row_spec(tm, width, period):
    return pl.BlockSpec((tm, width), lambda i, j: (i % period, 0))


def _slab_out(n_slabs, rows, width, tile_slabs, tm):
    spec = pl.BlockSpec((tile_slabs, tm, width), lambda i, j: (j, i, 0))
    return [spec], [jax.ShapeDtypeStruct((n_slabs, rows, width), BF16)]


def _linear_t_kernel(*refs, n_extra, epilogue):
    wt_ref, x_ref = refs[0], refs[1]
    extra = refs[2:2 + n_extra]
    o_ref = refs[2 + n_extra]

    def product():
        return lax.dot_general(wt_ref[...], x_ref[...], _NT, preferred_element_type=F32)

    epilogue(product, extra, o_ref)


def _linear_t(x, wt, *, name, tm, epilogue, extras=(), extra_specs=()):
    m, kdim = x.shape
    n = wt.shape[0]
    assert m % tm == 0, (m, tm)
    tn = _col_tile(kdim, n, tm=tm, **_PLAIN)
    in_specs = [pl.BlockSpec((tn, kdim), lambda i, j: (j, 0)),
                pl.BlockSpec((tm, kdim), lambda i, j: (i, 0))]
    in_specs.extend(extra_specs)
    kernel = functools.partial(_linear_t_kernel, n_extra=len(extras), epilogue=epilogue)
    return pl.pallas_call(
        kernel,
        grid=(m // tm, n // tn),
        in_specs=in_specs,
        out_specs=pl.BlockSpec((tn, tm), lambda i, j: (j, i)),
        out_shape=jax.ShapeDtypeStruct((n, m), BF16),
        compiler_params=_compiler_params(("parallel", "arbitrary")),
        name=name,
    )(wt, x, *extras)


def _ep_t_plain(product, extra, o_ref):
    o_ref[...] = product().astype(o_ref.dtype)


def _ep_t_mla_q(product, extra, o_ref, *, scale):
    cc_ref, ss_ref = extra
    acc = product()
    cc = cc_ref[...]
    ss = ss_ref[...]
    half = LANES // 2
    for hh in range(o_ref.shape[0] // MXU_DIM):
        lo = hh * MXU_DIM
        mid = lo + LANES
        hi = lo + MXU_DIM
        o_ref[lo:mid, :] = (acc[lo:mid, :] * scale).astype(o_ref.dtype)
        x = acc[mid:hi, :]
        xr = jnp.concatenate([x[half:], x[:half]], axis=0)
        o_ref[mid:hi, :] = ((x * cc + xr * ss) * scale).astype(o_ref.dtype)


def _mlp_down_kernel(*refs, nk, final_norm):
    if final_norm:
        x_ref, w_ref, res_ref, g_ref, o_ref = refs
    else:
        x_ref, w_ref, res_ref, o_ref = refs
    k = pl.program_id(1)

    def product():
        return jnp.dot(x_ref[...], w_ref[...], preferred_element_type=F32)

    @pl.when(k == 0)
    def _():
        o_ref[...] = res_ref[...] + product()

    @pl.when(k > 0)
    def _():
        o_ref[...] += product()

    if final_norm:
        @pl.when(k == nk - 1)
        def _():
            _norm_rows_pass(o_ref, g_ref, o_ref)


def _mlp_down(hid, w2, res, *, tm, final_g=None):
    m, kdim = hid.shape
    d = w2.shape[1]
    tk = _tile(kdim, DOWN_K_TILE)
    nk = kdim // tk
    final_norm = final_g is not None
    in_specs = [pl.BlockSpec((tm, tk), lambda i, k: (i, k)),
                pl.BlockSpec((tk, d), lambda i, k: (k, 0)),
                pl.BlockSpec((tm, d), lambda i, k: (i, 0))]
    args = [hid, w2, res]
    if final_norm:
        in_specs.append(pl.BlockSpec((1, d), lambda i, k: (0, 0)))
        args.append(final_g.reshape(1, d).astype(F32))
    return pl.pallas_call(
        functools.partial(_mlp_down_kernel, nk=nk, final_norm=final_norm),
        grid=(m // tm, nk),
        in_specs=in_specs,
        out_specs=pl.BlockSpec((tm, d), lambda i, k: (i, 0)),
        out_shape=jax.ShapeDtypeStruct((m, d), F32),
        compiler_params=_compiler_params(("parallel", "arbitrary")),
        name="mlp_down",
    )(*args)


def _ret_kernel(cd_ref, q_ref, k_ref, v_ref, g_ref, qm_ref, km_ref, vm_ref, gm_ref,
                dmask_ref, qdf_ref, kdf_ref, qdb_ref, kdb_ref,
                o_ref, om_ref, rf_ref, rb_ref, opart_ref, *, n_groups, group):
    h = pl.program_id(1)
    t = pl.program_id(2)
    c = RET_CHUNK
    cd_f = cd_ref[h]
    cd_b = cd_ref[RET_HEADS + h]

    def decayed(x, dec_ref):
        return x * dec_ref[0]

    def wide(ref, sl):
        return jnp.concatenate([ref[0, sl, :], ref[1, sl, :]], axis=1)

    def pad_meta(x):
        return jnp.concatenate([jnp.zeros((c - N_META, x.shape[1]), x.dtype), x], axis=0)

    def chunk_rows(cidx):
        if isinstance(cidx, int):
            return slice(cidx * c, (cidx + 1) * c)
        return pl.ds(pl.multiple_of(cidx * c, c), c)

    def bwd_update(kc, vc):
        return lax.dot_general(decayed(kc, kdb_ref), vc, _TN, preferred_element_type=F32)

    def bwd_out(state, qc, cidx):
        ob = jnp.dot(decayed(qc, qdb_ref), state.astype(BF16), preferred_element_type=F32)
        opart_ref[chunk_rows(cidx), :] = ob

    def fwd_indep(qc, kc, vc):
        s = lax.dot_general(qc, kc, _NT, preferred_element_type=F32) * dmask_ref[0]
        sv = jnp.dot(s.astype(BF16), vc, preferred_element_type=F32)
        upd = lax.dot_general(decayed(kc, kdf_ref), vc, _TN, preferred_element_type=F32)
        return sv, upd

    def fwd_finish(state, indep, qc, gc, cidx):
        sv, upd = indep
        o = sv + jnp.dot(decayed(qc, qdf_ref), state.astype(BF16), preferred_element_type=F32)
        o = o + opart_ref[chunk_rows(cidx), :]
        mu = jnp.mean(o, axis=-1, keepdims=True)
        xc = o - mu
        var = jnp.mean(xc * xc, axis=-1, keepdims=True)
        on = xc * lax.rsqrt(var + NORM_EPS)
        return cd_f * state + upd, on.astype(BF16) * gc

    all_meta = slice(0, N_META)

    def rows_of(cc):
        return slice(cc * c, (cc + 1) * c)

    @pl.when(t == 0)
    def _():
        rb_ref[...] = jnp.zeros_like(rb_ref)

    @pl.when(t < n_groups)
    def _():
        grp = n_groups - 1 - t
        state = rb_ref[...]
        order = list(range(group - 1, -1, -1))
        upd_next = bwd_update(k_ref[rows_of(order[0]), :], wide(v_ref, rows_of(order[0])))
        for n, cc in enumerate(order):
            upd = upd_next
            if n + 1 < group:
                nxt = rows_of(order[n + 1])
                upd_next = bwd_update(k_ref[nxt, :], wide(v_ref, nxt))
            bwd_out(state, q_ref[rows_of(cc), :], 1 + grp * group + cc)
            state = cd_b * state + upd
        rb_ref[...] = state

    @pl.when(t == n_groups - 1)
    def _():
        bwd_out(rb_ref[...], pad_meta(qm_ref[...]), 0)

    @pl.when(t == n_groups)
    def _():
        qm = pad_meta(qm_ref[...])
        indep = fwd_indep(qm, pad_meta(km_ref[...]), pad_meta(wide(vm_ref, all_meta)))
        state, res = fwd_finish(jnp.zeros(rf_ref.shape, F32), indep, qm, pad_meta(wide(gm_ref, all_meta)), 0)
        rf_ref[...] = state
        om_ref[0] = res[c - N_META:, :RET_SLAB]
        om_ref[1] = res[c - N_META:, RET_SLAB:]

    @pl.when(t >= n_groups)
    def _():
        grp = t - n_groups
        state = rf_ref[...]

        def indep_of(cc):
            sl = rows_of(cc)
            return fwd_indep(q_ref[sl, :], k_ref[sl, :], wide(v_ref, sl))

        indep_next = indep_of(0)
        for cc in range(group):
            indep = indep_next
            if cc + 1 < group:
                indep_next = indep_of(cc + 1)
            sl = rows_of(cc)
            state, res = fwd_finish(state, indep, q_ref[sl, :], wide(g_ref, sl), 1 + grp * group + cc)
            o_ref[0, sl, :] = res[:, :RET_SLAB]
            o_ref[1, sl, :] = res[:, RET_SLAB:]
        rf_ref[...] = state


def _retention_tables():
    c = RET_CHUNK
    hh = jnp.arange(RET_HEADS, dtype=F32)
    lg_f = jnp.log(1.0 - 2.0 ** (-5.0 - hh))
    lg_b = jnp.log(1.0 - 2.0 ** (-5.5 - hh))
    i = jnp.arange(c, dtype=F32)
    diff = i[:, None] - i[None, :]
    ad = jnp.abs(diff)[None]
    dmask = (jnp.where(diff[None] >= 0, jnp.exp(lg_f[:, None, None] * ad), 0.0)
             + jnp.where(diff[None] < 0, jnp.exp(lg_b[:, None, None] * ad), 0.0))

    def wide(v):
        return jnp.broadcast_to(v[:, :, None], (RET_HEADS, c, RET_DK)).astype(BF16)

    qd_f = wide(jnp.exp(lg_f[:, None] * (i + 1.0)[None]))
    kd_f = wide(jnp.exp(lg_f[:, None] * (c - 1.0 - i)[None]))
    qd_b = wide(jnp.exp(lg_b[:, None] * (c - i)[None]))
    kd_b = wide(jnp.exp(lg_b[:, None] * i[None]))
    cd = jnp.concatenate([jnp.exp(lg_f * c), jnp.exp(lg_b * c)])
    return cd, dmask, qd_f, kd_f, qd_b, kd_b


def _retention(qkvg_r, qkvg_m, batch, seq, tables):
    cd, dmask, qd_f, kd_f, qd_b, kd_b = tables
    c = RET_CHUNK
    n_real = seq // c
    group = _tile(n_real, RET_GROUP)
    n_groups = n_real // group
    rows = group * c
    v_per = RET_DV // RET_SLAB
    k_off = RET_HEADS
    v_off = 2 * RET_HEADS // v_per
    g_off = v_off + RET_HEADS

    def sweep(t):
        return jnp.where(t < n_groups, n_groups - 1 - t, t - n_groups)

    def fwd_only(t):
        return jnp.maximum(t - n_groups, 0)

    in_specs = [
        pl.BlockSpec((None, rows, RET_SLAB), lambda b, h, t, cd: (h, b * n_groups + sweep(t), 0)),
        pl.BlockSpec((None, rows, RET_SLAB), lambda b, h, t, cd: (k_off + h, b * n_groups + sweep(t), 0)),
        pl.BlockSpec((v_per, rows, RET_SLAB), lambda b, h, t, cd: (v_off + h, b * n_groups + sweep(t), 0)),
        pl.BlockSpec((v_per, rows, RET_SLAB), lambda b, h, t, cd: (g_off + h, b * n_groups + fwd_only(t), 0)),
        pl.BlockSpec((None, N_META, RET_SLAB), lambda b, h, t, cd: (h, b, 0)),
        pl.BlockSpec((None, N_META, RET_SLAB), lambda b, h, t, cd: (k_off + h, b, 0)),
        pl.BlockSpec((v_per, N_META, RET_SLAB), lambda b, h, t, cd: (v_off + h, b, 0)),
        pl.BlockSpec((v_per, N_META, RET_SLAB), lambda b, h, t, cd: (g_off + h, b, 0)),
        pl.BlockSpec((1, c, c), lambda b, h, t, cd: (h, 0, 0)),
        pl.BlockSpec((1, c, RET_DK), lambda b, h, t, cd: (h, 0, 0)),
        pl.BlockSpec((1, c, RET_DK), lambda b, h, t, cd: (h, 0, 0)),
        pl.BlockSpec((1, c, RET_DK), lambda b, h, t, cd: (h, 0, 0)),
        pl.BlockSpec((1, c, RET_DK), lambda b, h, t, cd: (h, 0, 0)),
    ]
    out_specs = [
        pl.BlockSpec((v_per, rows, RET_SLAB), lambda b, h, t, cd: (h, b * n_groups + fwd_only(t), 0)),
        pl.BlockSpec((v_per, N_META, RET_SLAB), lambda b, h, t, cd: (h, b, 0)),
    ]
    out_shape = [
        jax.ShapeDtypeStruct((RET_HEADS * v_per, batch * seq, RET_SLAB), BF16),
        jax.ShapeDtypeStruct((RET_HEADS * v_per, batch * N_META, RET_SLAB), BF16),
    ]
    grid_spec = pltpu.PrefetchScalarGridSpec(
        num_scalar_prefetch=1,
        grid=(batch, RET_HEADS, 2 * n_groups),
        in_specs=in_specs,
        out_specs=out_specs,
        scratch_shapes=[
            pltpu.VMEM((RET_DK, RET_DV), F32),
            pltpu.VMEM((RET_DK, RET_DV), F32),
            pltpu.VMEM(((n_real + 1) * c, RET_DV), F32),
        ],
    )
    kernel = functools.partial(_ret_kernel, n_groups=n_groups, group=group)
    return pl.pallas_call(
        kernel,
        grid_spec=grid_spec,
        out_shape=out_shape,
        compiler_params=_compiler_params(("parallel", "parallel", "arbitrary")),
        name="retention",
    )(cd, qkvg_r, qkvg_r, qkvg_r, qkvg_r, qkvg_m, qkvg_m, qkvg_m, qkvg_m,
      dmask, qd_f, kd_f, qd_b, kd_b)


def _attn_kernel(qt_ref, kn_ref, kr_ref, vt_ref, knm_ref, krm_ref, vm_ref, o_ref, *, tk):
    def values(vt, n):
        return jnp.concatenate([vt, jnp.ones((ONES_ROWS, n), BF16)], axis=0)

    qt = qt_ref[...]

    def scores(kn, kr):
        return jnp.dot(jnp.concatenate([kn, kr], axis=1), qt, preferred_element_type=F32)

    def block(c):
        sl = slice(c * tk, (c + 1) * tk)
        return scores(kn_ref[sl, :], kr_ref[sl, :])

    nk = kn_ref.shape[0] // tk
    s = scores(knm_ref[...], krm_ref[...])
    s_next = block(0)
    m = jnp.max(s, axis=0, keepdims=True)
    p = jnp.exp2(s - m).astype(BF16)
    vm_t = vm_ref[...].astype(F32).T.astype(BF16)
    acc = jnp.dot(values(vm_t, N_META), p, preferred_element_type=F32)
    for c in range(nk):
        s = s_next
        if c + 1 < nk:
            s_next = block(c + 1)
        m_new = jnp.maximum(m, jnp.max(s, axis=0, keepdims=True))
        alpha = jnp.exp2(m - m_new)
        p = jnp.exp2(s - m_new).astype(BF16)
        sl = slice(c * tk, (c + 1) * tk)
        acc = alpha * acc + jnp.dot(values(vt_ref[:, sl], tk), p, preferred_element_type=F32)
        m = m_new
    o_ref[...] = (acc[:MLA_V] / acc[MLA_V:MLA_V + 1]).T.astype(o_ref.dtype)


def _attn_meta_kernel(q_ref, kn_ref, kr_ref, vt_ref, knm_ref, krm_ref, vm_ref, o_ref, *, tk, heads):
    def cols(hh, width):
        return slice(hh * width, (hh + 1) * width)

    qs = [q_ref[:, cols(hh, MXU_DIM)] for hh in range(heads)]

    def scores(q, kn, kr):
        return lax.dot_general(q, jnp.concatenate([kn, kr], axis=1), _NT, preferred_element_type=F32)

    state = []
    for hh in range(heads):
        s = scores(qs[hh], knm_ref[:, cols(hh, MLA_NOPE)], krm_ref[...])
        m0 = jnp.max(s, axis=1, keepdims=True)
        p = jnp.exp2(s - m0)
        state += [m0, jnp.sum(p, axis=1, keepdims=True),
                  jnp.dot(p.astype(BF16), vm_ref[:, cols(hh, MLA_V)], preferred_element_type=F32)]

    def body(c, state):
        sl = pl.ds(pl.multiple_of(c * tk, tk), tk)
        kr = kr_ref[sl, :]
        new = []
        for hh in range(heads):
            m_prev, l_prev, acc = state[3 * hh:3 * hh + 3]
            s = scores(qs[hh], kn_ref[hh, sl, :], kr)
            m_new = jnp.maximum(m_prev, jnp.max(s, axis=1, keepdims=True))
            alpha = jnp.exp2(m_prev - m_new)
            p = jnp.exp2(s - m_new)
            pv = lax.dot_general(p.astype(BF16), vt_ref[cols(hh, MLA_V), sl], _NT,
                                 preferred_element_type=F32)
            new += [m_new, alpha * l_prev + jnp.sum(p, axis=1, keepdims=True), alpha * acc + pv]
        return tuple(new)

    trips = kn_ref.shape[1] // tk
    state = lax.fori_loop(0, trips, body, tuple(state), unroll=math.gcd(trips, ATT_META_UNROLL))
    for hh in range(heads):
        o_ref[hh] = (state[3 * hh + 2] / state[3 * hh + 1]).astype(o_ref.dtype)


def _attention(qt, q_m, kn, kr, vt, kv_m, kr_m, batch, seq):
    tq = _tile(seq, ATT_TQ)
    nq = seq // tq
    o_r = pl.pallas_call(
        functools.partial(_attn_kernel, tk=_tile(seq, ATT_TK)),
        grid=(batch, MLA_HEADS, nq),
        in_specs=[
            pl.BlockSpec((MXU_DIM, tq), lambda b, h, i: (h, b * nq + i)),
            pl.BlockSpec((None, seq, MLA_NOPE), lambda b, h, i: (h, b, 0)),
            pl.BlockSpec((seq, LANES), lambda b, h, i: (b, 0)),
            pl.BlockSpec((MLA_V, seq), lambda b, h, i: (h, b)),
            pl.BlockSpec((N_META, MLA_NOPE), lambda b, h, i: (b, h)),
            pl.BlockSpec((N_META, LANES), lambda b, h, i: (b, 0)),
            pl.BlockSpec((N_META, MLA_V), lambda b, h, i: (b, MLA_HEADS + h)),
        ],
        out_specs=pl.BlockSpec((None, tq, MLA_V), lambda b, h, i: (h, b * nq + i, 0)),
        out_shape=jax.ShapeDtypeStruct((MLA_HEADS, batch * seq, MLA_V), BF16),
        compiler_params=_compiler_params(("parallel", "parallel", "arbitrary")),
        name="attention",
    )(qt, kn, kr, vt, kv_m, kr_m, kv_m)
    if q_m is None:
        return o_r, None
    hg = ATT_META_HEADS
    n_hg = MLA_HEADS // hg
    o_m = pl.pallas_call(
        functools.partial(_attn_meta_kernel, tk=_tile(seq, ATT_META_TK), heads=hg),
        grid=(batch, n_hg),
        in_specs=[
            pl.BlockSpec((N_META, hg * MXU_DIM), lambda b, g: (b, g)),
            pl.BlockSpec((hg, seq, MLA_NOPE), lambda b, g: (g, b, 0)),
            pl.BlockSpec((seq, LANES), lambda b, g: (b, 0)),
            pl.BlockSpec((hg * MLA_V, seq), lambda b, g: (g, b)),
            pl.BlockSpec((N_META, hg * MLA_NOPE), lambda b, g: (b, g)),
            pl.BlockSpec((N_META, LANES), lambda b, g: (b, 0)),
            pl.BlockSpec((N_META, hg * MLA_V), lambda b, g: (b, n_hg + g)),
        ],
        out_specs=pl.BlockSpec((hg, N_META, MLA_V), lambda b, g: (g, b, 0)),
        out_shape=jax.ShapeDtypeStruct((MLA_HEADS, batch * N_META, MLA_V), BF16),
        compiler_params=_compiler_params(("parallel", "arbitrary")),
        name="attention_meta",
    )(q_m, kn, kr, vt, kv_m, kr_m, kv_m)
    return o_r, o_m


def _rope_tables(n, dim):
    inv = 1.0 / (ROPE_BASE ** (jnp.arange(0, dim, 2, dtype=F32) / dim))
    ang = jnp.arange(n, dtype=F32)[:, None] * inv[None, :]
    return jnp.cos(ang), jnp.sin(ang)


def _swap_halves(w):
    half = w.shape[-1] // 2
    return jnp.concatenate([w[..., half:], w[..., :half]], axis=-1)


def _prep_weights(p):
    d = D_MODEL
    out = {}
    out["ret_qkvg"] = [
        jnp.concatenate([p["ret_wq"][j], p["ret_wk"][j] * (RET_DK ** -0.5),
                         p["ret_wv"][j], p["ret_wg"][j]], axis=1).astype(BF16)
        for j in range(p["ret_wq"].shape[0])]
    out["ret_wo"] = [p["ret_wo"][j].astype(BF16) for j in range(p["ret_wo"].shape[0])]
    keys = ("mla_a", "mla_qb", "mla_qb_t", "mla_kvb", "mla_kn", "mla_v_t", "mla_wo")
    for key in keys:
        out[key] = []
    zeros = jnp.zeros((d, LANES - MLA_ROPE), F32)
    for j in range(p["mla_wq_a"].shape[0]):
        wkv_a = p["mla_wkv_a"][j]
        wr = wkv_a[:, MLA_KV_LORA:]
        out["mla_a"].append(jnp.concatenate(
            [p["mla_wq_a"][j], wkv_a[:, :MLA_KV_LORA], wr, zeros, _swap_halves(wr), zeros],
            axis=1).astype(BF16))
        wq_b = p["mla_wq_b"][j].reshape(MLA_Q_LORA, MLA_HEADS, MLA_NOPE + MLA_ROPE)
        rope = wq_b[..., MLA_NOPE:]
        qb = jnp.concatenate([wq_b[..., :MLA_NOPE], rope, _swap_halves(rope)], axis=-1)
        qb = qb.reshape(MLA_Q_LORA, MLA_HEADS * MXU_DIM).astype(BF16)
        out["mla_qb"].append(qb)
        out["mla_qb_t"].append(qb.T)
        wkv_b = p["mla_wkv_b"][j].reshape(MLA_KV_LORA, MLA_HEADS, MLA_NOPE + MLA_V)
        kn = wkv_b[..., :MLA_NOPE].reshape(MLA_KV_LORA, MLA_HEADS * MLA_NOPE).astype(BF16)
        vv = wkv_b[..., MLA_NOPE:].reshape(MLA_KV_LORA, MLA_HEADS * MLA_V).astype(BF16)
        out["mla_kvb"].append(jnp.concatenate([kn, vv], axis=1))
        out["mla_kn"].append(kn)
        out["mla_v_t"].append(vv.T)
        out["mla_wo"].append(p["mla_wo"][j].astype(BF16))
    out["mlp_w1"] = [p["mlp_w1"][i].astype(BF16) for i in range(p["mlp_w1"].shape[0])]
    out["mlp_w2"] = [p["mlp_w2"][i].astype(BF16) for i in range(p["mlp_w2"].shape[0])]
    return out


def _position_tables(batch, seq):
    n = seq + N_META
    cos_r, sin_r = _rope_tables(n, RET_DK)
    cos_m, sin_m = _rope_tables(n, MLA_ROPE)
    pad = jnp.zeros((n, LANES - MLA_ROPE), F32)
    cc = jnp.concatenate([cos_m, cos_m, pad], axis=1)
    ss = jnp.concatenate([-sin_m, sin_m, pad], axis=1)

    def split(tbl):
        return tbl[N_META:], jnp.tile(tbl[:N_META], (batch, 1))

    return {"ret_cos": split(cos_r), "ret_sin": split(sin_r),
            "mla_cc": split(cc), "mla_ss": split(ss),
            "mla_cc_t": cc[N_META:].T, "mla_ss_t": ss[N_META:].T}


def _trunk(x, p, w, ret_tables):
    batch, seq, d = x.shape
    pos = _position_tables(batch, seq)
    h_r = x.reshape(batch * seq, d)
    h_m = jnp.broadcast_to(p["meta_tokens"].astype(F32)[None], (batch, N_META, d)).reshape(batch * N_META, d)
    depth = p["norm1_g"].shape[0]
    q_scale = math.log2(math.e) * (MLA_NOPE + MLA_ROPE) ** -0.5

    def both(fn, last=False):
        return fn(0), (None if last else fn(1))

    tms = (_tile(seq, ROW_TILE), batch * N_META)
    pers = (seq // tms[0], 1)
    n_rows = (batch * seq, batch * N_META)

    for i in range(depth):
        last = i == depth - 1
        j = i // 2
        hs = (h_r, h_m)
        if i % 2 == 0:
            w_in = w["ret_qkvg"][j]
            tn = _col_tile(d, 2 * RET_HEADS * RET_DK, **_NORM_IN)
            ep = [(2 * RET_HEADS * RET_DK // tn, _ep_ret_rope), (RET_HEADS * RET_DV // tn, _ep_slabs),
                  (RET_HEADS * RET_DV // tn, _ep_ret_gate)]

            def qkvg(which):
                tm, per = tms[which], pers[which]
                o_specs, o_shape = _slab_out(w_in.shape[1] // RET_SLAB, n_rows[which], RET_SLAB,
                                             tn // RET_SLAB, tm)
                return _linear(hs[which], w_in, name="ret_qkvg", tm=tm, tn=tn, epilogue=ep,
                               norm_g=p["norm1_g"][i], out_specs=o_specs, out_shape=o_shape,
                               extras=(pos["ret_cos"][which], pos["ret_sin"][which]),
                               extra_specs=(_row_spec(tm, LANES, per), _row_spec(tm, LANES, per)))[0]

            qkvg_r, qkvg_m = both(qkvg)
            mix_r, mix_m = _retention(qkvg_r, qkvg_m, batch, seq, ret_tables)
            w_out = w["ret_wo"][j]
        else:
            w_a = w["mla_a"][j]

            def stage_a(which):
                tm, per = tms[which], pers[which]
                one = lambda width: pl.BlockSpec((1, width), lambda i_, j_: (0, 0))
                return _linear(hs[which], w_a, name="mla_a", tm=tm, tn=w_a.shape[1], epilogue=_ep_mla_a,
                               out_widths=[MLA_Q_LORA, MLA_KV_LORA, LANES], out_dtypes=[BF16] * 3,
                               out_col_tiles=[MLA_Q_LORA, MLA_KV_LORA, LANES],
                               norm_g=p["norm1_g"][i],
                               extras=(p["mla_q_norm"][j].reshape(1, -1), p["mla_kv_norm"][j].reshape(1, -1),
                                       pos["mla_cc"][which], pos["mla_ss"][which]),
                               extra_specs=(one(MLA_Q_LORA), one(MLA_KV_LORA),
                                            _row_spec(tm, LANES, per), _row_spec(tm, LANES, per)))

            (cq_r, ckv_r, kr_r), (cq_m, ckv_m, kr_m) = both(stage_a)
            tm_r, per_r = tms[0], pers[0]
            t_spec = pl.BlockSpec((LANES, tm_r), lambda i_, j_: (0, i_ % per_r))
            qt_r = _linear_t(cq_r, w["mla_qb_t"][j], name="mla_qb_t", tm=tm_r,
                             epilogue=functools.partial(_ep_t_mla_q, scale=q_scale),
                             extras=(pos["mla_cc_t"], pos["mla_ss_t"]), extra_specs=(t_spec, t_spec))
            tn_kn = _col_tile(MLA_KV_LORA, MLA_HEADS * MLA_NOPE, **_PLAIN)
            kn_specs, kn_shape = _slab_out(MLA_HEADS, n_rows[0], MLA_NOPE, tn_kn // MLA_NOPE, tm_r)
            kn_r = _linear(ckv_r, w["mla_kn"][j], name="mla_kn", tm=tm_r, tn=tn_kn, epilogue=_ep_slabs,
                           out_specs=kn_specs, out_shape=kn_shape)[0]
            vt_r = _linear_t(ckv_r, w["mla_v_t"][j], name="mla_v_t", tm=tm_r, epilogue=_ep_t_plain)
            tm_m = tms[1]
            kv_m = _linear(ckv_m, w["mla_kvb"][j], name="mla_kvb", tm=tm_m, tn=MAX_COL_TILE,
                           epilogue=_ep_plain, out_widths=[MLA_HEADS * (MLA_NOPE + MLA_V)],
                           out_dtypes=[BF16])[0]
            q_m = None
            if not last:
                m_spec = _row_spec(tm_m, LANES, 1)
                q_m = _linear(cq_m, w["mla_qb"][j], name="mla_qb", tm=tm_m, tn=MAX_COL_TILE,
                              epilogue=functools.partial(_ep_mla_q, scale=q_scale),
                              out_widths=[MLA_HEADS * MXU_DIM], out_dtypes=[BF16],
                              extras=(pos["mla_cc"][1], pos["mla_ss"][1]), extra_specs=(m_spec, m_spec))[0]
            mix_r, mix_m = _attention(qt_r, q_m, kn_r, kr_r, vt_r, kv_m, kr_m, batch, seq)
            w_out = w["mla_wo"][j]

        mixes = (mix_r, mix_m)
        tm_out = _tile(seq, ROW_TILE // 2) if w_out.shape[0] <= OUT_RESIDENT_K else tms[0]
        tn_out = _col_tile(w_out.shape[0], d, tm=tm_out, **_RES_OUT)

        def proj_out(which):
            tm = tm_out if which == 0 else tms[1]
            return _linear(mixes[which], w_out, name="mix_out", tm=tm, tn=tn_out, epilogue=_ep_residual,
                           out_widths=[d], out_dtypes=[F32], extras=(hs[which],),
                           extra_specs=(pl.BlockSpec((tm, tn_out), lambda i_, j_: (i_, j_)),))[0]

        h_r, h_m = both(proj_out, last)
        hs = (h_r, h_m)
        tn_up = _col_tile(d, D_FF, **_NORM_IN)

        def mlp(which):
            tm = tms[which]
            hid = _linear(hs[which], w["mlp_w1"][i], name="mlp_up", tm=tm, tn=tn_up, epilogue=_ep_relu2,
                          out_widths=[D_FF], out_dtypes=[BF16], norm_g=p["norm2_g"][i])[0]
            return _mlp_down(hid, w["mlp_w2"][i], hs[which], tm=tm,
                             final_g=p["final_norm"] if last else None)

        h_r, h_m = both(mlp, last)

    return h_r.reshape(batch, seq, d)


def kernel(x_prompt, x_sample, meta_tokens, norm1_g, norm2_g, mlp_w1, mlp_w2, ret_wq, ret_wk, ret_wv, ret_wg, ret_wo, mla_wq_a, mla_q_norm, mla_wq_b, mla_wkv_a, mla_kv_norm, mla_wkv_b, mla_wo, final_norm):
    p = dict(meta_tokens=meta_tokens, norm1_g=norm1_g, norm2_g=norm2_g, mlp_w1=mlp_w1, mlp_w2=mlp_w2,
             ret_wq=ret_wq, ret_wk=ret_wk, ret_wv=ret_wv, ret_wg=ret_wg, ret_wo=ret_wo,
             mla_wq_a=mla_wq_a, mla_q_norm=mla_q_norm, mla_wq_b=mla_wq_b, mla_wkv_a=mla_wkv_a,
             mla_kv_norm=mla_kv_norm, mla_wkv_b=mla_wkv_b, mla_wo=mla_wo, final_norm=final_norm)
    w = _prep_weights(p)
    ret_tables = _retention_tables()
    return (_trunk(x_prompt, p, w, ret_tables), _trunk(x_sample, p, w, ret_tables))
```

```python
import functools
import math

import jax
import jax.numpy as jnp
from jax import lax
from jax.experimental import pallas as pl
from jax.experimental.pallas import tpu as pltpu

F32 = jnp.float32
BF16 = jnp.bfloat16

D_MODEL = 2048
N_META = 16
D_FF = 4 * D_MODEL
NORM_EPS = 1e-6
ROPE_BASE = 10000.0
RET_HEADS = 8
RET_DK = 256
RET_DV = 512
MLA_HEADS = 16
MLA_Q_LORA = 512
MLA_KV_LORA = 512
MLA_NOPE = 128
MLA_ROPE = 64
MLA_V = 128

LANES = 128
MXU_DIM = 256
VMEM_LIMIT_BYTES = 56 * 1024 * 1024
VMEM_TILE_BUDGET = 48 * 1024 * 1024

ROW_TILE = 1024
TRANSPOSED_ROW_TILE = 2048
STEP_MACS = 2 ** 32
MAX_COL_TILE = 2048
DOWN_K_TILE = 1024
OUT_RESIDENT_K = 2048
NORM_ROWS = 64
NORM_UNROLL = 4
NORM_SUBTILES = 4
RET_CHUNK = 256
RET_GROUP = 8
RET_SLAB = RET_DK
ATT_TQ = 1024
ATT_TK = 512
ATT_META_TK = 2048
ATT_META_HEADS = 4
ATT_META_UNROLL = 2
ONES_ROWS = 16

_NT = (((1,), (1,)), ((), ()))
_TN = (((0,), (0,)), ((), ()))


def _tile(n, pref):
    t = min(n, pref)
    assert n % t == 0, (n, pref)
    return t


def _col_tile(kdim, n, *, x_bytes, has_norm, out_bytes, tm=ROW_TILE):
    def footprint(tn):
        return (2 * tm * kdim * x_bytes + (2 * tm * kdim if has_norm else 0)
                + 2 * kdim * tn * 2 + 2 * tm * tn * out_bytes)

    tn = min(n, MAX_COL_TILE)
    while tn > MXU_DIM and (tm * kdim * tn > STEP_MACS or footprint(tn) > VMEM_TILE_BUDGET):
        tn //= 2
    assert n % tn == 0, (n, tn)
    return tn


_NORM_IN = dict(x_bytes=4, has_norm=True, out_bytes=2)
_RES_OUT = dict(x_bytes=2, has_norm=False, out_bytes=8)
_PLAIN = dict(x_bytes=2, has_norm=False, out_bytes=2)


def _compiler_params(semantics):
    return pltpu.CompilerParams(dimension_semantics=semantics,
                                vmem_limit_bytes=VMEM_LIMIT_BYTES)


def _rms_rows(x, g):
    ms = jnp.mean(x * x, axis=-1, keepdims=True)
    return x * lax.rsqrt(ms + NORM_EPS) * g


def _norm_rows_pass(x_ref, g_ref, o_ref):
    rows = x_ref.shape[0]
    step = min(rows, NORM_ROWS)

    def body(r, carry):
        sl = pl.ds(pl.multiple_of(r * step, step), step)
        o_ref[sl, :] = _rms_rows(x_ref[sl, :], g_ref[...]).astype(o_ref.dtype)
        return carry

    trips = rows // step
    lax.fori_loop(0, trips, body, 0, unroll=math.gcd(trips, NORM_UNROLL))


def _linear_kernel(*refs, has_norm, n_slabs, n_extra, n_out, epilogue):
    x_ref = refs[0]
    pos = 1
    g_ref = None
    if has_norm:
        g_ref = refs[pos]
        pos += 1
    w_ref = refs[pos]
    pos += 1
    extra = refs[pos:pos + n_extra]
    pos += n_extra
    outs = refs[pos:pos + n_out]
    pos += n_out
    j = pl.program_id(1)
    a_ref = refs[pos] if has_norm else None
    tm = x_ref.shape[-2]

    def product():
        if has_norm:
            lhs = a_ref[...]
        elif n_slabs:
            lhs = jnp.concatenate([x_ref[s] for s in range(n_slabs)], axis=1)
        else:
            lhs = x_ref[...]
        return jnp.dot(lhs, w_ref[...], preferred_element_type=F32)

    def rows_of(ref, rows):
        if len(ref.shape) == 3:
            return ref.at[:, rows] if ref.shape[1] == tm else ref
        return ref.at[rows] if ref.shape[0] == tm else ref

    def first_step(fn):
        sub = tm // NORM_SUBTILES if tm % (NORM_SUBTILES * NORM_ROWS) == 0 else tm
        for r in range(tm // sub):
            rows = pl.ds(r * sub, sub)
            _norm_rows_pass(x_ref.at[rows], g_ref, a_ref.at[rows])

            def sub_product(rows=rows):
                return jnp.dot(a_ref[rows, :], w_ref[...], preferred_element_type=F32)

            fn(sub_product, [rows_of(e, rows) for e in extra], [rows_of(o, rows) for o in outs])

    if len(epilogue) == 1 and not has_norm:
        epilogue[0][1](product, extra, outs)
        return
    lo = 0
    for seg, (count, fn) in enumerate(epilogue):
        hi = lo + count
        if has_norm and seg == 0:
            pl.when(j == 0)(functools.partial(first_step, fn))
            lo = 1
        if hi > lo:
            pl.when(jnp.logical_and(j >= lo, j < hi))(functools.partial(fn, product, extra, outs))
        lo = hi


def _linear(x, w, *, name, tm, tn, epilogue, out_widths=None, out_dtypes=None, norm_g=None,
            extras=(), extra_specs=(), out_col_tiles=None, out_specs=None, out_shape=None):
    kdim, n = w.shape
    n_slabs = x.shape[0] if x.ndim == 3 else 0
    m = x.shape[-2]
    assert m % tm == 0 and n % tn == 0, (x.shape, w.shape, tm, tn)
    has_norm = norm_g is not None

    if n_slabs:
        assert n_slabs * x.shape[2] == kdim
        in_specs = [pl.BlockSpec((n_slabs, tm, x.shape[2]), lambda i, j: (0, i, 0))]
    else:
        assert x.shape[1] == kdim
        in_specs = [pl.BlockSpec((tm, kdim), lambda i, j: (i, 0))]
    args = [x]
    if has_norm:
        in_specs.append(pl.BlockSpec((1, kdim), lambda i, j: (0, 0)))
        args.append(norm_g.reshape(1, kdim).astype(F32))
    in_specs.append(pl.BlockSpec((kdim, tn), lambda i, j: (0, j)))
    args.append(w)
    in_specs.extend(extra_specs)
    args.extend(extras)

    if out_specs is None:
        if out_col_tiles is None:
            out_col_tiles = [tn] * len(out_widths)
        out_specs = [pl.BlockSpec((tm, ct), lambda i, j: (i, j)) for ct in out_col_tiles]
        out_shape = [jax.ShapeDtypeStruct((m, wd), dt) for wd, dt in zip(out_widths, out_dtypes)]
    scratch = [pltpu.VMEM((tm, kdim), BF16)] if has_norm else []
    if callable(epilogue):
        epilogue = [(n // tn, epilogue)]
    assert sum(count for count, _ in epilogue) == n // tn

    kernel = functools.partial(_linear_kernel, has_norm=has_norm, n_slabs=n_slabs,
                               n_extra=len(extras), n_out=len(out_specs), epilogue=epilogue)
    return pl.pallas_call(
        kernel,
        grid=(m // tm, n // tn),
        in_specs=in_specs,
        out_specs=out_specs,
        out_shape=out_shape,
        scratch_shapes=scratch,
        compiler_params=_compiler_params(("parallel", "arbitrary")),
        name=name,
    )(*args)


def _ep_plain(product, extra, outs):
    outs[0][...] = product().astype(outs[0].dtype)


def _ep_slabs(product, extra, outs):
    o_ref = outs[0]
    acc = product()
    width = o_ref.shape[2]
    for s in range(o_ref.shape[0]):
        o_ref[s] = acc[:, s * width:(s + 1) * width].astype(o_ref.dtype)


def _ep_relu2(product, extra, outs):
    r = jnp.maximum(product(), 0.0)
    outs[0][...] = (r * r).astype(outs[0].dtype)


def _ep_residual(product, extra, outs):
    outs[0][...] = extra[0][...] + product()


def _silu(x):
    return x * (1.0 / (1.0 + jnp.exp(-x)))


def _ep_ret_rope(product, extra, outs):
    cos_ref, sin_ref = extra
    o_ref = outs[0]
    half = RET_DK // 2
    acc = product()
    c = cos_ref[...]
    s = sin_ref[...]
    for hh in range(o_ref.shape[0]):
        lo = hh * RET_DK
        x1 = acc[:, lo:lo + half]
        x2 = acc[:, lo + half:lo + RET_DK]
        o_ref[hh, :, :half] = (x1 * c - x2 * s).astype(o_ref.dtype)
        o_ref[hh, :, half:] = (x1 * s + x2 * c).astype(o_ref.dtype)


def _ep_ret_gate(product, extra, outs):
    _ep_slabs(lambda: _silu(product()), extra, outs)


def _ep_mla_a(product, extra, outs):
    qn_ref, kvn_ref, cc_ref, ss_ref = extra
    cq_ref, ckv_ref, kr_ref = outs
    acc = product()
    a0 = MLA_Q_LORA
    a1 = a0 + MLA_KV_LORA
    a2 = a1 + LANES
    cq_ref[...] = _rms_rows(acc[:, :a0], qn_ref[...]).astype(cq_ref.dtype)
    ckv_ref[...] = _rms_rows(acc[:, a0:a1], kvn_ref[...]).astype(ckv_ref.dtype)
    kr = acc[:, a1:a2] * cc_ref[...] + acc[:, a2:] * ss_ref[...]
    kr_ref[...] = kr.astype(kr_ref.dtype)


def _ep_mla_q(product, extra, outs, *, scale):
    cc_ref, ss_ref = extra
    o_ref = outs[0]
    acc = product()
    cc = cc_ref[...]
    ss = ss_ref[...]
    for hh in range(o_ref.shape[1] // MXU_DIM):
        lo = hh * MXU_DIM
        mid = lo + LANES
        hi = lo + MXU_DIM
        o_ref[:, lo:mid] = (acc[:, lo:mid] * scale).astype(o_ref.dtype)
        x = acc[:, mid:hi]
        xr = pltpu.roll(x, LANES // 2, 1)
        o_ref[:, mid:hi] = ((x * cc + xr * ss) * scale).astype(o_ref.dtype)


def _row_spec(tm, width, period):
    return pl.BlockSpec((tm, width), lambda i, j: (i % period, 0))


def _slab_out(n_slabs, rows, width, tile_slabs, tm):
    spec = pl.BlockSpec((tile_slabs, tm, width), lambda i, j: (j, i, 0))
    return [spec], [jax.ShapeDtypeStruct((n_slabs, rows, width), BF16)]


def _linear_t_kernel(*refs, n_extra, epilogue):
    wt_ref, x_ref = refs[0], refs[1]
    extra = refs[2:2 + n_extra]
    o_ref = refs[2 + n_extra]

    def product():
        return lax.dot_general(wt_ref[...], x_ref[...], _NT, preferred_element_type=F32)

    epilogue(product, extra, o_ref)


def _linear_t(x, wt, *, name, tm, epilogue, extras=(), extra_specs=()):
    m, kdim = x.shape
    n = wt.shape[0]
    assert m % tm == 0, (m, tm)
    tn = _col_tile(kdim, n, tm=tm, **_PLAIN)
    in_specs = [pl.BlockSpec((tn, kdim), lambda i, j: (j, 0)),
                pl.BlockSpec((tm, kdim), lambda i, j: (i, 0))]
    in_specs.extend(extra_specs)
    kernel = functools.partial(_linear_t_kernel, n_extra=len(extras), epilogue=epilogue)
    return pl.pallas_call(
        kernel,
        grid=(m // tm, n // tn),
        in_specs=in_specs,
        out_specs=pl.BlockSpec((tn, tm), lambda i, j: (j, i)),
        out_shape=jax.ShapeDtypeStruct((n, m), BF16),
        compiler_params=_compiler_params(("parallel", "arbitrary")),
        name=name,
    )(wt, x, *extras)


def _ep_t_plain(product, extra, o_ref):
    o_ref[...] = product().astype(o_ref.dtype)


def _ep_t_mla_q(product, extra, o_ref, *, scale):
    cc_ref, ss_ref = extra
    acc = product()
    cc = cc_ref[...]
    ss = ss_ref[...]
    half = LANES // 2
    for hh in range(o_ref.shape[0] // MXU_DIM):
        lo = hh * MXU_DIM
        mid = lo + LANES
        hi = lo + MXU_DIM
        o_ref[lo:mid, :] = (acc[lo:mid, :] * scale).astype(o_ref.dtype)
        x = acc[mid:hi, :]
        xr = jnp.concatenate([x[half:], x[:half]], axis=0)
        o_ref[mid:hi, :] = ((x * cc + xr * ss) * scale).astype(o_ref.dtype)


def _mlp_down_kernel(*refs, nk, final_norm):
    if final_norm:
        x_ref, w_ref, res_ref, g_ref, o_ref = refs
    else:
        x_ref, w_ref, res_ref, o_ref = refs
    k = pl.program_id(1)

    def product():
        return jnp.dot(x_ref[...], w_ref[...], preferred_element_type=F32)

    @pl.when(k == 0)
    def _():
        o_ref[...] = res_ref[...] + product()

    @pl.when(k > 0)
    def _():
        o_ref[...] += product()

    if final_norm:
        @pl.when(k == nk - 1)
        def _():
            _norm_rows_pass(o_ref, g_ref, o_ref)


def _mlp_down(hid, w2, res, *, tm, final_g=None):
    m, kdim = hid.shape
    d = w2.shape[1]
    tk = _tile(kdim, DOWN_K_TILE)
    nk = kdim // tk
    final_norm = final_g is not None
    in_specs = [pl.BlockSpec((tm, tk), lambda i, k: (i, k)),
                pl.BlockSpec((tk, d), lambda i, k: (k, 0)),
                pl.BlockSpec((tm, d), lambda i, k: (i, 0))]
    args = [hid, w2, res]
    if final_norm:
        in_specs.append(pl.BlockSpec((1, d), lambda i, k: (0, 0)))
        args.append(final_g.reshape(1, d).astype(F32))
    return pl.pallas_call(
        functools.partial(_mlp_down_kernel, nk=nk, final_norm=final_norm),
        grid=(m // tm, nk),
        in_specs=in_specs,
        out_specs=pl.BlockSpec((tm, d), lambda i, k: (i, 0)),
        out_shape=jax.ShapeDtypeStruct((m, d), F32),
        compiler_params=_compiler_params(("parallel", "arbitrary")),
        name="mlp_down",
    )(*args)


def _ret_kernel(cd_ref, q_ref, k_ref, v_ref, g_ref, qm_ref, km_ref, vm_ref, gm_ref,
                dmask_ref, qdf_ref, kdf_ref, qdb_ref, kdb_ref,
                o_ref, om_ref, rf_ref, rb_ref, opart_ref, *, n_groups, group):
    h = pl.program_id(1)
    t = pl.program_id(2)
    c = RET_CHUNK
    cd_f = cd_ref[h]
    cd_b = cd_ref[RET_HEADS + h]

    def decayed(x, dec_ref):
        return x * dec_ref[0]

    def wide(ref, sl):
        return jnp.concatenate([ref[0, sl, :], ref[1, sl, :]], axis=1)

    def pad_meta(x):
        return jnp.concatenate([jnp.zeros((c - N_META, x.shape[1]), x.dtype), x], axis=0)

    def chunk_rows(cidx):
        if isinstance(cidx, int):
            return slice(cidx * c, (cidx + 1) * c)
        return pl.ds(pl.multiple_of(cidx * c, c), c)

    def bwd_update(kc, vc):
        return lax.dot_general(decayed(kc, kdb_ref), vc, _TN, preferred_element_type=F32)

    def bwd_out(state, qc, cidx):
        ob = jnp.dot(decayed(qc, qdb_ref), state.astype(BF16), preferred_element_type=F32)
        opart_ref[chunk_rows(cidx), :] = ob

    def fwd_indep(qc, kc, vc):
        s = lax.dot_general(qc, kc, _NT, preferred_element_type=F32) * dmask_ref[0]
        sv = jnp.dot(s.astype(BF16), vc, preferred_element_type=F32)
        upd = lax.dot_general(decayed(kc, kdf_ref), vc, _TN, preferred_element_type=F32)
        return sv, upd

    def fwd_finish(state, indep, qc, gc, cidx):
        sv, upd = indep
        o = sv + jnp.dot(decayed(qc, qdf_ref), state.astype(BF16), preferred_element_type=F32)
        o = o + opart_ref[chunk_rows(cidx), :]
        mu = jnp.mean(o, axis=-1, keepdims=True)
        xc = o - mu
        var = jnp.mean(xc * xc, axis=-1, keepdims=True)
        on = xc * lax.rsqrt(var + NORM_EPS)
        return cd_f * state + upd, on.astype(BF16) * gc

    all_meta = slice(0, N_META)

    def rows_of(cc):
        return slice(cc * c, (cc + 1) * c)

    @pl.when(t == 0)
    def _():
        rb_ref[...] = jnp.zeros_like(rb_ref)

    @pl.when(t < n_groups)
    def _():
        grp = n_groups - 1 - t
        state = rb_ref[...]
        order = list(range(group - 1, -1, -1))
        upd_next = bwd_update(k_ref[rows_of(order[0]), :], wide(v_ref, rows_of(order[0])))
        for n, cc in enumerate(order):
            upd = upd_next
            if n + 1 < group:
                nxt = rows_of(order[n + 1])
                upd_next = bwd_update(k_ref[nxt, :], wide(v_ref, nxt))
            bwd_out(state, q_ref[rows_of(cc), :], 1 + grp * group + cc)
            state = cd_b * state + upd
        rb_ref[...] = state

    @pl.when(t == n_groups - 1)
    def _():
        bwd_out(rb_ref[...], pad_meta(qm_ref[...]), 0)

    @pl.when(t == n_groups)
    def _():
        qm = pad_meta(qm_ref[...])
        indep = fwd_indep(qm, pad_meta(km_ref[...]), pad_meta(wide(vm_ref, all_meta)))
        state, res = fwd_finish(jnp.zeros(rf_ref.shape, F32), indep, qm, pad_meta(wide(gm_ref, all_meta)), 0)
        rf_ref[...] = state
        om_ref[0] = res[c - N_META:, :RET_SLAB]
        om_ref[1] = res[c - N_META:, RET_SLAB:]

    @pl.when(t >= n_groups)
    def _():
        grp = t - n_groups
        state = rf_ref[...]

        def indep_of(cc):
            sl = rows_of(cc)
            return fwd_indep(q_ref[sl, :], k_ref[sl, :], wide(v_ref, sl))

        indep_next = indep_of(0)
        for cc in range(group):
            indep = indep_next
            if cc + 1 < group:
                indep_next = indep_of(cc + 1)
            sl = rows_of(cc)
            state, res = fwd_finish(state, indep, q_ref[sl, :], wide(g_ref, sl), 1 + grp * group + cc)
            o_ref[0, sl, :] = res[:, :RET_SLAB]
            o_ref[1, sl, :] = res[:, RET_SLAB:]
        rf_ref[...] = state


def _retention_tables():
    c = RET_CHUNK
    hh = jnp.arange(RET_HEADS, dtype=F32)
    lg_f = jnp.log(1.0 - 2.0 ** (-5.0 - hh))
    lg_b = jnp.log(1.0 - 2.0 ** (-5.5 - hh))
    i = jnp.arange(c, dtype=F32)
    diff = i[:, None] - i[None, :]
    ad = jnp.abs(diff)[None]
    dmask = (jnp.where(diff[None] >= 0, jnp.exp(lg_f[:, None, None] * ad), 0.0)
             + jnp.where(diff[None] < 0, jnp.exp(lg_b[:, None, None] * ad), 0.0))

    def wide(v):
        return jnp.broadcast_to(v[:, :, None], (RET_HEADS, c, RET_DK)).astype(BF16)

    qd_f = wide(jnp.exp(lg_f[:, None] * (i + 1.0)[None]))
    kd_f = wide(jnp.exp(lg_f[:, None] * (c - 1.0 - i)[None]))
    qd_b = wide(jnp.exp(lg_b[:, None] * (c - i)[None]))
    kd_b = wide(jnp.exp(lg_b[:, None] * i[None]))
    cd = jnp.concatenate([jnp.exp(lg_f * c), jnp.exp(lg_b * c)])
    return cd, dmask, qd_f, kd_f, qd_b, kd_b


def _retention(qkvg_r, qkvg_m, batch, seq, tables):
    cd, dmask, qd_f, kd_f, qd_b, kd_b = tables
    c = RET_CHUNK
    n_real = seq // c
    group = _tile(n_real, RET_GROUP)
    n_groups = n_real // group
    rows = group * c
    v_per = RET_DV // RET_SLAB
    k_off = RET_HEADS
    v_off = 2 * RET_HEADS // v_per
    g_off = v_off + RET_HEADS

    def sweep(t):
        return jnp.where(t < n_groups, n_groups - 1 - t, t - n_groups)

    def fwd_only(t):
        return jnp.maximum(t - n_groups, 0)

    in_specs = [
        pl.BlockSpec((None, rows, RET_SLAB), lambda b, h, t, cd: (h, b * n_groups + sweep(t), 0)),
        pl.BlockSpec((None, rows, RET_SLAB), lambda b, h, t, cd: (k_off + h, b * n_groups + sweep(t), 0)),
        pl.BlockSpec((v_per, rows, RET_SLAB), lambda b, h, t, cd: (v_off + h, b * n_groups + sweep(t), 0)),
        pl.BlockSpec((v_per, rows, RET_SLAB), lambda b, h, t, cd: (g_off + h, b * n_groups + fwd_only(t), 0)),
        pl.BlockSpec((None, N_META, RET_SLAB), lambda b, h, t, cd: (h, b, 0)),
        pl.BlockSpec((None, N_META, RET_SLAB), lambda b, h, t, cd: (k_off + h, b, 0)),
        pl.BlockSpec((v_per, N_META, RET_SLAB), lambda b, h, t, cd: (v_off + h, b, 0)),
        pl.BlockSpec((v_per, N_META, RET_SLAB), lambda b, h, t, cd: (g_off + h, b, 0)),
        pl.BlockSpec((1, c, c), lambda b, h, t, cd: (h, 0, 0)),
        pl.BlockSpec((1, c, RET_DK), lambda b, h, t, cd: (h, 0, 0)),
        pl.BlockSpec((1, c, RET_DK), lambda b, h, t, cd: (h, 0, 0)),
        pl.BlockSpec((1, c, RET_DK), lambda b, h, t, cd: (h, 0, 0)),
        pl.BlockSpec((1, c, RET_DK), lambda b, h, t, cd: (h, 0, 0)),
    ]
    out_specs = [
        pl.BlockSpec((v_per, rows, RET_SLAB), lambda b, h, t, cd: (h, b * n_groups + fwd_only(t), 0)),
        pl.BlockSpec((v_per, N_META, RET_SLAB), lambda b, h, t, cd: (h, b, 0)),
    ]
    out_shape = [
        jax.ShapeDtypeStruct((RET_HEADS * v_per, batch * seq, RET_SLAB), BF16),
        jax.ShapeDtypeStruct((RET_HEADS * v_per, batch * N_META, RET_SLAB), BF16),
    ]
    grid_spec = pltpu.PrefetchScalarGridSpec(
        num_scalar_prefetch=1,
        grid=(batch, RET_HEADS, 2 * n_groups),
        in_specs=in_specs,
        out_specs=out_specs,
        scratch_shapes=[
            pltpu.VMEM((RET_DK, RET_DV), F32),
            pltpu.VMEM((RET_DK, RET_DV), F32),
            pltpu.VMEM(((n_real + 1) * c, RET_DV), F32),
        ],
    )
    kernel = functools.partial(_ret_kernel, n_groups=n_groups, group=group)
    return pl.pallas_call(
        kernel,
        grid_spec=grid_spec,
        out_shape=out_shape,
        compiler_params=_compiler_params(("parallel", "parallel", "arbitrary")),
        name="retention",
    )(cd, qkvg_r, qkvg_r, qkvg_r, qkvg_r, qkvg_m, qkvg_m, qkvg_m, qkvg_m,
      dmask, qd_f, kd_f, qd_b, kd_b)


def _attn_kernel(qt_ref, kn_ref, kr_ref, vt_ref, knm_ref, krm_ref, vm_ref, o_ref, *, tk):
    def values(vt, n):
        return jnp.concatenate([vt, jnp.ones((ONES_ROWS, n), BF16)], axis=0)

    qt = qt_ref[...]

    def scores(kn, kr):
        return jnp.dot(jnp.concatenate([kn, kr], axis=1), qt, preferred_element_type=F32)

    def block(c):
        sl = slice(c * tk, (c + 1) * tk)
        return scores(kn_ref[sl, :], kr_ref[sl, :])

    nk = kn_ref.shape[0] // tk
    s = scores(knm_ref[...], krm_ref[...])
    s_next = block(0)
    m = jnp.max(s, axis=0, keepdims=True)
    p = jnp.exp2(s - m).astype(BF16)
    vm_t = vm_ref[...].astype(F32).T.astype(BF16)
    acc = jnp.dot(values(vm_t, N_META), p, preferred_element_type=F32)
    for c in range(nk):
        s = s_next
        if c + 1 < nk:
            s_next = block(c + 1)
        m_new = jnp.maximum(m, jnp.max(s, axis=0, keepdims=True))
        alpha = jnp.exp2(m - m_new)
        p = jnp.exp2(s - m_new).astype(BF16)
        sl = slice(c * tk, (c + 1) * tk)
        acc = alpha * acc + jnp.dot(values(vt_ref[:, sl], tk), p, preferred_element_type=F32)
        m = m_new
    o_ref[...] = (acc[:MLA_V] / acc[MLA_V:MLA_V + 1]).T.astype(o_ref.dtype)


def _attn_meta_kernel(q_ref, kn_ref, kr_ref, vt_ref, knm_ref, krm_ref, vm_ref, o_ref, *, tk, heads):
    def cols(hh, width):
        return slice(hh * width, (hh + 1) * width)

    qs = [q_ref[:, cols(hh, MXU_DIM)] for hh in range(heads)]

    def scores(q, kn, kr):
        return lax.dot_general(q, jnp.concatenate([kn, kr], axis=1), _NT, preferred_element_type=F32)

    state = []
    for hh in range(heads):
        s = scores(qs[hh], knm_ref[:, cols(hh, MLA_NOPE)], krm_ref[...])
        m0 = jnp.max(s, axis=1, keepdims=True)
        p = jnp.exp2(s - m0)
        state += [m0, jnp.sum(p, axis=1, keepdims=True),
                  jnp.dot(p.astype(BF16), vm_ref[:, cols(hh, MLA_V)], preferred_element_type=F32)]

    def body(c, state):
        sl = pl.ds(pl.multiple_of(c * tk, tk), tk)
        kr = kr_ref[sl, :]
        new = []
        for hh in range(heads):
            m_prev, l_prev, acc = state[3 * hh:3 * hh + 3]
            s = scores(qs[hh], kn_ref[hh, sl, :], kr)
            m_new = jnp.maximum(m_prev, jnp.max(s, axis=1, keepdims=True))
            alpha = jnp.exp2(m_prev - m_new)
            p = jnp.exp2(s - m_new)
            pv = lax.dot_general(p.astype(BF16), vt_ref[cols(hh, MLA_V), sl], _NT,
                                 preferred_element_type=F32)
            new += [m_new, alpha * l_prev + jnp.sum(p, axis=1, keepdims=True), alpha * acc + pv]
        return tuple(new)

    trips = kn_ref.shape[1] // tk
    state = lax.fori_loop(0, trips, body, tuple(state), unroll=math.gcd(trips, ATT_META_UNROLL))
    for hh in range(heads):
        o_ref[hh] = (state[3 * hh + 2] / state[3 * hh + 1]).astype(o_ref.dtype)


def _attention(qt, q_m, kn, kr, vt, kv_m, kr_m, batch, seq):
    tq = _tile(seq, ATT_TQ)
    nq = seq // tq
    o_r = pl.pallas_call(
        functools.partial(_attn_kernel, tk=_tile(seq, ATT_TK)),
        grid=(batch, MLA_HEADS, nq),
        in_specs=[
            pl.BlockSpec((MXU_DIM, tq), lambda b, h, i: (h, b * nq + i)),
            pl.BlockSpec((None, seq, MLA_NOPE), lambda b, h, i: (h, b, 0)),
            pl.BlockSpec((seq, LANES), lambda b, h, i: (b, 0)),
            pl.BlockSpec((MLA_V, seq), lambda b, h, i: (h, b)),
            pl.BlockSpec((N_META, MLA_NOPE), lambda b, h, i: (b, h)),
            pl.BlockSpec((N_META, LANES), lambda b, h, i: (b, 0)),
            pl.BlockSpec((N_META, MLA_V), lambda b, h, i: (b, MLA_HEADS + h)),
        ],
        out_specs=pl.BlockSpec((None, tq, MLA_V), lambda b, h, i: (h, b * nq + i, 0)),
        out_shape=jax.ShapeDtypeStruct((MLA_HEADS, batch * seq, MLA_V), BF16),
        compiler_params=_compiler_params(("parallel", "parallel", "arbitrary")),
        name="attention",
    )(qt, kn, kr, vt, kv_m, kr_m, kv_m)
    if q_m is None:
        return o_r, None
    hg = ATT_META_HEADS
    n_hg = MLA_HEADS // hg
    o_m = pl.pallas_call(
        functools.partial(_attn_meta_kernel, tk=_tile(seq, ATT_META_TK), heads=hg),
        grid=(batch, n_hg),
        in_specs=[
            pl.BlockSpec((N_META, hg * MXU_DIM), lambda b, g: (b, g)),
            pl.BlockSpec((hg, seq, MLA_NOPE), lambda b, g: (g, b, 0)),
            pl.BlockSpec((seq, LANES), lambda b, g: (b, 0)),
            pl.BlockSpec((hg * MLA_V, seq), lambda b, g: (g, b)),
            pl.BlockSpec((N_META, hg * MLA_NOPE), lambda b, g: (b, g)),
            pl.BlockSpec((N_META, LANES), lambda b, g: (b, 0)),
            pl.BlockSpec((N_META, hg * MLA_V), lambda b, g: (b, n_hg + g)),
        ],
        out_specs=pl.BlockSpec((hg, N_META, MLA_V), lambda b, g: (g, b, 0)),
        out_shape=jax.ShapeDtypeStruct((MLA_HEADS, batch * N_META, MLA_V), BF16),
        compiler_params=_compiler_params(("parallel", "arbitrary")),
        name="attention_meta",
    )(q_m, kn, kr, vt, kv_m, kr_m, kv_m)
    return o_r, o_m


def _rope_tables(n, dim):
    inv = 1.0 / (ROPE_BASE ** (jnp.arange(0, dim, 2, dtype=F32) / dim))
    ang = jnp.arange(n, dtype=F32)[:, None] * inv[None, :]
    return jnp.cos(ang), jnp.sin(ang)


def _swap_halves(w):
    half = w.shape[-1] // 2
    return jnp.concatenate([w[..., half:], w[..., :half]], axis=-1)


def _prep_weights(p):
    d = D_MODEL
    out = {}
    out["ret_qkvg"] = [
        jnp.concatenate([p["ret_wq"][j], p["ret_wk"][j] * (RET_DK ** -0.5),
                         p["ret_wv"][j], p["ret_wg"][j]], axis=1).astype(BF16)
        for j in range(p["ret_wq"].shape[0])]
    out["ret_wo"] = [p["ret_wo"][j].astype(BF16) for j in range(p["ret_wo"].shape[0])]
    keys = ("mla_a", "mla_qb", "mla_qb_t", "mla_kvb", "mla_kn", "mla_v_t", "mla_wo")
    for key in keys:
        out[key] = []
    zeros = jnp.zeros((d, LANES - MLA_ROPE), F32)
    for j in range(p["mla_wq_a"].shape[0]):
        wkv_a = p["mla_wkv_a"][j]
        wr = wkv_a[:, MLA_KV_LORA:]
        out["mla_a"].append(jnp.concatenate(
            [p["mla_wq_a"][j], wkv_a[:, :MLA_KV_LORA], wr, zeros, _swap_halves(wr), zeros],
            axis=1).astype(BF16))
        wq_b = p["mla_wq_b"][j].reshape(MLA_Q_LORA, MLA_HEADS, MLA_NOPE + MLA_ROPE)
        rope = wq_b[..., MLA_NOPE:]
        qb = jnp.concatenate([wq_b[..., :MLA_NOPE], rope, _swap_halves(rope)], axis=-1)
        qb = qb.reshape(MLA_Q_LORA, MLA_HEADS * MXU_DIM).astype(BF16)
        out["mla_qb"].append(qb)
        out["mla_qb_t"].append(qb.T)
        wkv_b = p["mla_wkv_b"][j].reshape(MLA_KV_LORA, MLA_HEADS, MLA_NOPE + MLA_V)
        kn = wkv_b[..., :MLA_NOPE].reshape(MLA_KV_LORA, MLA_HEADS * MLA_NOPE).astype(BF16)
        vv = wkv_b[..., MLA_NOPE:].reshape(MLA_KV_LORA, MLA_HEADS * MLA_V).astype(BF16)
        out["mla_kvb"].append(jnp.concatenate([kn, vv], axis=1))
        out["mla_kn"].append(kn)
        out["mla_v_t"].append(vv.T)
        out["mla_wo"].append(p["mla_wo"][j].astype(BF16))
    out["mlp_w1"] = [p["mlp_w1"][i].astype(BF16) for i in range(p["mlp_w1"].shape[0])]
    out["mlp_w2"] = [p["mlp_w2"][i].astype(BF16) for i in range(p["mlp_w2"].shape[0])]
    return out


def _position_tables(batch, seq):
    n = seq + N_META
    cos_r, sin_r = _rope_tables(n, RET_DK)
    cos_m, sin_m = _rope_tables(n, MLA_ROPE)
    pad = jnp.zeros((n, LANES - MLA_ROPE), F32)
    cc = jnp.concatenate([cos_m, cos_m, pad], axis=1)
    ss = jnp.concatenate([-sin_m, sin_m, pad], axis=1)

    def split(tbl):
        return tbl[N_META:], jnp.tile(tbl[:N_META], (batch, 1))

    return {"ret_cos": split(cos_r), "ret_sin": split(sin_r),
            "mla_cc": split(cc), "mla_ss": split(ss),
            "mla_cc_t": cc[N_META:].T, "mla_ss_t": ss[N_META:].T}


def _trunk(x, p, w, ret_tables):
    batch, seq, d = x.shape
    pos = _position_tables(batch, seq)
    h_r = x.reshape(batch * seq, d)
    h_m = jnp.broadcast_to(p["meta_tokens"].astype(F32)[None], (batch, N_META, d)).reshape(batch * N_META, d)
    depth = p["norm1_g"].shape[0]
    q_scale = math.log2(math.e) * (MLA_NOPE + MLA_ROPE) ** -0.5

    def both(fn, last=False):
        return fn(0), (None if last else fn(1))

    tms = (_tile(seq, ROW_TILE), batch * N_META)
    pers = (seq // tms[0], 1)
    n_rows = (batch * seq, batch * N_META)

    for i in range(depth):
        last = i == depth - 1
        j = i // 2
        hs = (h_r, h_m)
        if i % 2 == 0:
            w_in = w["ret_qkvg"][j]
            tn = _col_tile(d, 2 * RET_HEADS * RET_DK, **_NORM_IN)
            ep = [(2 * RET_HEADS * RET_DK // tn, _ep_ret_rope), (RET_HEADS * RET_DV // tn, _ep_slabs),
                  (RET_HEADS * RET_DV // tn, _ep_ret_gate)]

            def qkvg(which):
                tm, per = tms[which], pers[which]
                o_specs, o_shape = _slab_out(w_in.shape[1] // RET_SLAB, n_rows[which], RET_SLAB,
                                             tn // RET_SLAB, tm)
                return _linear(hs[which], w_in, name="ret_qkvg", tm=tm, tn=tn, epilogue=ep,
                               norm_g=p["norm1_g"][i], out_specs=o_specs, out_shape=o_shape,
                               extras=(pos["ret_cos"][which], pos["ret_sin"][which]),
                               extra_specs=(_row_spec(tm, LANES, per), _row_spec(tm, LANES, per)))[0]

            qkvg_r, qkvg_m = both(qkvg)
            mix_r, mix_m = _retention(qkvg_r, qkvg_m, batch, seq, ret_tables)
            w_out = w["ret_wo"][j]
        else:
            w_a = w["mla_a"][j]

            def stage_a(which):
                tm, per = tms[which], pers[which]
                one = lambda width: pl.BlockSpec((1, width), lambda i_, j_: (0, 0))
                return _linear(hs[which], w_a, name="mla_a", tm=tm, tn=w_a.shape[1], epilogue=_ep_mla_a,
                               out_widths=[MLA_Q_LORA, MLA_KV_LORA, LANES], out_dtypes=[BF16] * 3,
                               out_col_tiles=[MLA_Q_LORA, MLA_KV_LORA, LANES],
                               norm_g=p["norm1_g"][i],
                               extras=(p["mla_q_norm"][j].reshape(1, -1), p["mla_kv_norm"][j].reshape(1, -1),
                                       pos["mla_cc"][which], pos["mla_ss"][which]),
                               extra_specs=(one(MLA_Q_LORA), one(MLA_KV_LORA),
                                            _row_spec(tm, LANES, per), _row_spec(tm, LANES, per)))

            (cq_r, ckv_r, kr_r), (cq_m, ckv_m, kr_m) = both(stage_a)
            tm_r = tms[0]
            tm_t = _tile(seq, TRANSPOSED_ROW_TILE)
            per_t = seq // tm_t
            t_spec = pl.BlockSpec((LANES, tm_t), lambda i_, j_: (0, i_ % per_t))
            qt_r = _linear_t(cq_r, w["mla_qb_t"][j], name="mla_qb_t", tm=tm_t,
                             epilogue=functools.partial(_ep_t_mla_q, scale=q_scale),
                             extras=(pos["mla_cc_t"], pos["mla_ss_t"]), extra_specs=(t_spec, t_spec))
            tn_kn = _col_tile(MLA_KV_LORA, MLA_HEADS * MLA_NOPE, **_PLAIN)
            kn_specs, kn_shape = _slab_out(MLA_HEADS, n_rows[0], MLA_NOPE, tn_kn // MLA_NOPE, tm_r)
            kn_r = _linear(ckv_r, w["mla_kn"][j], name="mla_kn", tm=tm_r, tn=tn_kn, epilogue=_ep_slabs,
                           out_specs=kn_specs, out_shape=kn_shape)[0]
            vt_r = _linear_t(ckv_r, w["mla_v_t"][j], name="mla_v_t", tm=tm_t, epilogue=_ep_t_plain)
            tm_m = tms[1]
            kv_m = _linear(ckv_m, w["mla_kvb"][j], name="mla_kvb", tm=tm_m, tn=MAX_COL_TILE,
                           epilogue=_ep_plain, out_widths=[MLA_HEADS * (MLA_NOPE + MLA_V)],
                           out_dtypes=[BF16])[0]
            q_m = None
            if not last:
                m_spec = _row_spec(tm_m, LANES, 1)
                q_m = _linear(cq_m, w["mla_qb"][j], name="mla_qb", tm=tm_m, tn=MAX_COL_TILE,
                              epilogue=functools.partial(_ep_mla_q, scale=q_scale),
                              out_widths=[MLA_HEADS * MXU_DIM], out_dtypes=[BF16],
                              extras=(pos["mla_cc"][1], pos["mla_ss"][1]), extra_specs=(m_spec, m_spec))[0]
            mix_r, mix_m = _attention(qt_r, q_m, kn_r, kr_r, vt_r, kv_m, kr_m, batch, seq)
            w_out = w["mla_wo"][j]

        mixes = (mix_r, mix_m)
        tm_out = _tile(seq, ROW_TILE // 2) if w_out.shape[0] <= OUT_RESIDENT_K else tms[0]
        tn_out = _col_tile(w_out.shape[0], d, tm=tm_out, **_RES_OUT)

        def proj_out(which):
            tm = tm_out if which == 0 else tms[1]
            return _linear(mixes[which], w_out, name="mix_out", tm=tm, tn=tn_out, epilogue=_ep_residual,
                           out_widths=[d], out_dtypes=[F32], extras=(hs[which],),
                           extra_specs=(pl.BlockSpec((tm, tn_out), lambda i_, j_: (i_, j_)),))[0]

        h_r, h_m = both(proj_out, last)
        hs = (h_r, h_m)
        tn_up = _col_tile(d, D_FF, **_NORM_IN)

        def mlp(which):
            tm = tms[which]
            hid = _linear(hs[which], w["mlp_w1"][i], name="mlp_up", tm=tm, tn=tn_up, epilogue=_ep_relu2,
                          out_widths=[D_FF], out_dtypes=[BF16], norm_g=p["norm2_g"][i])[0]
            return _mlp_down(hid, w["mlp_w2"][i], hs[which], tm=tm,
                             final_g=p["final_norm"] if last else None)

        h_r, h_m = both(mlp, last)

    return h_r.reshape(batch, seq, d)


def kernel(x_prompt, x_sample, meta_tokens, norm1_g, norm2_g, mlp_w1, mlp_w2, ret_wq, ret_wk, ret_wv, ret_wg, ret_wo, mla_wq_a, mla_q_norm, mla_wq_b, mla_wkv_a, mla_kv_norm, mla_wkv_b, mla_wo, final_norm):
    p = dict(meta_tokens=meta_tokens, norm1_g=norm1_g, norm2_g=norm2_g, mlp_w1=mlp_w1, mlp_w2=mlp_w2,
             ret_wq=ret_wq, ret_wk=ret_wk, ret_wv=ret_wv, ret_wg=ret_wg, ret_wo=ret_wo,
             mla_wq_a=mla_wq_a, mla_q_norm=mla_q_norm, mla_wq_b=mla_wq_b, mla_wkv_a=mla_wkv_a,
             mla_kv_norm=mla_kv_norm, mla_wkv_b=mla_wkv_b, mla_wo=mla_wo, final_norm=final_norm)
    w = _prep_weights(p)
    ret_tables = _retention_tables()
    return (_trunk(x_prompt, p, w, ret_tables), _trunk(x_sample, p, w, ret_tables))
```

```python
import functools
import math

import jax
import jax.numpy as jnp
from jax import lax
from jax.experimental import pallas as pl
from jax.experimental.pallas import tpu as pltpu

F32 = jnp.float32
BF16 = jnp.bfloat16

D_MODEL = 2048
N_META = 16
D_FF = 4 * D_MODEL
NORM_EPS = 1e-6
ROPE_BASE = 10000.0
RET_HEADS = 8
RET_DK = 256
RET_DV = 512
MLA_HEADS = 16
MLA_Q_LORA = 512
MLA_KV_LORA = 512
MLA_NOPE = 128
MLA_ROPE = 64
MLA_V = 128

LANES = 128
MXU_DIM = 256
VMEM_LIMIT_BYTES = 56 * 1024 * 1024
VMEM_TILE_BUDGET = 48 * 1024 * 1024

ROW_TILE = 1024
TRANSPOSED_ROW_TILE = 2048
STEP_MACS = 2 ** 32
MAX_COL_TILE = 2048
DOWN_K_TILE = 2048
OUT_RESIDENT_K = 2048
NORM_ROWS = 64
NORM_UNROLL = 4
NORM_SUBTILES = 4
RET_CHUNK = 256
RET_GROUP = 8
RET_SLAB = RET_DK
ATT_TQ = 1024
ATT_TK = 512
ATT_META_TK = 2048
ATT_META_HEADS = 4
ATT_META_UNROLL = 2
ONES_ROWS = 16

_NT = (((1,), (1,)), ((), ()))
_TN = (((0,), (0,)), ((), ()))


def _tile(n, pref):
    t = min(n, pref)
    assert n % t == 0, (n, pref)
    return t


def _col_tile(kdim, n, *, x_bytes, has_norm, out_bytes, tm=ROW_TILE):
    def footprint(tn):
        return (2 * tm * kdim * x_bytes + (2 * tm * kdim if has_norm else 0)
                + 2 * kdim * tn * 2 + 2 * tm * tn * out_bytes)

    tn = min(n, MAX_COL_TILE)
    while tn > MXU_DIM and (tm * kdim * tn > STEP_MACS or footprint(tn) > VMEM_TILE_BUDGET):
        tn //= 2
    assert n % tn == 0, (n, tn)
    return tn


_NORM_IN = dict(x_bytes=4, has_norm=True, out_bytes=2)
_RES_OUT = dict(x_bytes=2, has_norm=False, out_bytes=8)
_PLAIN = dict(x_bytes=2, has_norm=False, out_bytes=2)


def _compiler_params(semantics):
    return pltpu.CompilerParams(dimension_semantics=semantics,
                                vmem_limit_bytes=VMEM_LIMIT_BYTES)


def _rms_rows(x, g):
    ms = jnp.mean(x * x, axis=-1, keepdims=True)
    return x * lax.rsqrt(ms + NORM_EPS) * g


def _norm_rows_pass(x_ref, g_ref, o_ref):
    rows = x_ref.shape[0]
    step = min(rows, NORM_ROWS)

    def body(r, carry):
        sl = pl.ds(pl.multiple_of(r * step, step), step)
        o_ref[sl, :] = _rms_rows(x_ref[sl, :], g_ref[...]).astype(o_ref.dtype)
        return carry

    trips = rows // step
    lax.fori_loop(0, trips, body, 0, unroll=math.gcd(trips, NORM_UNROLL))


def _linear_kernel(*refs, has_norm, n_slabs, n_extra, n_out, epilogue):
    x_ref = refs[0]
    pos = 1
    g_ref = None
    if has_norm:
        g_ref = refs[pos]
        pos += 1
    w_ref = refs[pos]
    pos += 1
    extra = refs[pos:pos + n_extra]
    pos += n_extra
    outs = refs[pos:pos + n_out]
    pos += n_out
    j = pl.program_id(1)
    a_ref = refs[pos] if has_norm else None
    tm = x_ref.shape[-2]

    def product():
        if has_norm:
            lhs = a_ref[...]
        elif n_slabs:
            lhs = jnp.concatenate([x_ref[s] for s in range(n_slabs)], axis=1)
        else:
            lhs = x_ref[...]
        return jnp.dot(lhs, w_ref[...], preferred_element_type=F32)

    def rows_of(ref, rows):
        if len(ref.shape) == 3:
            return ref.at[:, rows] if ref.shape[1] == tm else ref
        return ref.at[rows] if ref.shape[0] == tm else ref

    def first_step(fn):
        sub = tm // NORM_SUBTILES if tm % (NORM_SUBTILES * NORM_ROWS) == 0 else tm
        for r in range(tm // sub):
            rows = pl.ds(r * sub, sub)
            _norm_rows_pass(x_ref.at[rows], g_ref, a_ref.at[rows])

            def sub_product(rows=rows):
                return jnp.dot(a_ref[rows, :], w_ref[...], preferred_element_type=F32)

            fn(sub_product, [rows_of(e, rows) for e in extra], [rows_of(o, rows) for o in outs])

    if len(epilogue) == 1 and not has_norm:
        epilogue[0][1](product, extra, outs)
        return
    lo = 0
    for seg, (count, fn) in enumerate(epilogue):
        hi = lo + count
        if has_norm and seg == 0:
            pl.when(j == 0)(functools.partial(first_step, fn))
            lo = 1
        if hi > lo:
            pl.when(jnp.logical_and(j >= lo, j < hi))(functools.partial(fn, product, extra, outs))
        lo = hi


def _linear(x, w, *, name, tm, tn, epilogue, out_widths=None, out_dtypes=None, norm_g=None,
            extras=(), extra_specs=(), out_col_tiles=None, out_specs=None, out_shape=None):
    kdim, n = w.shape
    n_slabs = x.shape[0] if x.ndim == 3 else 0
    m = x.shape[-2]
    assert m % tm == 0 and n % tn == 0, (x.shape, w.shape, tm, tn)
    has_norm = norm_g is not None

    if n_slabs:
        assert n_slabs * x.shape[2] == kdim
        in_specs = [pl.BlockSpec((n_slabs, tm, x.shape[2]), lambda i, j: (0, i, 0))]
    else:
        assert x.shape[1] == kdim
        in_specs = [pl.BlockSpec((tm, kdim), lambda i, j: (i, 0))]
    args = [x]
    if has_norm:
        in_specs.append(pl.BlockSpec((1, kdim), lambda i, j: (0, 0)))
        args.append(norm_g.reshape(1, kdim).astype(F32))
    in_specs.append(pl.BlockSpec((kdim, tn), lambda i, j: (0, j)))
    args.append(w)
    in_specs.extend(extra_specs)
    args.extend(extras)

    if out_specs is None:
        if out_col_tiles is None:
            out_col_tiles = [tn] * len(out_widths)
        out_specs = [pl.BlockSpec((tm, ct), lambda i, j: (i, j)) for ct in out_col_tiles]
        out_shape = [jax.ShapeDtypeStruct((m, wd), dt) for wd, dt in zip(out_widths, out_dtypes)]
    scratch = [pltpu.VMEM((tm, kdim), BF16)] if has_norm else []
    if callable(epilogue):
        epilogue = [(n // tn, epilogue)]
    assert sum(count for count, _ in epilogue) == n // tn

    kernel = functools.partial(_linear_kernel, has_norm=has_norm, n_slabs=n_slabs,
                               n_extra=len(extras), n_out=len(out_specs), epilogue=epilogue)
    return pl.pallas_call(
        kernel,
        grid=(m // tm, n // tn),
        in_specs=in_specs,
        out_specs=out_specs,
        out_shape=out_shape,
        scratch_shapes=scratch,
        compiler_params=_compiler_params(("parallel", "arbitrary")),
        name=name,
    )(*args)


def _ep_plain(product, extra, outs):
    outs[0][...] = product().astype(outs[0].dtype)


def _ep_slabs(product, extra, outs):
    o_ref = outs[0]
    acc = product()
    width = o_ref.shape[2]
    for s in range(o_ref.shape[0]):
        o_ref[s] = acc[:, s * width:(s + 1) * width].astype(o_ref.dtype)


def _ep_relu2(product, extra, outs):
    r = jnp.maximum(product(), 0.0)
    outs[0][...] = (r * r).astype(outs[0].dtype)


def _ep_residual(product, extra, outs):
    outs[0][...] = extra[0][...] + product()


def _silu(x):
    return x * (1.0 / (1.0 + jnp.exp(-x)))


def _ep_ret_rope(product, extra, outs):
    cos_ref, sin_ref = extra
    o_ref = outs[0]
    half = RET_DK // 2
    acc = product()
    c = cos_ref[...]
    s = sin_ref[...]
    for hh in range(o_ref.shape[0]):
        lo = hh * RET_DK
        x1 = acc[:, lo:lo + half]
        x2 = acc[:, lo + half:lo + RET_DK]
        o_ref[hh, :, :half] = (x1 * c - x2 * s).astype(o_ref.dtype)
        o_ref[hh, :, half:] = (x1 * s + x2 * c).astype(o_ref.dtype)


def _ep_ret_gate(product, extra, outs):
    _ep_slabs(lambda: _silu(product()), extra, outs)


def _ep_mla_a(product, extra, outs):
    qn_ref, kvn_ref, cc_ref, ss_ref = extra
    cq_ref, ckv_ref, kr_ref = outs
    acc = product()
    a0 = MLA_Q_LORA
    a1 = a0 + MLA_KV_LORA
    a2 = a1 + LANES
    cq_ref[...] = _rms_rows(acc[:, :a0], qn_ref[...]).astype(cq_ref.dtype)
    ckv_ref[...] = _rms_rows(acc[:, a0:a1], kvn_ref[...]).astype(ckv_ref.dtype)
    kr = acc[:, a1:a2] * cc_ref[...] + acc[:, a2:] * ss_ref[...]
    kr_ref[...] = kr.astype(kr_ref.dtype)


def _ep_mla_q(product, extra, outs, *, scale):
    cc_ref, ss_ref = extra
    o_ref = outs[0]
    acc = product()
    cc = cc_ref[...]
    ss = ss_ref[...]
    for hh in range(o_ref.shape[1] // MXU_DIM):
        lo = hh * MXU_DIM
        mid = lo + LANES
        hi = lo + MXU_DIM
        o_ref[:, lo:mid] = (acc[:, lo:mid] * scale).astype(o_ref.dtype)
        x = acc[:, mid:hi]
        xr = pltpu.roll(x, LANES // 2, 1)
        o_ref[:, mid:hi] = ((x * cc + xr * ss) * scale).astype(o_ref.dtype)


def _row_spec(tm, width, period):
    return pl.BlockSpec((tm, width), lambda i, j: (i % period, 0))


def _slab_out(n_slabs, rows, width, tile_slabs, tm):
    spec = pl.BlockSpec((tile_slabs, tm, width), lambda i, j: (j, i, 0))
    return [spec], [jax.ShapeDtypeStruct((n_slabs, rows, width), BF16)]


def _linear_t_kernel(*refs, n_extra, epilogue):
    wt_ref, x_ref = refs[0], refs[1]
    extra = refs[2:2 + n_extra]
    o_ref = refs[2 + n_extra]

    def product():
        return lax.dot_general(wt_ref[...], x_ref[...], _NT, preferred_element_type=F32)

    epilogue(product, extra, o_ref)


def _linear_t(x, wt, *, name, tm, epilogue, extras=(), extra_specs=()):
    m, kdim = x.shape
    n = wt.shape[0]
    assert m % tm == 0, (m, tm)
    tn = _col_tile(kdim, n, tm=tm, **_PLAIN)
    in_specs = [pl.BlockSpec((tn, kdim), lambda i, j: (j, 0)),
                pl.BlockSpec((tm, kdim), lambda i, j: (i, 0))]
    in_specs.extend(extra_specs)
    kernel = functools.partial(_linear_t_kernel, n_extra=len(extras), epilogue=epilogue)
    return pl.pallas_call(
        kernel,
        grid=(m // tm, n // tn),
        in_specs=in_specs,
        out_specs=pl.BlockSpec((tn, tm), lambda i, j: (j, i)),
        out_shape=jax.ShapeDtypeStruct((n, m), BF16),
        compiler_params=_compiler_params(("parallel", "arbitrary")),
        name=name,
    )(wt, x, *extras)


def _ep_t_plain(product, extra, o_ref):
    o_ref[...] = product().astype(o_ref.dtype)


def _ep_t_mla_q(product, extra, o_ref, *, scale):
    cc_ref, ss_ref = extra
    acc = product()
    cc = cc_ref[...]
    ss = ss_ref[...]
    half = LANES // 2
    for hh in range(o_ref.shape[0] // MXU_DIM):
        lo = hh * MXU_DIM
        mid = lo + LANES
        hi = lo + MXU_DIM
        o_ref[lo:mid, :] = (acc[lo:mid, :] * scale).astype(o_ref.dtype)
        x = acc[mid:hi, :]
        xr = jnp.concatenate([x[half:], x[:half]], axis=0)
        o_ref[mid:hi, :] = ((x * cc + xr * ss) * scale).astype(o_ref.dtype)


def _mlp_down_kernel(*refs, nk, final_norm):
    if final_norm:
        x_ref, w_ref, res_ref, g_ref, o_ref = refs
    else:
        x_ref, w_ref, res_ref, o_ref = refs
    k = pl.program_id(1)

    def product():
        return jnp.dot(x_ref[...], w_ref[...], preferred_element_type=F32)

    @pl.when(k == 0)
    def _():
        o_ref[...] = res_ref[...] + product()

    @pl.when(k > 0)
    def _():
        o_ref[...] += product()

    if final_norm:
        @pl.when(k == nk - 1)
        def _():
            _norm_rows_pass(o_ref, g_ref, o_ref)


def _mlp_down(hid, w2, res, *, tm, final_g=None):
    m, kdim = hid.shape
    d = w2.shape[1]
    tk = _tile(kdim, DOWN_K_TILE)
    nk = kdim // tk
    final_norm = final_g is not None
    in_specs = [pl.BlockSpec((tm, tk), lambda i, k: (i, k)),
                pl.BlockSpec((tk, d), lambda i, k: (k, 0)),
                pl.BlockSpec((tm, d), lambda i, k: (i, 0), pipeline_mode=pl.Buffered(1))]
    args = [hid, w2, res]
    if final_norm:
        in_specs.append(pl.BlockSpec((1, d), lambda i, k: (0, 0)))
        args.append(final_g.reshape(1, d).astype(F32))
    return pl.pallas_call(
        functools.partial(_mlp_down_kernel, nk=nk, final_norm=final_norm),
        grid=(m // tm, nk),
        in_specs=in_specs,
        out_specs=pl.BlockSpec((tm, d), lambda i, k: (i, 0)),
        out_shape=jax.ShapeDtypeStruct((m, d), F32),
        compiler_params=_compiler_params(("parallel", "arbitrary")),
        name="mlp_down",
    )(*args)


def _ret_kernel(cd_ref, q_ref, k_ref, v_ref, g_ref, qm_ref, km_ref, vm_ref, gm_ref,
                dmask_ref, qdf_ref, kdf_ref, qdb_ref, kdb_ref,
                o_ref, om_ref, rf_ref, rb_ref, opart_ref, *, n_groups, group):
    h = pl.program_id(1)
    t = pl.program_id(2)
    c = RET_CHUNK
    cd_f = cd_ref[h]
    cd_b = cd_ref[RET_HEADS + h]

    def decayed(x, dec_ref):
        return x * dec_ref[0]

    def wide(ref, sl):
        return jnp.concatenate([ref[0, sl, :], ref[1, sl, :]], axis=1)

    def pad_meta(x):
        return jnp.concatenate([jnp.zeros((c - N_META, x.shape[1]), x.dtype), x], axis=0)

    def chunk_rows(cidx):
        if isinstance(cidx, int):
            return slice(cidx * c, (cidx + 1) * c)
        return pl.ds(pl.multiple_of(cidx * c, c), c)

    def bwd_update(kc, vc):
        return lax.dot_general(decayed(kc, kdb_ref), vc, _TN, preferred_element_type=F32)

    def bwd_out(state, qc, cidx):
        ob = jnp.dot(decayed(qc, qdb_ref), state.astype(BF16), preferred_element_type=F32)
        opart_ref[chunk_rows(cidx), :] = ob

    def fwd_indep(qc, kc, vc):
        s = lax.dot_general(qc, kc, _NT, preferred_element_type=F32) * dmask_ref[0]
        sv = jnp.dot(s.astype(BF16), vc, preferred_element_type=F32)
        upd = lax.dot_general(decayed(kc, kdf_ref), vc, _TN, preferred_element_type=F32)
        return sv, upd

    def fwd_finish(state, indep, qc, gc, cidx):
        sv, upd = indep
        o = sv + jnp.dot(decayed(qc, qdf_ref), state.astype(BF16), preferred_element_type=F32)
        o = o + opart_ref[chunk_rows(cidx), :]
        mu = jnp.mean(o, axis=-1, keepdims=True)
        xc = o - mu
        var = jnp.mean(xc * xc, axis=-1, keepdims=True)
        on = xc * lax.rsqrt(var + NORM_EPS)
        return cd_f * state + upd, on.astype(BF16) * gc

    all_meta = slice(0, N_META)

    def rows_of(cc):
        return slice(cc * c, (cc + 1) * c)

    @pl.when(t == 0)
    def _():
        rb_ref[...] = jnp.zeros_like(rb_ref)

    @pl.when(t < n_groups)
    def _():
        grp = n_groups - 1 - t
        state = rb_ref[...]
        order = list(range(group - 1, -1, -1))
        upd_next = bwd_update(k_ref[rows_of(order[0]), :], wide(v_ref, rows_of(order[0])))
        for n, cc in enumerate(order):
            upd = upd_next
            if n + 1 < group:
                nxt = rows_of(order[n + 1])
                upd_next = bwd_update(k_ref[nxt, :], wide(v_ref, nxt))
            bwd_out(state, q_ref[rows_of(cc), :], 1 + grp * group + cc)
            state = cd_b * state + upd
        rb_ref[...] = state

    @pl.when(t == n_groups - 1)
    def _():
        bwd_out(rb_ref[...], pad_meta(qm_ref[...]), 0)

    @pl.when(t == n_groups)
    def _():
        qm = pad_meta(qm_ref[...])
        indep = fwd_indep(qm, pad_meta(km_ref[...]), pad_meta(wide(vm_ref, all_meta)))
        state, res = fwd_finish(jnp.zeros(rf_ref.shape, F32), indep, qm, pad_meta(wide(gm_ref, all_meta)), 0)
        rf_ref[...] = state
        om_ref[0] = res[c - N_META:, :RET_SLAB]
        om_ref[1] = res[c - N_META:, RET_SLAB:]

    @pl.when(t >= n_groups)
    def _():
        grp = t - n_groups
        state = rf_ref[...]

        def indep_of(cc):
            sl = rows_of(cc)
            return fwd_indep(q_ref[sl, :], k_ref[sl, :], wide(v_ref, sl))

        indep_next = indep_of(0)
        for cc in range(group):
            indep = indep_next
            if cc + 1 < group:
                indep_next = indep_of(cc + 1)
            sl = rows_of(cc)
            state, res = fwd_finish(state, indep, q_ref[sl, :], wide(g_ref, sl), 1 + grp * group + cc)
            o_ref[0, sl, :] = res[:, :RET_SLAB]
            o_ref[1, sl, :] = res[:, RET_SLAB:]
        rf_ref[...] = state


def _retention_tables():
    c = RET_CHUNK
    hh = jnp.arange(RET_HEADS, dtype=F32)
    lg_f = jnp.log(1.0 - 2.0 ** (-5.0 - hh))
    lg_b = jnp.log(1.0 - 2.0 ** (-5.5 - hh))
    i = jnp.arange(c, dtype=F32)
    diff = i[:, None] - i[None, :]
    ad = jnp.abs(diff)[None]
    dmask = (jnp.where(diff[None] >= 0, jnp.exp(lg_f[:, None, None] * ad), 0.0)
             + jnp.where(diff[None] < 0, jnp.exp(lg_b[:, None, None] * ad), 0.0))

    def wide(v):
        return jnp.broadcast_to(v[:, :, None], (RET_HEADS, c, RET_DK)).astype(BF16)

    qd_f = wide(jnp.exp(lg_f[:, None] * (i + 1.0)[None]))
    kd_f = wide(jnp.exp(lg_f[:, None] * (c - 1.0 - i)[None]))
    qd_b = wide(jnp.exp(lg_b[:, None] * (c - i)[None]))
    kd_b = wide(jnp.exp(lg_b[:, None] * i[None]))
    cd = jnp.concatenate([jnp.exp(lg_f * c), jnp.exp(lg_b * c)])
    return cd, dmask, qd_f, kd_f, qd_b, kd_b


def _retention(qkvg_r, qkvg_m, batch, seq, tables):
    cd, dmask, qd_f, kd_f, qd_b, kd_b = tables
    c = RET_CHUNK
    n_real = seq // c
    group = _tile(n_real, RET_GROUP)
    n_groups = n_real // group
    rows = group * c
    v_per = RET_DV // RET_SLAB
    k_off = RET_HEADS
    v_off = 2 * RET_HEADS // v_per
    g_off = v_off + RET_HEADS

    def sweep(t):
        return jnp.where(t < n_groups, n_groups - 1 - t, t - n_groups)

    def fwd_only(t):
        return jnp.maximum(t - n_groups, 0)

    in_specs = [
        pl.BlockSpec((None, rows, RET_SLAB), lambda b, h, t, cd: (h, b * n_groups + sweep(t), 0)),
        pl.BlockSpec((None, rows, RET_SLAB), lambda b, h, t, cd: (k_off + h, b * n_groups + sweep(t), 0)),
        pl.BlockSpec((v_per, rows, RET_SLAB), lambda b, h, t, cd: (v_off + h, b * n_groups + sweep(t), 0)),
        pl.BlockSpec((v_per, rows, RET_SLAB), lambda b, h, t, cd: (g_off + h, b * n_groups + fwd_only(t), 0)),
        pl.BlockSpec((None, N_META, RET_SLAB), lambda b, h, t, cd: (h, b, 0)),
        pl.BlockSpec((None, N_META, RET_SLAB), lambda b, h, t, cd: (k_off + h, b, 0)),
        pl.BlockSpec((v_per, N_META, RET_SLAB), lambda b, h, t, cd: (v_off + h, b, 0)),
        pl.BlockSpec((v_per, N_META, RET_SLAB), lambda b, h, t, cd: (g_off + h, b, 0)),
        pl.BlockSpec((1, c, c), lambda b, h, t, cd: (h, 0, 0)),
        pl.BlockSpec((1, c, RET_DK), lambda b, h, t, cd: (h, 0, 0)),
        pl.BlockSpec((1, c, RET_DK), lambda b, h, t, cd: (h, 0, 0)),
        pl.BlockSpec((1, c, RET_DK), lambda b, h, t, cd: (h, 0, 0)),
        pl.BlockSpec((1, c, RET_DK), lambda b, h, t, cd: (h, 0, 0)),
    ]
    out_specs = [
        pl.BlockSpec((v_per, rows, RET_SLAB), lambda b, h, t, cd: (h, b * n_groups + fwd_only(t), 0)),
        pl.BlockSpec((v_per, N_META, RET_SLAB), lambda b, h, t, cd: (h, b, 0)),
    ]
    out_shape = [
        jax.ShapeDtypeStruct((RET_HEADS * v_per, batch * seq, RET_SLAB), BF16),
        jax.ShapeDtypeStruct((RET_HEADS * v_per, batch * N_META, RET_SLAB), BF16),
    ]
    grid_spec = pltpu.PrefetchScalarGridSpec(
        num_scalar_prefetch=1,
        grid=(batch, RET_HEADS, 2 * n_groups),
        in_specs=in_specs,
        out_specs=out_specs,
        scratch_shapes=[
            pltpu.VMEM((RET_DK, RET_DV), F32),
            pltpu.VMEM((RET_DK, RET_DV), F32),
            pltpu.VMEM(((n_real + 1) * c, RET_DV), F32),
        ],
    )
    kernel = functools.partial(_ret_kernel, n_groups=n_groups, group=group)
    return pl.pallas_call(
        kernel,
        grid_spec=grid_spec,
        out_shape=out_shape,
        compiler_params=_compiler_params(("parallel", "parallel", "arbitrary")),
        name="retention",
    )(cd, qkvg_r, qkvg_r, qkvg_r, qkvg_r, qkvg_m, qkvg_m, qkvg_m, qkvg_m,
      dmask, qd_f, kd_f, qd_b, kd_b)


def _attn_kernel(qt_ref, kn_ref, kr_ref, vt_ref, knm_ref, krm_ref, vm_ref, o_ref, *, tk):
    def values(vt, n):
        return jnp.concatenate([vt, jnp.ones((ONES_ROWS, n), BF16)], axis=0)

    qt = qt_ref[...]

    def scores(kn, kr):
        return jnp.dot(jnp.concatenate([kn, kr], axis=1), qt, preferred_element_type=F32)

    def block(c):
        sl = slice(c * tk, (c + 1) * tk)
        return scores(kn_ref[sl, :], kr_ref[sl, :])

    nk = kn_ref.shape[0] // tk
    s = scores(knm_ref[...], krm_ref[...])
    s_next = block(0)
    m = jnp.max(s, axis=0, keepdims=True)
    p = jnp.exp2(s - m).astype(BF16)
    vm_t = vm_ref[...].astype(F32).T.astype(BF16)
    acc = jnp.dot(values(vm_t, N_META), p, preferred_element_type=F32)
    for c in range(nk):
        s = s_next
        if c + 1 < nk:
            s_next = block(c + 1)
        m_new = jnp.maximum(m, jnp.max(s, axis=0, keepdims=True))
        alpha = jnp.exp2(m - m_new)
        p = jnp.exp2(s - m_new).astype(BF16)
        sl = slice(c * tk, (c + 1) * tk)
        acc = alpha * acc + jnp.dot(values(vt_ref[:, sl], tk), p, preferred_element_type=F32)
        m = m_new
    o_ref[...] = (acc[:MLA_V] / acc[MLA_V:MLA_V + 1]).T.astype(o_ref.dtype)


def _attn_meta_kernel(q_ref, kn_ref, kr_ref, vt_ref, knm_ref, krm_ref, vm_ref, o_ref, *, tk, heads):
    def cols(hh, width):
        return slice(hh * width, (hh + 1) * width)

    qs = [q_ref[:, cols(hh, MXU_DIM)] for hh in range(heads)]

    def scores(q, kn, kr):
        return lax.dot_general(q, jnp.concatenate([kn, kr], axis=1), _NT, preferred_element_type=F32)

    state = []
    for hh in range(heads):
        s = scores(qs[hh], knm_ref[:, cols(hh, MLA_NOPE)], krm_ref[...])
        m0 = jnp.max(s, axis=1, keepdims=True)
        p = jnp.exp2(s - m0)
        state += [m0, jnp.sum(p, axis=1, keepdims=True),
                  jnp.dot(p.astype(BF16), vm_ref[:, cols(hh, MLA_V)], preferred_element_type=F32)]

    def body(c, state):
        sl = pl.ds(pl.multiple_of(c * tk, tk), tk)
        kr = kr_ref[sl, :]
        new = []
        for hh in range(heads):
            m_prev, l_prev, acc = state[3 * hh:3 * hh + 3]
            s = scores(qs[hh], kn_ref[hh, sl, :], kr)
            m_new = jnp.maximum(m_prev, jnp.max(s, axis=1, keepdims=True))
            alpha = jnp.exp2(m_prev - m_new)
            p = jnp.exp2(s - m_new)
            pv = lax.dot_general(p.astype(BF16), vt_ref[cols(hh, MLA_V), sl], _NT,
                                 preferred_element_type=F32)
            new += [m_new, alpha * l_prev + jnp.sum(p, axis=1, keepdims=True), alpha * acc + pv]
        return tuple(new)

    trips = kn_ref.shape[1] // tk
    state = lax.fori_loop(0, trips, body, tuple(state), unroll=math.gcd(trips, ATT_META_UNROLL))
    for hh in range(heads):
        o_ref[hh] = (state[3 * hh + 2] / state[3 * hh + 1]).astype(o_ref.dtype)


def _attention(qt, q_m, kn, kr, vt, kv_m, kr_m, batch, seq):
    tq = _tile(seq, ATT_TQ)
    nq = seq // tq
    o_r = pl.pallas_call(
        functools.partial(_attn_kernel, tk=_tile(seq, ATT_TK)),
        grid=(batch, MLA_HEADS, nq),
        in_specs=[
            pl.BlockSpec((MXU_DIM, tq), lambda b, h, i: (h, b * nq + i)),
            pl.BlockSpec((None, seq, MLA_NOPE), lambda b, h, i: (h, b, 0)),
            pl.BlockSpec((seq, LANES), lambda b, h, i: (b, 0)),
            pl.BlockSpec((MLA_V, seq), lambda b, h, i: (h, b)),
            pl.BlockSpec((N_META, MLA_NOPE), lambda b, h, i: (b, h)),
            pl.BlockSpec((N_META, LANES), lambda b, h, i: (b, 0)),
            pl.BlockSpec((N_META, MLA_V), lambda b, h, i: (b, MLA_HEADS + h)),
        ],
        out_specs=pl.BlockSpec((None, tq, MLA_V), lambda b, h, i: (h, b * nq + i, 0)),
        out_shape=jax.ShapeDtypeStruct((MLA_HEADS, batch * seq, MLA_V), BF16),
        compiler_params=_compiler_params(("parallel", "parallel", "arbitrary")),
        name="attention",
    )(qt, kn, kr, vt, kv_m, kr_m, kv_m)
    if q_m is None:
        return o_r, None
    hg = ATT_META_HEADS
    n_hg = MLA_HEADS // hg
    o_m = pl.pallas_call(
        functools.partial(_attn_meta_kernel, tk=_tile(seq, ATT_META_TK), heads=hg),
        grid=(batch, n_hg),
        in_specs=[
            pl.BlockSpec((N_META, hg * MXU_DIM), lambda b, g: (b, g)),
            pl.BlockSpec((hg, seq, MLA_NOPE), lambda b, g: (g, b, 0)),
            pl.BlockSpec((seq, LANES), lambda b, g: (b, 0)),
            pl.BlockSpec((hg * MLA_V, seq), lambda b, g: (g, b)),
            pl.BlockSpec((N_META, hg * MLA_NOPE), lambda b, g: (b, g)),
            pl.BlockSpec((N_META, LANES), lambda b, g: (b, 0)),
            pl.BlockSpec((N_META, hg * MLA_V), lambda b, g: (b, n_hg + g)),
        ],
        out_specs=pl.BlockSpec((hg, N_META, MLA_V), lambda b, g: (g, b, 0)),
        out_shape=jax.ShapeDtypeStruct((MLA_HEADS, batch * N_META, MLA_V), BF16),
        compiler_params=_compiler_params(("parallel", "arbitrary")),
        name="attention_meta",
    )(q_m, kn, kr, vt, kv_m, kr_m, kv_m)
    return o_r, o_m


def _rope_tables(n, dim):
    inv = 1.0 / (ROPE_BASE ** (jnp.arange(0, dim, 2, dtype=F32) / dim))
    ang = jnp.arange(n, dtype=F32)[:, None] * inv[None, :]
    return jnp.cos(ang), jnp.sin(ang)


def _swap_halves(w):
    half = w.shape[-1] // 2
    return jnp.concatenate([w[..., half:], w[..., :half]], axis=-1)


def _prep_weights(p):
    d = D_MODEL
    out = {}
    out["ret_qkvg"] = [
        jnp.concatenate([p["ret_wq"][j], p["ret_wk"][j] * (RET_DK ** -0.5),
                         p["ret_wv"][j], p["ret_wg"][j]], axis=1).astype(BF16)
        for j in range(p["ret_wq"].shape[0])]
    out["ret_wo"] = [p["ret_wo"][j].astype(BF16) for j in range(p["ret_wo"].shape[0])]
    keys = ("mla_a", "mla_qb", "mla_qb_t", "mla_kvb", "mla_kn", "mla_v_t", "mla_wo")
    for key in keys:
        out[key] = []
    zeros = jnp.zeros((d, LANES - MLA_ROPE), F32)
    for j in range(p["mla_wq_a"].shape[0]):
        wkv_a = p["mla_wkv_a"][j]
        wr = wkv_a[:, MLA_KV_LORA:]
        out["mla_a"].append(jnp.concatenate(
            [p["mla_wq_a"][j], wkv_a[:, :MLA_KV_LORA], wr, zeros, _swap_halves(wr), zeros],
            axis=1).astype(BF16))
        wq_b = p["mla_wq_b"][j].reshape(MLA_Q_LORA, MLA_HEADS, MLA_NOPE + MLA_ROPE)
        rope = wq_b[..., MLA_NOPE:]
        qb = jnp.concatenate([wq_b[..., :MLA_NOPE], rope, _swap_halves(rope)], axis=-1)
        qb = qb.reshape(MLA_Q_LORA, MLA_HEADS * MXU_DIM).astype(BF16)
        out["mla_qb"].append(qb)
        out["mla_qb_t"].append(qb.T)
        wkv_b = p["mla_wkv_b"][j].reshape(MLA_KV_LORA, MLA_HEADS, MLA_NOPE + MLA_V)
        kn = wkv_b[..., :MLA_NOPE].reshape(MLA_KV_LORA, MLA_HEADS * MLA_NOPE).astype(BF16)
        vv = wkv_b[..., MLA_NOPE:].reshape(MLA_KV_LORA, MLA_HEADS * MLA_V).astype(BF16)
        out["mla_kvb"].append(jnp.concatenate([kn, vv], axis=1))
        out["mla_kn"].append(kn)
        out["mla_v_t"].append(vv.T)
        out["mla_wo"].append(p["mla_wo"][j].astype(BF16))
    out["mlp_w1"] = [p["mlp_w1"][i].astype(BF16) for i in range(p["mlp_w1"].shape[0])]
    out["mlp_w2"] = [p["mlp_w2"][i].astype(BF16) for i in range(p["mlp_w2"].shape[0])]
    return out


def _position_tables(batch, seq):
    n = seq + N_META
    cos_r, sin_r = _rope_tables(n, RET_DK)
    cos_m, sin_m = _rope_tables(n, MLA_ROPE)
    pad = jnp.zeros((n, LANES - MLA_ROPE), F32)
    cc = jnp.concatenate([cos_m, cos_m, pad], axis=1)
    ss = jnp.concatenate([-sin_m, sin_m, pad], axis=1)

    def split(tbl):
        return tbl[N_META:], jnp.tile(tbl[:N_META], (batch, 1))

    return {"ret_cos": split(cos_r), "ret_sin": split(sin_r),
            "mla_cc": split(cc), "mla_ss": split(ss),
            "mla_cc_t": cc[N_META:].T, "mla_ss_t": ss[N_META:].T}


def _trunk(x, p, w, ret_tables):
    batch, seq, d = x.shape
    pos = _position_tables(batch, seq)
    h_r = x.reshape(batch * seq, d)
    h_m = jnp.broadcast_to(p["meta_tokens"].astype(F32)[None], (batch, N_META, d)).reshape(batch * N_META, d)
    depth = p["norm1_g"].shape[0]
    q_scale = math.log2(math.e) * (MLA_NOPE + MLA_ROPE) ** -0.5

    def both(fn, last=False):
        return fn(0), (None if last else fn(1))

    tms = (_tile(seq, ROW_TILE), batch * N_META)
    pers = (seq // tms[0], 1)
    n_rows = (batch * seq, batch * N_META)

    for i in range(depth):
        last = i == depth - 1
        j = i // 2
        hs = (h_r, h_m)
        if i % 2 == 0:
            w_in = w["ret_qkvg"][j]
            tn = _col_tile(d, 2 * RET_HEADS * RET_DK, **_NORM_IN)
            ep = [(2 * RET_HEADS * RET_DK // tn, _ep_ret_rope), (RET_HEADS * RET_DV // tn, _ep_slabs),
                  (RET_HEADS * RET_DV // tn, _ep_ret_gate)]

            def qkvg(which):
                tm, per = tms[which], pers[which]
                o_specs, o_shape = _slab_out(w_in.shape[1] // RET_SLAB, n_rows[which], RET_SLAB,
                                             tn // RET_SLAB, tm)
                return _linear(hs[which], w_in, name="ret_qkvg", tm=tm, tn=tn, epilogue=ep,
                               norm_g=p["norm1_g"][i], out_specs=o_specs, out_shape=o_shape,
                               extras=(pos["ret_cos"][which], pos["ret_sin"][which]),
                               extra_specs=(_row_spec(tm, LANES, per), _row_spec(tm, LANES, per)))[0]

            qkvg_r, qkvg_m = both(qkvg)
            mix_r, mix_m = _retention(qkvg_r, qkvg_m, batch, seq, ret_tables)
            w_out = w["ret_wo"][j]
        else:
            w_a = w["mla_a"][j]

            def stage_a(which):
                tm, per = tms[which], pers[which]
                one = lambda width: pl.BlockSpec((1, width), lambda i_, j_: (0, 0))
                return _linear(hs[which], w_a, name="mla_a", tm=tm, tn=w_a.shape[1], epilogue=_ep_mla_a,
                               out_widths=[MLA_Q_LORA, MLA_KV_LORA, LANES], out_dtypes=[BF16] * 3,
                               out_col_tiles=[MLA_Q_LORA, MLA_KV_LORA, LANES],
                               norm_g=p["norm1_g"][i],
                               extras=(p["mla_q_norm"][j].reshape(1, -1), p["mla_kv_norm"][j].reshape(1, -1),
                                       pos["mla_cc"][which], pos["mla_ss"][which]),
                               extra_specs=(one(MLA_Q_LORA), one(MLA_KV_LORA),
                                            _row_spec(tm, LANES, per), _row_spec(tm, LANES, per)))

            (cq_r, ckv_r, kr_r), (cq_m, ckv_m, kr_m) = both(stage_a)
            tm_r = tms[0]
            tm_t = _tile(seq, TRANSPOSED_ROW_TILE)
            per_t = seq // tm_t
            t_spec = pl.BlockSpec((LANES, tm_t), lambda i_, j_: (0, i_ % per_t))
            qt_r = _linear_t(cq_r, w["mla_qb_t"][j], name="mla_qb_t", tm=tm_t,
                             epilogue=functools.partial(_ep_t_mla_q, scale=q_scale),
                             extras=(pos["mla_cc_t"], pos["mla_ss_t"]), extra_specs=(t_spec, t_spec))
            tn_kn = _col_tile(MLA_KV_LORA, MLA_HEADS * MLA_NOPE, **_PLAIN)
            kn_specs, kn_shape = _slab_out(MLA_HEADS, n_rows[0], MLA_NOPE, tn_kn // MLA_NOPE, tm_r)
            kn_r = _linear(ckv_r, w["mla_kn"][j], name="mla_kn", tm=tm_r, tn=tn_kn, epilogue=_ep_slabs,
                           out_specs=kn_specs, out_shape=kn_shape)[0]
            vt_r = _linear_t(ckv_r, w["mla_v_t"][j], name="mla_v_t", tm=tm_t, epilogue=_ep_t_plain)
            tm_m = tms[1]
            kv_m = _linear(ckv_m, w["mla_kvb"][j], name="mla_kvb", tm=tm_m, tn=MAX_COL_TILE,
                           epilogue=_ep_plain, out_widths=[MLA_HEADS * (MLA_NOPE + MLA_V)],
                           out_dtypes=[BF16])[0]
            q_m = None
            if not last:
                m_spec = _row_spec(tm_m, LANES, 1)
                q_m = _linear(cq_m, w["mla_qb"][j], name="mla_qb", tm=tm_m, tn=MAX_COL_TILE,
                              epilogue=functools.partial(_ep_mla_q, scale=q_scale),
                              out_widths=[MLA_HEADS * MXU_DIM], out_dtypes=[BF16],
                              extras=(pos["mla_cc"][1], pos["mla_ss"][1]), extra_specs=(m_spec, m_spec))[0]
            mix_r, mix_m = _attention(qt_r, q_m, kn_r, kr_r, vt_r, kv_m, kr_m, batch, seq)
            w_out = w["mla_wo"][j]

        mixes = (mix_r, mix_m)
        tm_out = _tile(seq, ROW_TILE // 2) if w_out.shape[0] <= OUT_RESIDENT_K else tms[0]
        tn_out = _col_tile(w_out.shape[0], d, tm=tm_out, **_RES_OUT)

        def proj_out(which):
            tm = tm_out if which == 0 else tms[1]
            return _linear(mixes[which], w_out, name="mix_out", tm=tm, tn=tn_out, epilogue=_ep_residual,
                           out_widths=[d], out_dtypes=[F32], extras=(hs[which],),
                           extra_specs=(pl.BlockSpec((tm, tn_out), lambda i_, j_: (i_, j_)),))[0]

        h_r, h_m = both(proj_out, last)
        hs = (h_r, h_m)
        tn_up = _col_tile(d, D_FF, **_NORM_IN)

        def mlp(which):
            tm = tms[which]
            hid = _linear(hs[which], w["mlp_w1"][i], name="mlp_up", tm=tm, tn=tn_up, epilogue=_ep_relu2,
                          out_widths=[D_FF], out_dtypes=[BF16], norm_g=p["norm2_g"][i])[0]
            return _mlp_down(hid, w["mlp_w2"][i], hs[which], tm=tm,
                             final_g=p["final_norm"] if last else None)

        h_r, h_m = both(mlp, last)

    return h_r.reshape(batch, seq, d)


def kernel(x_prompt, x_sample, meta_tokens, norm1_g, norm2_g, mlp_w1, mlp_w2, ret_wq, ret_wk, ret_wv, ret_wg, ret_wo, mla_wq_a, mla_q_norm, mla_wq_b, mla_wkv_a, mla_kv_norm, mla_wkv_b, mla_wo, final_norm):
    p = dict(meta_tokens=meta_tokens, norm1_g=norm1_g, norm2_g=norm2_g, mlp_w1=mlp_w1, mlp_w2=mlp_w2,
             ret_wq=ret_wq, ret_wk=ret_wk, ret_wv=ret_wv, ret_wg=ret_wg, ret_wo=ret_wo,
             mla_wq_a=mla_wq_a, mla_q_norm=mla_q_norm, mla_wq_b=mla_wq_b, mla_wkv_a=mla_wkv_a,
             mla_kv_norm=mla_kv_norm, mla_wkv_b=mla_wkv_b, mla_wo=mla_wo, final_norm=final_norm)
    w = _prep_weights(p)
    ret_tables = _retention_tables()
    return (_trunk(x_prompt, p, w, ret_tables), _trunk(x_sample, p, w, ret_tables))
```
